```python
import math
import jax, jax.numpy as jnp
from jax import lax
import numpy as np

D_MODEL = 4096
BATCH = 1
SEQ = 16384
DEPTH = 2

HEAD_DIM = 128
A_PATTERNS = ((128, 1), (512, 4), (2048, 16))
A_HEADS = 8
N_PAT = len(A_PATTERNS)
B_HEADS = 12
B_KV_HEADS = 4
B_HALF = 128
C_HEADS = 12
C_QK_DIM = 64
C_V_DIM = 2 * C_QK_DIM
Q_BLOCK = 128
D_FF = 4 * D_MODEL
N_BRANCH = 3
EPS = 1e-6
NEG = -1e30

A_COLS = N_PAT * 3 * A_HEADS * HEAD_DIM
B_COLS = (B_HEADS + 2 * B_KV_HEADS) * HEAD_DIM
C_COLS = C_HEADS * (4 * C_QK_DIM + C_V_DIM)
GATE_COLS = N_BRANCH * D_MODEL
OFF_B = A_COLS
OFF_C = OFF_B + B_COLS
OFF_G = OFF_C + C_COLS
P_IN = OFF_G + GATE_COLS
A_OUT = A_HEADS * HEAD_DIM
B_OUT = B_HEADS * HEAD_DIM
C_OUT = C_HEADS * C_V_DIM

kernel_name = 'hybrid_gated_dilated_window_diff_encoder'


def _rmsnorm(x, g):
    x32 = x.astype(jnp.float32)
    y = x32 * lax.rsqrt(jnp.mean(x32 * x32, axis=-1, keepdims=True) + EPS)
    return (y * g.astype(jnp.float32)).astype(x.dtype)


def _alibi_slopes(n):
    return jnp.exp2(-8.0 * jnp.arange(1, n + 1, dtype=jnp.float32) / n)


def _banded_attention(q, k, v, half, slopes, step, sinks=None):
    N, L, Hkv, G, dh = q.shape
    blk = half
    nb = -(-L // blk)
    Lp = nb * blk
    pad = Lp - L
    qp = jnp.pad(q, ((0, 0), (0, pad), (0, 0), (0, 0), (0, 0)))
    kp = jnp.pad(k, ((0, 0), (blk, pad + blk), (0, 0), (0, 0)))
    vp = jnp.pad(v, ((0, 0), (blk, pad + blk), (0, 0), (0, 0)))
    qb = qp.reshape(N, nb, blk, Hkv, G, dh)
    kb = kp.reshape(N, nb + 2, blk, Hkv, dh)
    vb = vp.reshape(N, nb + 2, blk, Hkv, dh)
    kw = jnp.concatenate([kb[:, :-2], kb[:, 1:-1], kb[:, 2:]], axis=2)
    vw = jnp.concatenate([vb[:, :-2], vb[:, 1:-1], vb[:, 2:]], axis=2)
    qpos = jnp.arange(Lp).reshape(nb, blk)
    kpos = (jnp.arange(nb)[:, None] - 1) * blk + jnp.arange(3 * blk)[None, :]
    dist = jnp.abs(qpos[:, :, None] - kpos[:, None, :])
    valid = (dist <= half) & (kpos[:, None, :] >= 0) & (kpos[:, None, :] < L)
    s = jnp.einsum('nbqhgd,nbkhd->nbhgqk', qb, kw).astype(jnp.float32) * (dh ** -0.5)
    s = s - slopes[None, None, :, :, None, None] * (step * dist).astype(jnp.float32)[None, :, None, None, :, :]
    s = jnp.where(valid[None, :, None, None, :, :], s, NEG)
    m = jnp.max(s, axis=-1, keepdims=True)
    if sinks is not None:
        sk = sinks.astype(jnp.float32)[None, None, :, :, None, None]
        m = jnp.maximum(m, sk)
    e = jnp.exp(s - m)
    denom = jnp.sum(e, axis=-1, keepdims=True)
    if sinks is not None:
        denom = denom + jnp.exp(sk - m)
    o = jnp.einsum('nbhgqk,nbkhd->nbqhgd', (e / denom).astype(v.dtype), vw)
    o = o.reshape(N, Lp, Hkv, G, dh)[:, :L]
    lse = (m + jnp.log(denom))[..., 0]
    lse = lse.transpose(0, 1, 4, 2, 3).reshape(N, Lp, Hkv, G)[:, :L]
    return o, lse


def _to_strided(t, dil):
    B, S, H, dh = t.shape
    return t.reshape(B, S // dil, dil, H, dh).transpose(0, 2, 1, 3, 4).reshape(B * dil, S // dil, H, dh)


def _from_strided(t, B, dil):
    rest = t.shape[2:]
    Ls = t.shape[1]
    t = t.reshape((B, dil, Ls) + rest)
    t = jnp.swapaxes(t, 1, 2)
    return t.reshape((B, Ls * dil) + rest)


def _dilated_attention(pa, slopes):
    B, S = pa.shape[:2]
    outs, lses = [], []
    for g, (window, dil) in enumerate(A_PATTERNS):
        half = window // (2 * dil)
        q = _to_strided(pa[:, :, g, 0], dil)[:, :, :, None]
        k = _to_strided(pa[:, :, g, 1], dil)
        v = _to_strided(pa[:, :, g, 2], dil)
        o, lse = _banded_attention(q, k, v, half, slopes[:, None], dil)
        outs.append(_from_strided(o[:, :, :, 0], B, dil))
        lses.append(_from_strided(lse[:, :, :, 0], B, dil))
    w = jax.nn.softmax(jnp.stack(lses, axis=0), axis=0)
    o = jnp.sum(w[..., None] * jnp.stack(outs, axis=0).astype(jnp.float32), axis=0)
    return o.astype(pa.dtype).reshape(B, S, A_OUT)


def _diff_attention(q, k, v, lam, slopes):
    B, S, H, _, dk = q.shape
    nq = S // Q_BLOCK
    qb = q.reshape(B, nq, Q_BLOCK, H, 2, dk).transpose(1, 0, 2, 3, 4, 5)
    kpos = jnp.arange(S)

    def block(args):
        qi, idx = args
        qpos = idx * Q_BLOCK + jnp.arange(Q_BLOCK)
        dist = jnp.abs(qpos[:, None] - kpos[None, :]).astype(jnp.float32)
        s = jnp.einsum('bqhmd,bkhmd->bhmqk', qi, k).astype(jnp.float32) * (dk ** -0.5)
        s = s - slopes[None, :, None, None, None] * dist[None, None, None]
        p = jax.nn.softmax(s, axis=-1)
        a = p[:, :, 0] - lam * p[:, :, 1]
        return jnp.einsum('bhqk,bkhd->bqhd', a.astype(v.dtype), v)

    o = lax.map(block, (qb, jnp.arange(nq)))
    return o.transpose(1, 0, 2, 3, 4).reshape(B, S, H, v.shape[-1])


def setup_inputs(seed: int = 0) -> dict:
    key = jax.random.key(seed)
    ks = jax.random.split(key, 18)
    f = jnp.float32
    nrm = lambda k, shape, scale: jax.random.normal(k, shape, f) * scale
    return {
        'x': nrm(ks[0], (BATCH, SEQ, D_MODEL), 1.0),
        'mix_norm_g': 1.0 + nrm(ks[1], (DEPTH, D_MODEL), 0.02),
        'w_in': nrm(ks[2], (DEPTH, D_MODEL, P_IN), D_MODEL ** -0.5),
        'b_sink': nrm(ks[3], (DEPTH, B_HEADS), 0.5),
        'diff_lq1': nrm(ks[4], (DEPTH, C_QK_DIM), 0.1),
        'diff_lk1': nrm(ks[5], (DEPTH, C_QK_DIM), 0.1),
        'diff_lq2': nrm(ks[6], (DEPTH, C_QK_DIM), 0.1),
        'diff_lk2': nrm(ks[7], (DEPTH, C_QK_DIM), 0.1),
        'diff_norm_g': 1.0 + nrm(ks[8], (DEPTH, C_V_DIM), 0.02),
        'w_branch_a': nrm(ks[9], (DEPTH, A_OUT, D_MODEL), A_OUT ** -0.5),
        'w_branch_b': nrm(ks[10], (DEPTH, B_OUT, D_MODEL), B_OUT ** -0.5),
        'w_branch_c': nrm(ks[11], (DEPTH, C_OUT, D_MODEL), C_OUT ** -0.5),
        'w_out': nrm(ks[12], (DEPTH, D_MODEL, D_MODEL), D_MODEL ** -0.5),
        'mlp_norm_g': 1.0 + nrm(ks[13], (DEPTH, D_MODEL), 0.02),
        'w_up': nrm(ks[14], (DEPTH, D_MODEL, D_FF), D_MODEL ** -0.5),
        'w_down': nrm(ks[15], (DEPTH, D_FF, D_MODEL), D_FF ** -0.5),
        'final_norm_g': 1.0 + nrm(ks[16], (D_MODEL,), 0.02),
    }


def reference(x, mix_norm_g, w_in, b_sink, diff_lq1, diff_lk1, diff_lq2, diff_lk2, diff_norm_g,
              w_branch_a, w_branch_b, w_branch_c, w_out, mlp_norm_g, w_up, w_down, final_norm_g):
    B, S, D = x.shape
    slopes_a = _alibi_slopes(A_HEADS)
    slopes_b = _alibi_slopes(B_HEADS).reshape(B_KV_HEADS, B_HEADS // B_KV_HEADS)
    slopes_c = _alibi_slopes(C_HEADS)
    qb_cols = B_HEADS * HEAD_DIM
    kb_cols = B_KV_HEADS * HEAD_DIM
    qc_cols = C_HEADS * 2 * C_QK_DIM
    h = x
    for l in range(DEPTH):
        u = _rmsnorm(h, mix_norm_g[l])
        w = w_in[l]
        pa = (u @ w[:, :OFF_B]).reshape(B, S, N_PAT, 3, A_HEADS, HEAD_DIM)
        o_a = _dilated_attention(pa, slopes_a)
        pb = u @ w[:, OFF_B:OFF_C]
        q_b = pb[..., :qb_cols].reshape(B, S, B_KV_HEADS, B_HEADS // B_KV_HEADS, HEAD_DIM)
        k_b = pb[..., qb_cols:qb_cols + kb_cols].reshape(B, S, B_KV_HEADS, HEAD_DIM)
        v_b = pb[..., qb_cols + kb_cols:].reshape(B, S, B_KV_HEADS, HEAD_DIM)
        o_b, _ = _banded_attention(q_b, k_b, v_b, B_HALF, slopes_b, 1,
                                   b_sink[l].reshape(B_KV_HEADS, B_HEADS // B_KV_HEADS))
        o_b = o_b.reshape(B, S, B_OUT)
        pc = u @ w[:, OFF_C:OFF_G]
        q_c = pc[..., :qc_cols].reshape(B, S, C_HEADS, 2, C_QK_DIM)
        k_c = pc[..., qc_cols:2 * qc_cols].reshape(B, S, C_HEADS, 2, C_QK_DIM)
        v_c = pc[..., 2 * qc_cols:].reshape(B, S, C_HEADS, C_V_DIM)
        lam_init = 0.8 - 0.6 * math.exp(-0.3 * l)
        lam = (jnp.exp(jnp.sum(diff_lq1[l].astype(jnp.float32) * diff_lk1[l].astype(jnp.float32)))
               - jnp.exp(jnp.sum(diff_lq2[l].astype(jnp.float32) * diff_lk2[l].astype(jnp.float32)))
               + lam_init)
        o_c = _diff_attention(q_c, k_c, v_c, lam, slopes_c)
        o_c = (_rmsnorm(o_c, diff_norm_g[l]) * (1.0 - lam_init)).reshape(B, S, C_OUT)
        gates = u @ w[:, OFF_G:]
        merged = (jax.nn.sigmoid(gates[..., :D]) * (o_a @ w_branch_a[l])
                  + jax.nn.sigmoid(gates[..., D:2 * D]) * (o_b @ w_branch_b[l])
                  + jax.nn.sigmoid(gates[..., 2 * D:]) * (o_c @ w_branch_c[l]))
        h = h + merged @ w_out[l]
        v = _rmsnorm(h, mlp_norm_g[l])
        h = h + jnp.square(jax.nn.relu(v @ w_up[l])) @ w_down[l]
    return _rmsnorm(h, final_norm_g)
```

```python
import functools
import math

import jax
import jax.numpy as jnp
from jax import lax
from jax.experimental import pallas as pl
from jax.experimental.pallas import tpu as pltpu

HEAD_DIM = 128
A_PATTERNS = ((128, 1), (512, 4), (2048, 16))
A_HEADS = 8
N_PAT = len(A_PATTERNS)
B_HEADS = 12
B_KV_HEADS = 4
B_GROUP = B_HEADS // B_KV_HEADS
B_HALF = 128
C_HEADS = 12
C_QK_DIM = 64
C_V_DIM = 2 * C_QK_DIM
N_BRANCH = 3
EPS = 1e-6
NEG = -1e30

A_COLS = N_PAT * 3 * A_HEADS * HEAD_DIM
B_COLS = (B_HEADS + 2 * B_KV_HEADS) * HEAD_DIM
C_COLS = C_HEADS * (4 * C_QK_DIM + C_V_DIM)
OFF_B = A_COLS
OFF_C = OFF_B + B_COLS
OFF_G = OFF_C + C_COLS
A_OUT = A_HEADS * HEAD_DIM
B_OUT = B_HEADS * HEAD_DIM
C_OUT = C_HEADS * C_V_DIM

LANE = 128
VMEM_LIMIT = 56 * 1024 * 1024


def _tile(n, pref):
    t = pref
    while t > 1 and n % t:
        t //= 2
    return t


def _params(sem):
    return pltpu.CompilerParams(dimension_semantics=sem, vmem_limit_bytes=VMEM_LIMIT)


def _alibi_slopes(n):
    return [2.0 ** (-8.0 * i / n) for i in range(1, n + 1)]


def _rmsnorm_kernel(x_ref, g_ref, o_ref):
    x = x_ref[...]
    y = x * lax.rsqrt(jnp.mean(x * x, axis=-1, keepdims=True) + EPS)
    o_ref[...] = (y * g_ref[...]).astype(o_ref.dtype)


def _rmsnorm(x, g, out_dtype):
    s, d = x.shape
    tr = _tile(s, 256)
    return pl.pallas_call(
        _rmsnorm_kernel,
        out_shape=jax.ShapeDtypeStruct((s, d), out_dtype),
        grid=(s // tr,),
        in_specs=[pl.BlockSpec((tr, d), lambda i: (i, 0)),
                  pl.BlockSpec((1, d), lambda i: (0, 0))],
        out_specs=pl.BlockSpec((tr, d), lambda i: (i, 0)),
        compiler_params=_params(("parallel",)),
        name="rmsnorm",
    )(x, g.reshape(1, d))


def _matmul_kernel(x_ref, w_ref, o_ref, *, epilogue):
    acc = jnp.dot(x_ref[...], w_ref[...], preferred_element_type=jnp.float32)
    if epilogue == "sigmoid":
        acc = jax.nn.sigmoid(acc)
    o_ref[...] = acc.astype(o_ref.dtype)


def _matmul_res_kernel(x_ref, w_ref, r_ref, o_ref):
    acc = jnp.dot(x_ref[...], w_ref[...], preferred_element_type=jnp.float32)
    o_ref[...] = r_ref[...] + acc


def _matmul(x, w, layer, col_off, ncols, out_dtype, epilogue="none", residual=None, name="matmul"):
    m, k = x.shape
    tm = _tile(m, 1024)
    tn = _tile(math.gcd(ncols, col_off) if col_off else ncols, 1024)
    off = col_off // tn
    in_specs = [pl.BlockSpec((tm, k), lambda i, j: (i, 0)),
                pl.BlockSpec((None, k, tn), lambda i, j: (layer, 0, off + j))]
    args = [x, w]
    if residual is None:
        kern = functools.partial(_matmul_kernel, epilogue=epilogue)
    else:
        kern = _matmul_res_kernel
        in_specs.append(pl.BlockSpec((tm, tn), lambda i, j: (i, j)))
        args.append(residual)
    return pl.pallas_call(
        kern,
        out_shape=jax.ShapeDtypeStruct((m, ncols), out_dtype),
        grid=(m // tm, ncols // tn),
        in_specs=in_specs,
        out_specs=pl.BlockSpec((tm, tn), lambda i, j: (i, j)),
        compiler_params=_params(("parallel", "parallel")),
        name=name,
    )(*args)


def _merge_kernel(oa_ref, ob_ref, oc_ref, wa_ref, wb_ref, wc_ref, ga_ref, gb_ref, gc_ref, o_ref):
    f32 = jnp.float32
    ya = jnp.dot(oa_ref[...], wa_ref[...], preferred_element_type=f32)
    yb = jnp.dot(ob_ref[...], wb_ref[...], preferred_element_type=f32)
    yc = jnp.dot(oc_ref[...], wc_ref[...], preferred_element_type=f32)
    merged = (ga_ref[...].astype(f32) * ya + gb_ref[...].astype(f32) * yb
              + gc_ref[...].astype(f32) * yc)
    o_ref[...] = merged.astype(o_ref.dtype)


def _merge(o_a, o_b, o_c, gates, w_a, w_b, w_c, layer):
    s = o_a.shape[0]
    d = w_a.shape[-1]
    tm = _tile(s, 1024)
    tn = _tile(d, 512)
    nj = d // tn
    row = lambda width: pl.BlockSpec((tm, width), lambda i, j: (i, 0))
    wspec = lambda kk: pl.BlockSpec((None, kk, tn), lambda i, j: (layer, 0, j))
    gspec = lambda b: pl.BlockSpec((tm, tn), lambda i, j: (i, b * nj + j))
    return pl.pallas_call(
        _merge_kernel,
        out_shape=jax.ShapeDtypeStruct((s, d), jnp.bfloat16),
        grid=(s // tm, nj),
        in_specs=[row(A_OUT), row(B_OUT), row(C_OUT),
                  wspec(A_OUT), wspec(B_OUT), wspec(C_OUT),
                  gspec(0), gspec(1), gspec(2)],
        out_specs=pl.BlockSpec((tm, tn), lambda i, j: (i, j)),
        compiler_params=_params(("parallel", "parallel")),
        name="merge",
    )(o_a, o_b, o_c, w_a, w_b, w_c, gates, gates, gates)


def _mlp_kernel(v_ref, wu_ref, wd_ref, h_ref, o_ref):
    f = pl.program_id(1)

    @pl.when(f == 0)
    def _():
        o_ref[...] = h_ref[...]

    a = jnp.dot(v_ref[...], wu_ref[...], preferred_element_type=jnp.float32)
    a = jnp.square(jnp.maximum(a, 0.0)).astype(jnp.bfloat16)
    o_ref[...] += jnp.dot(a, wd_ref[...], preferred_element_type=jnp.float32)


def _mlp(v, h, w_up, w_down, layer):
    s, d = v.shape
    dff = w_up.shape[-1]
    tm = _tile(s, 512)
    tf = _tile(dff, 512)
    return pl.pallas_call(
        _mlp_kernel,
        out_shape=jax.ShapeDtypeStruct((s, d), jnp.float32),
        grid=(s // tm, dff // tf),
        in_specs=[pl.BlockSpec((tm, d), lambda i, f: (i, 0)),
                  pl.BlockSpec((None, d, tf), lambda i, f: (layer, 0, f)),
                  pl.BlockSpec((None, tf, d), lambda i, f: (layer, f, 0)),
                  pl.BlockSpec((tm, d), lambda i, f: (i, 0), pipeline_mode=pl.Buffered(1))],
        out_specs=pl.BlockSpec((tm, d), lambda i, f: (i, 0)),
        compiler_params=_params(("parallel", "arbitrary")),
        name="mlp",
    )(v, w_up, w_down, h)


A_TQ = 256
A_BLK = 64


def _attn_a_kernel(q_ref, kp_ref, km_ref, kn_ref, vp_ref, vm_ref, vn_ref, o_ref, lse_ref, *, dil, tq):
    i = pl.program_id(1)
    last = pl.num_programs(1) - 1
    nk = tq + 2 * A_BLK
    a = lax.broadcasted_iota(jnp.int32, (tq, nk), 0)
    b = lax.broadcasted_iota(jnp.int32, (tq, nk), 1)
    dist = jnp.abs(a - (b - A_BLK))
    valid = (dist <= A_BLK) & ((b >= A_BLK) | (i > 0)) & ((b < tq + A_BLK) | (i < last))
    distf = (dil * dist).astype(jnp.float32)
    lane = lax.broadcasted_iota(jnp.int32, (tq, LANE), 1)
    lse_tile = jnp.zeros((tq, LANE), jnp.float32)
    slopes = _alibi_slopes(A_HEADS)
    for h in range(A_HEADS):
        sl = slice(h * HEAD_DIM, (h + 1) * HEAD_DIM)
        q = q_ref[:, sl]
        k = jnp.concatenate([kp_ref[:, sl], km_ref[:, sl], kn_ref[:, sl]], axis=0)
        v = jnp.concatenate([vp_ref[:, sl], vm_ref[:, sl], vn_ref[:, sl]], axis=0)
        s = lax.dot_general(q, k, (((1,), (1,)), ((), ())), preferred_element_type=jnp.float32)
        s = s * (HEAD_DIM ** -0.5) - slopes[h] * distf
        s = jnp.where(valid, s, NEG)
        m = jnp.max(s, axis=-1, keepdims=True)
        e = jnp.exp(s - m)
        denom = jnp.sum(e, axis=-1, keepdims=True)
        p = (e / denom).astype(jnp.bfloat16)
        o_ref[:, sl] = jnp.dot(p, v, preferred_element_type=jnp.float32)
        lse_tile = jnp.where(lane == h, m + jnp.log(denom), lse_tile)
    lse_ref[...] = lse_tile


def _attn_a_pattern(qkv, g, dil):
    s, nc = qkv.shape
    ls = s // dil
    tq = _tile(ls, A_TQ)
    ncb = nc // A_OUT
    view = qkv.reshape(ls, dil * nc)
    r64 = tq // A_BLK
    nblk64 = ls // A_BLK
    cq, ck, cv = g * 3, g * 3 + 1, g * 3 + 2

    def main(c):
        return pl.BlockSpec((tq, A_OUT), lambda r, i: (i, r * ncb + c))

    def prev(c):
        return pl.BlockSpec((A_BLK, A_OUT), lambda r, i: (jnp.maximum(i * r64 - 1, 0), r * ncb + c))

    def nxt(c):
        return pl.BlockSpec((A_BLK, A_OUT),
                            lambda r, i: (jnp.minimum((i + 1) * r64, nblk64 - 1), r * ncb + c))

    o, lse = pl.pallas_call(
        functools.partial(_attn_a_kernel, dil=dil, tq=tq),
        out_shape=(jax.ShapeDtypeStruct((ls, dil * A_OUT), jnp.float32),
                   jax.ShapeDtypeStruct((ls, dil * LANE), jnp.float32)),
        grid=(dil, ls // tq),
        in_specs=[main(cq), prev(ck), main(ck), nxt(ck), prev(cv), main(cv), nxt(cv)],
        out_specs=(pl.BlockSpec((tq, A_OUT), lambda r, i: (i, r)),
                   pl.BlockSpec((tq, LANE), lambda r, i: (i, r))),
        compiler_params=_params(("parallel", "parallel")),
        name=f"attn_a{g}",
    )(view, view, view, view, view, view, view)
    return o.reshape(s, A_OUT), lse.reshape(s, LANE)


def _combine_a_kernel(o0_ref, o1_ref, o2_ref, l0_ref, l1_ref, l2_ref, o_ref):
    l0, l1, l2 = l0_ref[...], l1_ref[...], l2_ref[...]
    mx = jnp.maximum(jnp.maximum(l0, l1), l2)
    e0, e1, e2 = jnp.exp(l0 - mx), jnp.exp(l1 - mx), jnp.exp(l2 - mx)
    inv = 1.0 / (e0 + e1 + e2)
    w0, w1, w2 = e0 * inv, e1 * inv, e2 * inv
    for h in range(A_HEADS):
        sl = slice(h * HEAD_DIM, (h + 1) * HEAD_DIM)
        o = (w0[:, h:h + 1] * o0_ref[:, sl] + w1[:, h:h + 1] * o1_ref[:, sl]
             + w2[:, h:h + 1] * o2_ref[:, sl])
        o_ref[:, sl] = o.astype(o_ref.dtype)


def _attn_a(qkv):
    s = qkv.shape[0]
    outs = [_attn_a_pattern(qkv, g, dil) for g, (_, dil) in enumerate(A_PATTERNS)]
    tr = _tile(s, 512)
    ospec = pl.BlockSpec((tr, A_OUT), lambda i: (i, 0))
    lspec = pl.BlockSpec((tr, LANE), lambda i: (i, 0))
    return pl.pallas_call(
        _combine_a_kernel,
        out_shape=jax.ShapeDtypeStruct((s, A_OUT), jnp.bfloat16),
        grid=(s // tr,),
        in_specs=[ospec, ospec, ospec, lspec, lspec, lspec],
        out_specs=ospec,
        compiler_params=_params(("parallel",)),
        name="combine_a",
    )(outs[0][0], outs[1][0], outs[2][0], outs[0][1], outs[1][1], outs[2][1])


B_TQ = 256


def _attn_b_kernel(slope_ref, sink_ref, q_ref, kp_ref, km_ref, kn_ref, vp_ref, vm_ref, vn_ref, o_ref, *, tq):
    c = pl.program_id(0)
    i = pl.program_id(1)
    last = pl.num_programs(1) - 1
    nk = tq + 2 * B_HALF
    a = lax.broadcasted_iota(jnp.int32, (tq, nk), 0)
    b = lax.broadcasted_iota(jnp.int32, (tq, nk), 1)
    dist = jnp.abs(a - (b - B_HALF))
    valid = (dist <= B_HALF) & ((b >= B_HALF) | (i > 0)) & ((b < tq + B_HALF) | (i < last))
    distf = dist.astype(jnp.float32)
    k = jnp.concatenate([kp_ref[...], km_ref[...], kn_ref[...]], axis=0)
    v = jnp.concatenate([vp_ref[...], vm_ref[...], vn_ref[...]], axis=0)
    for g in range(B_GROUP):
        sl = slice(g * HEAD_DIM, (g + 1) * HEAD_DIM)
        slope = slope_ref[c * B_GROUP + g]
        sink = sink_ref[c * B_GROUP + g]
        s = lax.dot_general(q_ref[:, sl], k, (((1,), (1,)), ((), ())),
                            preferred_element_type=jnp.float32)
        s = s * (HEAD_DIM ** -0.5) - slope * distf
        s = jnp.where(valid, s, NEG)
        m = jnp.maximum(jnp.max(s, axis=-1, keepdims=True), sink)
        e = jnp.exp(s - m)
        denom = jnp.sum(e, axis=-1, keepdims=True) + jnp.exp(sink - m)
        p = (e / denom).astype(jnp.bfloat16)
        o_ref[:, sl] = jnp.dot(p, v, preferred_element_type=jnp.float32).astype(o_ref.dtype)


def _attn_b(qkv, sinks):
    s, nc = qkv.shape
    tq = _tile(s, B_TQ)
    r128 = tq // B_HALF
    nblk = s // B_HALF
    gw = B_GROUP * HEAD_DIM
    cq0 = OFF_B // gw
    ck0 = (OFF_B + B_HEADS * HEAD_DIM) // HEAD_DIM
    cv0 = ck0 + B_KV_HEADS
    smem = pl.BlockSpec(memory_space=pltpu.SMEM)

    def main(c0):
        return pl.BlockSpec((tq, HEAD_DIM), lambda c, i: (i, c0 + c))

    def prev(c0):
        return pl.BlockSpec((B_HALF, HEAD_DIM), lambda c, i: (jnp.maximum(i * r128 - 1, 0), c0 + c))

    def nxt(c0):
        return pl.BlockSpec((B_HALF, HEAD_DIM),
                            lambda c, i: (jnp.minimum((i + 1) * r128, nblk - 1), c0 + c))

    slopes = jnp.asarray(_alibi_slopes(B_HEADS), jnp.float32)
    return pl.pallas_call(
        functools.partial(_attn_b_kernel, tq=tq),
        out_shape=jax.ShapeDtypeStruct((s, B_OUT), jnp.bfloat16),
        grid=(B_KV_HEADS, s // tq),
        in_specs=[smem, smem,
                  pl.BlockSpec((tq, gw), lambda c, i: (i, cq0 + c)),
                  prev(ck0), main(ck0), nxt(ck0), prev(cv0), main(cv0), nxt(cv0)],
        out_specs=pl.BlockSpec((tq, gw), lambda c, i: (i, c)),
        compiler_params=_params(("parallel", "parallel")),
        name="attn_b",
    )(slopes, sinks.astype(jnp.float32), qkv, qkv, qkv, qkv, qkv, qkv, qkv)


C_TQ = 256
C_TK = 512


def _attn_c_kernel(slope_ref, q_ref, k_ref, v_ref, lq1_ref, lk1_ref, lq2_ref, lk2_ref, g_ref, o_ref,
                   m_ref, l_ref, acc_ref, *, tq, tk, nkv, lam_init):
    f32 = jnp.float32
    h = pl.program_id(0)
    i = pl.program_id(1)
    slope = slope_ref[h]
    q = q_ref[...]
    lane = lax.broadcasted_iota(jnp.int32, (tq, 2 * C_QK_DIM), 1)
    zero = jnp.zeros_like(q)
    qz = jnp.concatenate([jnp.where(lane < C_QK_DIM, q, zero), jnp.where(lane >= C_QK_DIM, q, zero)], axis=0)
    rel = (lax.broadcasted_iota(jnp.int32, (tq, tk), 0) - lax.broadcasted_iota(jnp.int32, (tq, tk), 1)
           + i * tq)
    m_ref[...] = jnp.full(m_ref.shape, NEG, f32)
    l_ref[...] = jnp.zeros(l_ref.shape, f32)
    acc_ref[...] = jnp.zeros(acc_ref.shape, f32)

    def body(j, carry):
        start = pl.multiple_of(j * tk, tk)
        k = k_ref[pl.ds(start, tk), :]
        v = v_ref[pl.ds(start, tk), :]
        s = lax.dot_general(qz, k, (((1,), (1,)), ((), ())), preferred_element_type=f32)
        bias = slope * jnp.abs(rel - j * tk).astype(f32)
        s = s * (C_QK_DIM ** -0.5) - jnp.concatenate([bias, bias], axis=0)
        m_old = m_ref[...]
        m_new = jnp.maximum(m_old, jnp.max(s, axis=-1, keepdims=True))
        alpha = jnp.exp(m_old - m_new)
        p = jnp.exp(s - m_new)
        l_ref[...] = alpha * l_ref[...] + jnp.sum(p, axis=-1, keepdims=True)
        acc_ref[...] = alpha * acc_ref[...] + jnp.dot(p.astype(jnp.bfloat16), v, preferred_element_type=f32)
        m_ref[...] = m_new
        return carry

    lax.fori_loop(0, nkv, body, 0)

    lam = (jnp.exp(jnp.sum(lq1_ref[...] * lk1_ref[...], axis=-1, keepdims=True))
           - jnp.exp(jnp.sum(lq2_ref[...] * lk2_ref[...], axis=-1, keepdims=True)) + lam_init)
    o = acc_ref[...] / l_ref[...]
    o = o[:tq] - lam * o[tq:]
    y = o * lax.rsqrt(jnp.mean(o * o, axis=-1, keepdims=True) + EPS)
    o_ref[...] = (y * g_ref[...] * (1.0 - lam_init)).astype(o_ref.dtype)


def _attn_c(qkv, lq1, lk1, lq2, lk2, norm_g, lam_init):
    s, nc = qkv.shape
    tq = _tile(s, C_TQ)
    tk = _tile(s, C_TK)
    cq0 = OFF_C // HEAD_DIM
    ck0 = cq0 + C_HEADS
    cv0 = ck0 + C_HEADS
    smem = pl.BlockSpec(memory_space=pltpu.SMEM)
    vec = lambda n: pl.BlockSpec((1, n), lambda h, i: (0, 0))
    slopes = jnp.asarray(_alibi_slopes(C_HEADS), jnp.float32)
    return pl.pallas_call(
        functools.partial(_attn_c_kernel, tq=tq, tk=tk, nkv=s // tk, lam_init=lam_init),
        out_shape=jax.ShapeDtypeStruct((s, C_OUT), jnp.bfloat16),
        grid=(C_HEADS, s // tq),
        in_specs=[smem,
                  pl.BlockSpec((tq, HEAD_DIM), lambda h, i: (i, cq0 + h)),
                  pl.BlockSpec((s, HEAD_DIM), lambda h, i: (0, ck0 + h)),
                  pl.BlockSpec((s, HEAD_DIM), lambda h, i: (0, cv0 + h)),
                  vec(C_QK_DIM), vec(C_QK_DIM), vec(C_QK_DIM), vec(C_QK_DIM), vec(C_V_DIM)],
        out_specs=pl.BlockSpec((tq, C_V_DIM), lambda h, i: (i, h)),
        scratch_shapes=[pltpu.VMEM((2 * tq, 1), jnp.float32),
                        pltpu.VMEM((2 * tq, 1), jnp.float32),
                        pltpu.VMEM((2 * tq, C_V_DIM), jnp.float32)],
        compiler_params=_params(("parallel", "parallel")),
        name="attn_c",
    )(slopes, qkv, qkv, qkv, lq1.reshape(1, -1), lk1.reshape(1, -1), lq2.reshape(1, -1),
      lk2.reshape(1, -1), norm_g.reshape(1, -1))


def kernel(x, mix_norm_g, w_in, b_sink, diff_lq1, diff_lk1, diff_lq2, diff_lk2, diff_norm_g,
           w_branch_a, w_branch_b, w_branch_c, w_out, mlp_norm_g, w_up, w_down, final_norm_g):
    bsz, seq, d = x.shape
    depth = w_in.shape[0]
    bf16 = jnp.bfloat16
    w_in_b, w_a_b, w_b_b, w_c_b = (w.astype(bf16) for w in (w_in, w_branch_a, w_branch_b, w_branch_c))
    w_out_b, w_up_b, w_down_b = (w.astype(bf16) for w in (w_out, w_up, w_down))
    xs = x.reshape(bsz * seq, d)
    outs = []
    for b in range(bsz):
        h = xs[b * seq:(b + 1) * seq]
        for l in range(depth):
            u = _rmsnorm(h, mix_norm_g[l], bf16)
            qkv = _matmul(u, w_in_b, l, 0, OFF_G, bf16, name="proj_qkv")
            gates = _matmul(u, w_in_b, l, OFF_G, N_BRANCH * d, bf16, epilogue="sigmoid", name="proj_gates")
            o_a = _attn_a(qkv)
            o_b = _attn_b(qkv, b_sink[l])
            lam_init = 0.8 - 0.6 * math.exp(-0.3 * l)
            o_c = _attn_c(qkv, diff_lq1[l], diff_lk1[l], diff_lq2[l], diff_lk2[l], diff_norm_g[l], lam_init)
            merged = _merge(o_a, o_b, o_c, gates, w_a_b, w_b_b, w_c_b, l)
            h = _matmul(merged, w_out_b, l, 0, d, jnp.float32, residual=h, name="out_proj")
            v = _rmsnorm(h, mlp_norm_g[l], bf16)
            h = _mlp(v, h, w_up_b, w_down_b, l)
        outs.append(_rmsnorm(h, final_norm_g, x.dtype))
    return jnp.concatenate(outs, axis=0).reshape(bsz, seq, d)
```

```python
import functools
import math

import jax
import jax.numpy as jnp
from jax import lax
from jax.experimental import pallas as pl
from jax.experimental.pallas import tpu as pltpu

HEAD_DIM = 128
A_PATTERNS = ((128, 1), (512, 4), (2048, 16))
A_HEADS = 8
N_PAT = len(A_PATTERNS)
B_HEADS = 12
B_KV_HEADS = 4
B_GROUP = B_HEADS // B_KV_HEADS
B_HALF = 128
C_HEADS = 12
C_QK_DIM = 64
C_V_DIM = 2 * C_QK_DIM
N_BRANCH = 3
EPS = 1e-6
NEG = -1e30

A_COLS = N_PAT * 3 * A_HEADS * HEAD_DIM
B_COLS = (B_HEADS + 2 * B_KV_HEADS) * HEAD_DIM
C_COLS = C_HEADS * (4 * C_QK_DIM + C_V_DIM)
OFF_B = A_COLS
OFF_C = OFF_B + B_COLS
OFF_G = OFF_C + C_COLS
A_OUT = A_HEADS * HEAD_DIM
B_OUT = B_HEADS * HEAD_DIM
C_OUT = C_HEADS * C_V_DIM

LANE = 128
VMEM_LIMIT = 56 * 1024 * 1024


def _tile(n, pref):
    t = pref
    while t > 1 and n % t:
        t //= 2
    return t


def _params(sem):
    return pltpu.CompilerParams(dimension_semantics=sem, vmem_limit_bytes=VMEM_LIMIT)


def _alibi_slopes(n):
    return [2.0 ** (-8.0 * i / n) for i in range(1, n + 1)]


def _rmsnorm_kernel(x_ref, g_ref, o_ref):
    x = x_ref[...]
    y = x * lax.rsqrt(jnp.mean(x * x, axis=-1, keepdims=True) + EPS)
    o_ref[...] = (y * g_ref[...]).astype(o_ref.dtype)


def _rmsnorm(x, g, out_dtype):
    s, d = x.shape
    tr = _tile(s, 256)
    return pl.pallas_call(
        _rmsnorm_kernel,
        out_shape=jax.ShapeDtypeStruct((s, d), out_dtype),
        grid=(s // tr,),
        in_specs=[pl.BlockSpec((tr, d), lambda i: (i, 0)),
                  pl.BlockSpec((1, d), lambda i: (0, 0))],
        out_specs=pl.BlockSpec((tr, d), lambda i: (i, 0)),
        compiler_params=_params(("parallel",)),
        name="rmsnorm",
    )(x, g.reshape(1, d))


def _matmul_kernel(x_ref, w_ref, o_ref, *, epilogue):
    acc = jnp.dot(x_ref[...], w_ref[...], preferred_element_type=jnp.float32)
    if epilogue == "sigmoid":
        acc = jax.nn.sigmoid(acc)
    o_ref[...] = acc.astype(o_ref.dtype)


def _matmul_scale_kernel(x_ref, w_ref, c_ref, o_ref):
    acc = jnp.dot(x_ref[...], w_ref[...], preferred_element_type=jnp.float32)
    o_ref[...] = (acc * c_ref[...]).astype(o_ref.dtype)


def _matmul_res_kernel(x_ref, w_ref, r_ref, o_ref):
    acc = jnp.dot(x_ref[...], w_ref[...], preferred_element_type=jnp.float32)
    o_ref[...] = r_ref[...] + acc


def _matmul(x, w, layer, col_off, ncols, out_dtype, epilogue="none", residual=None, col_scale=None,
            name="matmul"):
    m, k = x.shape
    tm = _tile(m, 1024)
    tn = _tile(math.gcd(ncols, col_off) if col_off else ncols, 1024)
    off = col_off // tn
    in_specs = [pl.BlockSpec((tm, k), lambda i, j: (i, 0)),
                pl.BlockSpec((None, k, tn), lambda i, j: (layer, 0, off + j))]
    args = [x, w]
    if col_scale is not None:
        kern = _matmul_scale_kernel
        in_specs.append(pl.BlockSpec((1, tn), lambda i, j: (0, j)))
        args.append(col_scale)
    elif residual is None:
        kern = functools.partial(_matmul_kernel, epilogue=epilogue)
    else:
        kern = _matmul_res_kernel
        in_specs.append(pl.BlockSpec((tm, tn), lambda i, j: (i, j)))
        args.append(residual)
    return pl.pallas_call(
        kern,
        out_shape=jax.ShapeDtypeStruct((m, ncols), out_dtype),
        grid=(m // tm, ncols // tn),
        in_specs=in_specs,
        out_specs=pl.BlockSpec((tm, tn), lambda i, j: (i, j)),
        compiler_params=_params(("parallel", "parallel")),
        name=name,
    )(*args)


def _merge_kernel(oa_ref, ob_ref, oc_ref, wa_ref, wb_ref, wc_ref, ga_ref, gb_ref, gc_ref, o_ref):
    f32 = jnp.float32
    ya = jnp.dot(oa_ref[...], wa_ref[...], preferred_element_type=f32)
    yb = jnp.dot(ob_ref[...], wb_ref[...], preferred_element_type=f32)
    yc = jnp.dot(oc_ref[...], wc_ref[...], preferred_element_type=f32)
    merged = (ga_ref[...].astype(f32) * ya + gb_ref[...].astype(f32) * yb
              + gc_ref[...].astype(f32) * yc)
    o_ref[...] = merged.astype(o_ref.dtype)


def _merge(o_a, o_b, o_c, gates, w_a, w_b, w_c, layer):
    s = o_a.shape[0]
    d = w_a.shape[-1]
    tm = _tile(s, 1024)
    tn = _tile(d, 512)
    nj = d // tn
    row = lambda width: pl.BlockSpec((tm, width), lambda i, j: (i, 0))
    wspec = lambda kk: pl.BlockSpec((None, kk, tn), lambda i, j: (layer, 0, j))
    gspec = lambda b: pl.BlockSpec((tm, tn), lambda i, j: (i, b * nj + j))
    return pl.pallas_call(
        _merge_kernel,
        out_shape=jax.ShapeDtypeStruct((s, d), jnp.bfloat16),
        grid=(s // tm, nj),
        in_specs=[row(A_OUT), row(B_OUT), row(C_OUT),
                  wspec(A_OUT), wspec(B_OUT), wspec(C_OUT),
                  gspec(0), gspec(1), gspec(2)],
        out_specs=pl.BlockSpec((tm, tn), lambda i, j: (i, j)),
        compiler_params=_params(("parallel", "parallel")),
        name="merge",
    )(o_a, o_b, o_c, w_a, w_b, w_c, gates, gates, gates)


def _mlp_kernel(v_ref, wu_ref, wd_ref, h_ref, o_ref):
    f = pl.program_id(1)

    @pl.when(f == 0)
    def _():
        o_ref[...] = h_ref[...]

    a = jnp.dot(v_ref[...], wu_ref[...], preferred_element_type=jnp.float32)
    a = jnp.square(jnp.maximum(a, 0.0)).astype(jnp.bfloat16)
    o_ref[...] += jnp.dot(a, wd_ref[...], preferred_element_type=jnp.float32)


def _mlp(v, h, w_up, w_down, layer):
    s, d = v.shape
    dff = w_up.shape[-1]
    tm = _tile(s, 512)
    tf = _tile(dff, 512)
    return pl.pallas_call(
        _mlp_kernel,
        out_shape=jax.ShapeDtypeStruct((s, d), jnp.float32),
        grid=(s // tm, dff // tf),
        in_specs=[pl.BlockSpec((tm, d), lambda i, f: (i, 0)),
                  pl.BlockSpec((None, d, tf), lambda i, f: (layer, 0, f)),
                  pl.BlockSpec((None, tf, d), lambda i, f: (layer, f, 0)),
                  pl.BlockSpec((tm, d), lambda i, f: (i, 0), pipeline_mode=pl.Buffered(1))],
        out_specs=pl.BlockSpec((tm, d), lambda i, f: (i, 0)),
        compiler_params=_params(("parallel", "arbitrary")),
        name="mlp",
    )(v, w_up, w_down, h)


A_TQ = 256
A_BLK = 64


def _attn_a_kernel(q_ref, kp_ref, km_ref, kn_ref, vp_ref, vm_ref, vn_ref, o_ref, lse_ref, *, dil, tq):
    i = pl.program_id(1)
    last = pl.num_programs(1) - 1
    nk = tq + 2 * A_BLK
    a = lax.broadcasted_iota(jnp.int32, (tq, nk), 0)
    b = lax.broadcasted_iota(jnp.int32, (tq, nk), 1)
    dist = jnp.abs(a - (b - A_BLK))
    valid = (dist <= A_BLK) & ((b >= A_BLK) | (i > 0)) & ((b < tq + A_BLK) | (i < last))
    distf = (dil * dist).astype(jnp.float32)
    lane = lax.broadcasted_iota(jnp.int32, (tq, LANE), 1)
    lse_tile = jnp.zeros((tq, LANE), jnp.float32)
    slopes = _alibi_slopes(A_HEADS)
    for h in range(A_HEADS):
        sl = slice(h * HEAD_DIM, (h + 1) * HEAD_DIM)
        q = q_ref[:, sl]
        k = jnp.concatenate([kp_ref[:, sl], km_ref[:, sl], kn_ref[:, sl]], axis=0)
        v = jnp.concatenate([vp_ref[:, sl], vm_ref[:, sl], vn_ref[:, sl]], axis=0)
        s = lax.dot_general(q, k, (((1,), (1,)), ((), ())), preferred_element_type=jnp.float32)
        s = s * (HEAD_DIM ** -0.5) - slopes[h] * distf
        s = jnp.where(valid, s, NEG)
        m = jnp.max(s, axis=-1, keepdims=True)
        e = jnp.exp(s - m)
        denom = jnp.sum(e, axis=-1, keepdims=True)
        p = (e / denom).astype(jnp.bfloat16)
        o_ref[:, sl] = jnp.dot(p, v, preferred_element_type=jnp.float32)
        lse_tile = jnp.where(lane == h, m + jnp.log(denom), lse_tile)
    lse_ref[...] = lse_tile


def _attn_a_pattern(qkv, g, dil):
    s, nc = qkv.shape
    ls = s // dil
    tq = _tile(ls, A_TQ)
    ncb = nc // A_OUT
    view = qkv.reshape(ls, dil * nc)
    r64 = tq // A_BLK
    nblk64 = ls // A_BLK
    cq, ck, cv = g * 3, g * 3 + 1, g * 3 + 2

    def main(c):
        return pl.BlockSpec((tq, A_OUT), lambda r, i: (i, r * ncb + c))

    def prev(c):
        return pl.BlockSpec((A_BLK, A_OUT), lambda r, i: (jnp.maximum(i * r64 - 1, 0), r * ncb + c))

    def nxt(c):
        return pl.BlockSpec((A_BLK, A_OUT),
                            lambda r, i: (jnp.minimum((i + 1) * r64, nblk64 - 1), r * ncb + c))

    o, lse = pl.pallas_call(
        functools.partial(_attn_a_kernel, dil=dil, tq=tq),
        out_shape=(jax.ShapeDtypeStruct((ls, dil * A_OUT), jnp.float32),
                   jax.ShapeDtypeStruct((ls, dil * LANE), jnp.float32)),
        grid=(dil, ls // tq),
        in_specs=[main(cq), prev(ck), main(ck), nxt(ck), prev(cv), main(cv), nxt(cv)],
        out_specs=(pl.BlockSpec((tq, A_OUT), lambda r, i: (i, r)),
                   pl.BlockSpec((tq, LANE), lambda r, i: (i, r))),
        compiler_params=_params(("parallel", "parallel")),
        name=f"attn_a{g}",
    )(view, view, view, view, view, view, view)
    return o.reshape(s, A_OUT), lse.reshape(s, LANE)


def _combine_a_kernel(o0_ref, o1_ref, o2_ref, l0_ref, l1_ref, l2_ref, o_ref):
    l0, l1, l2 = l0_ref[...], l1_ref[...], l2_ref[...]
    mx = jnp.maximum(jnp.maximum(l0, l1), l2)
    e0, e1, e2 = jnp.exp(l0 - mx), jnp.exp(l1 - mx), jnp.exp(l2 - mx)
    inv = 1.0 / (e0 + e1 + e2)
    w0, w1, w2 = e0 * inv, e1 * inv, e2 * inv
    for h in range(A_HEADS):
        sl = slice(h * HEAD_DIM, (h + 1) * HEAD_DIM)
        o = (w0[:, h:h + 1] * o0_ref[:, sl] + w1[:, h:h + 1] * o1_ref[:, sl]
             + w2[:, h:h + 1] * o2_ref[:, sl])
        o_ref[:, sl] = o.astype(o_ref.dtype)


def _attn_a(qkv):
    s = qkv.shape[0]
    outs = [_attn_a_pattern(qkv, g, dil) for g, (_, dil) in enumerate(A_PATTERNS)]
    tr = _tile(s, 512)
    ospec = pl.BlockSpec((tr, A_OUT), lambda i: (i, 0))
    lspec = pl.BlockSpec((tr, LANE), lambda i: (i, 0))
    return pl.pallas_call(
        _combine_a_kernel,
        out_shape=jax.ShapeDtypeStruct((s, A_OUT), jnp.bfloat16),
        grid=(s // tr,),
        in_specs=[ospec, ospec, ospec, lspec, lspec, lspec],
        out_specs=ospec,
        compiler_params=_params(("parallel",)),
        name="combine_a",
    )(outs[0][0], outs[1][0], outs[2][0], outs[0][1], outs[1][1], outs[2][1])


B_TQ = 256


def _attn_b_kernel(slope_ref, sink_ref, q_ref, kp_ref, km_ref, kn_ref, vp_ref, vm_ref, vn_ref, o_ref, *, tq):
    c = pl.program_id(0)
    i = pl.program_id(1)
    last = pl.num_programs(1) - 1
    nk = tq + 2 * B_HALF
    a = lax.broadcasted_iota(jnp.int32, (tq, nk), 0)
    b = lax.broadcasted_iota(jnp.int32, (tq, nk), 1)
    dist = jnp.abs(a - (b - B_HALF))
    valid = (dist <= B_HALF) & ((b >= B_HALF) | (i > 0)) & ((b < tq + B_HALF) | (i < last))
    distf = dist.astype(jnp.float32)
    k = jnp.concatenate([kp_ref[...], km_ref[...], kn_ref[...]], axis=0)
    v = jnp.concatenate([vp_ref[...], vm_ref[...], vn_ref[...]], axis=0)
    for g in range(B_GROUP):
        sl = slice(g * HEAD_DIM, (g + 1) * HEAD_DIM)
        slope = slope_ref[c * B_GROUP + g]
        sink = sink_ref[c * B_GROUP + g]
        s = lax.dot_general(q_ref[:, sl], k, (((1,), (1,)), ((), ())),
                            preferred_element_type=jnp.float32)
        s = s * (HEAD_DIM ** -0.5) - slope * distf
        s = jnp.where(valid, s, NEG)
        m = jnp.maximum(jnp.max(s, axis=-1, keepdims=True), sink)
        e = jnp.exp(s - m)
        denom = jnp.sum(e, axis=-1, keepdims=True) + jnp.exp(sink - m)
        p = (e / denom).astype(jnp.bfloat16)
        o_ref[:, sl] = jnp.dot(p, v, preferred_element_type=jnp.float32).astype(o_ref.dtype)


def _attn_b(qkv, sinks):
    s, nc = qkv.shape
    tq = _tile(s, B_TQ)
    r128 = tq // B_HALF
    nblk = s // B_HALF
    gw = B_GROUP * HEAD_DIM
    cq0 = OFF_B // gw
    ck0 = (OFF_B + B_HEADS * HEAD_DIM) // HEAD_DIM
    cv0 = ck0 + B_KV_HEADS
    smem = pl.BlockSpec(memory_space=pltpu.SMEM)

    def main(c0):
        return pl.BlockSpec((tq, HEAD_DIM), lambda c, i: (i, c0 + c))

    def prev(c0):
        return pl.BlockSpec((B_HALF, HEAD_DIM), lambda c, i: (jnp.maximum(i * r128 - 1, 0), c0 + c))

    def nxt(c0):
        return pl.BlockSpec((B_HALF, HEAD_DIM),
                            lambda c, i: (jnp.minimum((i + 1) * r128, nblk - 1), c0 + c))

    slopes = jnp.asarray(_alibi_slopes(B_HEADS), jnp.float32)
    return pl.pallas_call(
        functools.partial(_attn_b_kernel, tq=tq),
        out_shape=jax.ShapeDtypeStruct((s, B_OUT), jnp.bfloat16),
        grid=(B_KV_HEADS, s // tq),
        in_specs=[smem, smem,
                  pl.BlockSpec((tq, gw), lambda c, i: (i, cq0 + c)),
                  prev(ck0), main(ck0), nxt(ck0), prev(cv0), main(cv0), nxt(cv0)],
        out_specs=pl.BlockSpec((tq, gw), lambda c, i: (i, c)),
        compiler_params=_params(("parallel", "parallel")),
        name="attn_b",
    )(slopes, sinks.astype(jnp.float32), qkv, qkv, qkv, qkv, qkv, qkv, qkv)


C_TQ = 256
C_TK = 512
LOG2E = 1.4426950408889634
C_AUG = 16


def _attn_c_kernel(slope_ref, q_ref, k_ref, v_ref, lq1_ref, lk1_ref, lq2_ref, lk2_ref, g_ref, o_ref,
                   vt_ref, qzt_ref, m_ref, l_ref, acc_ref, s0_ref, s1_ref, mx0_ref, mx1_ref,
                   *, tq, tk, nkv, lam_init):
    f32, bf16 = jnp.float32, jnp.bfloat16
    h = pl.program_id(0)
    i = pl.program_id(1)
    slope = slope_ref[h]
    half = tk // 2

    @pl.when(i == 0)
    def _():
        def tbody(c, carry):
            st = pl.multiple_of(c * tk, tk)
            vt_ref[c] = v_ref[pl.ds(st, tk), :].astype(f32).T.astype(bf16)
            return carry
        lax.fori_loop(0, nkv, tbody, 0)

    qt = q_ref[...].astype(f32).T
    row = lax.broadcasted_iota(jnp.int32, qt.shape, 0)
    qzt_ref[...] = jnp.concatenate([jnp.where(row < C_QK_DIM, qt, 0.0), jnp.where(row >= C_QK_DIM, qt, 0.0)],
                                   axis=1).astype(bf16)
    m_ref[...] = jnp.full(m_ref.shape, NEG, f32)
    l_ref[...] = jnp.zeros(l_ref.shape, f32)
    acc_ref[...] = jnp.zeros(acc_ref.shape, f32)

    def chunk_at(t):
        return jnp.where(t == 0, jd, t - 1 + (t - 1 >= jd).astype(jnp.int32))

    def score_stage(t, s_ref, mx_ref):
        j = chunk_at(t)
        dc = (j * tk + half - i * tq).astype(f32)
        coef = jnp.where(j > jd, -slope, slope)
        base = jnp.where(rowq < 3, coef, -coef * (a_q - dc))
        p1 = base.astype(bf16)
        r1 = base - p1.astype(f32)
        p2 = r1.astype(bf16)
        p3 = (r1 - p2.astype(f32)).astype(bf16)
        piece = rowq % 3
        aug_q = jnp.where(rowq < 6, jnp.where(piece == 0, p1, jnp.where(piece == 1, p2, p3)),
                          jnp.zeros_like(p1))
        kc = k_ref[pl.ds(pl.multiple_of(j * tk, tk), tk), :]
        lhs = jnp.concatenate([kc, aug_k], axis=1)
        rhs = jnp.concatenate([qzt_ref[...], aug_q, zpad], axis=0)
        st = jnp.dot(lhs, rhs, preferred_element_type=f32)
        s_ref[...] = st
        mx_ref[...] = jnp.max(st, axis=0, keepdims=True)

    def softmax_stage(t, s_ref, mx_ref):
        m_old = m_ref[...]
        m_new = jnp.maximum(m_old, mx_ref[...])
        alpha = jnp.exp2(m_old - m_new)
        p = jnp.exp2(s_ref[...] - m_new)
        l_ref[...] = alpha * l_ref[...] + jnp.sum(p, axis=0, keepdims=True)
        acc_ref[...] = alpha * acc_ref[...] + jnp.dot(vt_ref[chunk_at(t)], p.astype(bf16),
                                                      preferred_element_type=f32)
        m_ref[...] = m_new

    colk = lax.broadcasted_iota(jnp.int32, (tk, LANE), 1)
    bk = (lax.broadcasted_iota(jnp.int32, (tk, LANE), 0) - half).astype(f32)
    aug_k = jnp.where(colk < 3, bk, jnp.where(colk < 6, 1.0, 0.0)).astype(bf16)
    rowq = lax.broadcasted_iota(jnp.int32, (C_AUG, 2 * tq), 0)
    a_q = (lax.broadcasted_iota(jnp.int32, (C_AUG, 2 * tq), 1) % tq).astype(f32)
    zpad = jnp.zeros((2 * LANE - 2 * C_QK_DIM - C_AUG, 2 * tq), bf16)

    jd = (i * tq) // tk
    kd = k_ref[pl.ds(pl.multiple_of(jd * tk, tk), tk), :]
    sd = jnp.dot(kd, qzt_ref[...], preferred_element_type=f32)
    rel = (lax.broadcasted_iota(jnp.int32, (tk, tq), 0) - lax.broadcasted_iota(jnp.int32, (tk, tq), 1)
           + (jd * tk - i * tq))
    bias = slope * jnp.abs(rel).astype(f32)
    sd = sd - jnp.concatenate([bias, bias], axis=1)
    s0_ref[...] = sd
    mx0_ref[...] = jnp.max(sd, axis=0, keepdims=True)

    def body(u, carry):
        score_stage(2 * u + 1, s1_ref, mx1_ref)
        softmax_stage(2 * u, s0_ref, mx0_ref)
        score_stage(2 * u + 2, s0_ref, mx0_ref)
        softmax_stage(2 * u + 1, s1_ref, mx1_ref)
        return carry

    npair = (nkv - 1) // 2
    lax.fori_loop(0, npair, body, 0)
    if nkv % 2 == 0:
        score_stage(nkv - 1, s1_ref, mx1_ref)
        softmax_stage(nkv - 2, s0_ref, mx0_ref)
        softmax_stage(nkv - 1, s1_ref, mx1_ref)
    else:
        softmax_stage(nkv - 1, s0_ref, mx0_ref)

    lam = (jnp.exp(jnp.sum(lq1_ref[...] * lk1_ref[...], axis=-1, keepdims=True))
           - jnp.exp(jnp.sum(lq2_ref[...] * lk2_ref[...], axis=-1, keepdims=True)) + lam_init)
    o = acc_ref[...] / l_ref[...]
    o = o[:, :tq] - lam * o[:, tq:]
    y = o * lax.rsqrt(jnp.mean(o * o, axis=0, keepdims=True) + EPS)
    o_ref[...] = (y * g_ref[...] * (1.0 - lam_init)).T.astype(o_ref.dtype)


def _attn_c(qkv, lq1, lk1, lq2, lk2, norm_g, lam_init):
    s, nc = qkv.shape
    tq = _tile(s, C_TQ)
    tk = _tile(s, C_TK)
    nkv = s // tk
    cq0 = OFF_C // HEAD_DIM
    ck0 = cq0 + C_HEADS
    cv0 = ck0 + C_HEADS
    smem = pl.BlockSpec(memory_space=pltpu.SMEM)
    vec = lambda n: pl.BlockSpec((1, n), lambda h, i: (0, 0))
    slopes = jnp.asarray([LOG2E * sl for sl in _alibi_slopes(C_HEADS)], jnp.float32)
    return pl.pallas_call(
        functools.partial(_attn_c_kernel, tq=tq, tk=tk, nkv=nkv, lam_init=lam_init),
        out_shape=jax.ShapeDtypeStruct((s, C_OUT), jnp.bfloat16),
        grid=(C_HEADS, s // tq),
        in_specs=[smem,
                  pl.BlockSpec((tq, HEAD_DIM), lambda h, i: (i, cq0 + h)),
                  pl.BlockSpec((s, HEAD_DIM), lambda h, i: (0, ck0 + h)),
                  pl.BlockSpec((s, HEAD_DIM), lambda h, i: (0, cv0 + h)),
                  vec(C_QK_DIM), vec(C_QK_DIM), vec(C_QK_DIM), vec(C_QK_DIM),
                  pl.BlockSpec((C_V_DIM, 1), lambda h, i: (0, 0))],
        out_specs=pl.BlockSpec((tq, C_V_DIM), lambda h, i: (i, h)),
        scratch_shapes=[pltpu.VMEM((nkv, C_V_DIM, tk), jnp.bfloat16),
                        pltpu.VMEM((2 * C_QK_DIM, 2 * tq), jnp.bfloat16),
                        pltpu.VMEM((1, 2 * tq), jnp.float32),
                        pltpu.VMEM((1, 2 * tq), jnp.float32),
                        pltpu.VMEM((C_V_DIM, 2 * tq), jnp.float32),
                        pltpu.VMEM((tk, 2 * tq), jnp.float32),
                        pltpu.VMEM((tk, 2 * tq), jnp.float32),
                        pltpu.VMEM((1, 2 * tq), jnp.float32),
                        pltpu.VMEM((1, 2 * tq), jnp.float32)],
        compiler_params=_params(("parallel", "arbitrary")),
        name="attn_c",
    )(slopes, qkv, qkv, qkv, lq1.reshape(1, -1), lk1.reshape(1, -1), lq2.reshape(1, -1),
      lk2.reshape(1, -1), norm_g.reshape(-1, 1))


def kernel(x, mix_norm_g, w_in, b_sink, diff_lq1, diff_lk1, diff_lq2, diff_lk2, diff_norm_g,
           w_branch_a, w_branch_b, w_branch_c, w_out, mlp_norm_g, w_up, w_down, final_norm_g):
    bsz, seq, d = x.shape
    depth = w_in.shape[0]
    bf16 = jnp.bfloat16
    w_in_b, w_a_b, w_b_b, w_c_b = (w.astype(bf16) for w in (w_in, w_branch_a, w_branch_b, w_branch_c))
    w_out_b, w_up_b, w_down_b = (w.astype(bf16) for w in (w_out, w_up, w_down))
    col = jnp.arange(OFF_G)
    cq = (col >= OFF_C) & (col < OFF_C + C_HEADS * 2 * C_QK_DIM)
    qkv_scale = jnp.where(cq, C_QK_DIM ** -0.5 * LOG2E, 1.0).astype(jnp.float32).reshape(1, OFF_G)
    xs = x.reshape(bsz * seq, d)
    outs = []
    for b in range(bsz):
        h = xs[b * seq:(b + 1) * seq]
        for l in range(depth):
            u = _rmsnorm(h, mix_norm_g[l], bf16)
            qkv = _matmul(u, w_in_b, l, 0, OFF_G, bf16, col_scale=qkv_scale, name="proj_qkv")
            gates = _matmul(u, w_in_b, l, OFF_G, N_BRANCH * d, bf16, epilogue="sigmoid", name="proj_gates")
            o_a = _attn_a(qkv)
            o_b = _attn_b(qkv, b_sink[l])
            lam_init = 0.8 - 0.6 * math.exp(-0.3 * l)
            o_c = _attn_c(qkv, diff_lq1[l], diff_lk1[l], diff_lq2[l], diff_lk2[l], diff_norm_g[l], lam_init)
            merged = _merge(o_a, o_b, o_c, gates, w_a_b, w_b_b, w_c_b, l)
            h = _matmul(merged, w_out_b, l, 0, d, jnp.float32, residual=h, name="out_proj")
            v = _rmsnorm(h, mlp_norm_g[l], bf16)
            h = _mlp(v, h, w_up_b, w_down_b, l)
        outs.append(_rmsnorm(h, final_norm_g, x.dtype))
    return jnp.concatenate(outs, axis=0).reshape(bsz, seq, d)
```

```python
import functools
import math

import jax
import jax.numpy as jnp
from jax import lax
from jax.experimental import pallas as pl
from jax.experimental.pallas import tpu as pltpu

HEAD_DIM = 128
A_PATTERNS = ((128, 1), (512, 4), (2048, 16))
A_HEADS = 8
N_PAT = len(A_PATTERNS)
B_HEADS = 12
B_KV_HEADS = 4
B_GROUP = B_HEADS // B_KV_HEADS
B_HALF = 128
C_HEADS = 12
C_QK_DIM = 64
C_V_DIM = 2 * C_QK_DIM
N_BRANCH = 3
EPS = 1e-6
NEG = -1e30

A_COLS = N_PAT * 3 * A_HEADS * HEAD_DIM
B_COLS = (B_HEADS + 2 * B_KV_HEADS) * HEAD_DIM
C_COLS = C_HEADS * (4 * C_QK_DIM + C_V_DIM)
OFF_B = A_COLS
OFF_C = OFF_B + B_COLS
OFF_G = OFF_C + C_COLS
A_OUT = A_HEADS * HEAD_DIM
B_OUT = B_HEADS * HEAD_DIM
C_OUT = C_HEADS * C_V_DIM

LANE = 128
VMEM_LIMIT = 56 * 1024 * 1024


def _tile(n, pref):
    t = pref
    while t > 1 and n % t:
        t //= 2
    return t


def _params(sem):
    return pltpu.CompilerParams(dimension_semantics=sem, vmem_limit_bytes=VMEM_LIMIT)


def _alibi_slopes(n):
    return [2.0 ** (-8.0 * i / n) for i in range(1, n + 1)]


def _rmsnorm_kernel(x_ref, g_ref, o_ref):
    x = x_ref[...]
    y = x * lax.rsqrt(jnp.mean(x * x, axis=-1, keepdims=True) + EPS)
    o_ref[...] = (y * g_ref[...]).astype(o_ref.dtype)


def _rmsnorm(x, g, out_dtype):
    s, d = x.shape
    tr = _tile(s, 256)
    return pl.pallas_call(
        _rmsnorm_kernel,
        out_shape=jax.ShapeDtypeStruct((s, d), out_dtype),
        grid=(s // tr,),
        in_specs=[pl.BlockSpec((tr, d), lambda i: (i, 0)),
                  pl.BlockSpec((1, d), lambda i: (0, 0))],
        out_specs=pl.BlockSpec((tr, d), lambda i: (i, 0)),
        compiler_params=_params(("parallel",)),
        name="rmsnorm",
    )(x, g.reshape(1, d))


def _rmsnorm_mix_kernel(x_ref, g_ref, o_ref, *grouped_refs, dils, tr):
    x = x_ref[...]
    y = (x * lax.rsqrt(jnp.mean(x * x, axis=-1, keepdims=True) + EPS) * g_ref[...]).astype(o_ref.dtype)
    o_ref[...] = y
    dst = lax.broadcasted_iota(jnp.int32, (tr, tr), 0)
    src = lax.broadcasted_iota(jnp.int32, (tr, tr), 1)
    for dil, s_ref in zip(dils, grouped_refs):
        n = tr // dil
        perm = (src == (dst % n) * dil + dst // n).astype(y.dtype)
        yp = jnp.dot(perm, y, preferred_element_type=jnp.float32).astype(y.dtype)
        for r in range(dil):
            s_ref[r] = yp[r * n:(r + 1) * n]


def _rmsnorm_mix(x, g, dils):
    s, d = x.shape
    tr = _tile(s, 256)
    out_shape = [jax.ShapeDtypeStruct((s, d), jnp.bfloat16)]
    out_specs = [pl.BlockSpec((tr, d), lambda i: (i, 0))]
    for dil in dils:
        out_shape.append(jax.ShapeDtypeStruct((dil, s // dil, d), jnp.bfloat16))
        out_specs.append(pl.BlockSpec((dil, tr // dil, d), lambda i: (0, i, 0)))
    return pl.pallas_call(
        functools.partial(_rmsnorm_mix_kernel, dils=dils, tr=tr),
        out_shape=out_shape,
        grid=(s // tr,),
        in_specs=[pl.BlockSpec((tr, d), lambda i: (i, 0)),
                  pl.BlockSpec((1, d), lambda i: (0, 0))],
        out_specs=out_specs,
        compiler_params=_params(("parallel",)),
        name="rmsnorm_mix",
    )(x, g.reshape(1, d))


def _matmul_kernel(x_ref, w_ref, o_ref, *, epilogue):
    acc = jnp.dot(x_ref[...], w_ref[...], preferred_element_type=jnp.float32)
    if epilogue == "sigmoid":
        acc = jax.nn.sigmoid(acc)
    o_ref[...] = acc.astype(o_ref.dtype)


def _matmul_scale_kernel(x_ref, w_ref, c_ref, o_ref):
    acc = jnp.dot(x_ref[...], w_ref[...], preferred_element_type=jnp.float32)
    o_ref[...] = (acc * c_ref[...]).astype(o_ref.dtype)


def _matmul_res_kernel(x_ref, w_ref, r_ref, o_ref):
    acc = jnp.dot(x_ref[...], w_ref[...], preferred_element_type=jnp.float32)
    o_ref[...] = r_ref[...] + acc


def _matmul(x, w, layer, col_off, ncols, out_dtype, epilogue="none", residual=None, col_scale=None,
            name="matmul"):
    m, k = x.shape
    tm = _tile(m, 1024)
    tn = _tile(math.gcd(ncols, col_off) if col_off else ncols, 1024)
    off = col_off // tn
    in_specs = [pl.BlockSpec((tm, k), lambda i, j: (i, 0)),
                pl.BlockSpec((None, k, tn), lambda i, j: (layer, 0, off + j))]
    args = [x, w]
    if col_scale is not None:
        kern = _matmul_scale_kernel
        in_specs.append(pl.BlockSpec((1, tn), lambda i, j: (0, j)))
        args.append(col_scale)
    elif residual is None:
        kern = functools.partial(_matmul_kernel, epilogue=epilogue)
    else:
        kern = _matmul_res_kernel
        in_specs.append(pl.BlockSpec((tm, tn), lambda i, j: (i, j)))
        args.append(residual)
    return pl.pallas_call(
        kern,
        out_shape=jax.ShapeDtypeStruct((m, ncols), out_dtype),
        grid=(m // tm, ncols // tn),
        in_specs=in_specs,
        out_specs=pl.BlockSpec((tm, tn), lambda i, j: (i, j)),
        compiler_params=_params(("parallel", "parallel")),
        name=name,
    )(*args)


def _merge_kernel(oa_ref, ob_ref, oc_ref, wa_ref, wb_ref, wc_ref, ga_ref, gb_ref, gc_ref, o_ref):
    f32 = jnp.float32
    ya = jnp.dot(oa_ref[...], wa_ref[...], preferred_element_type=f32)
    yb = jnp.dot(ob_ref[...], wb_ref[...], preferred_element_type=f32)
    yc = jnp.dot(oc_ref[...], wc_ref[...], preferred_element_type=f32)
    merged = (ga_ref[...].astype(f32) * ya + gb_ref[...].astype(f32) * yb
              + gc_ref[...].astype(f32) * yc)
    o_ref[...] = merged.astype(o_ref.dtype)


def _merge(o_a, o_b, o_c, gates, w_a, w_b, w_c, layer):
    s = o_a.shape[0]
    d = w_a.shape[-1]
    tm = _tile(s, 1024)
    tn = _tile(d, 512)
    nj = d // tn
    row = lambda width: pl.BlockSpec((tm, width), lambda i, j: (i, 0))
    wspec = lambda kk: pl.BlockSpec((None, kk, tn), lambda i, j: (layer, 0, j))
    gspec = lambda b: pl.BlockSpec((tm, tn), lambda i, j: (i, b * nj + j))
    return pl.pallas_call(
        _merge_kernel,
        out_shape=jax.ShapeDtypeStruct((s, d), jnp.bfloat16),
        grid=(s // tm, nj),
        in_specs=[row(A_OUT), row(B_OUT), row(C_OUT),
                  wspec(A_OUT), wspec(B_OUT), wspec(C_OUT),
                  gspec(0), gspec(1), gspec(2)],
        out_specs=pl.BlockSpec((tm, tn), lambda i, j: (i, j)),
        compiler_params=_params(("parallel", "parallel")),
        name="merge",
    )(o_a, o_b, o_c, w_a, w_b, w_c, gates, gates, gates)


def _mlp_kernel(v_ref, wu_ref, wd_ref, h_ref, o_ref):
    f = pl.program_id(1)

    @pl.when(f == 0)
    def _():
        o_ref[...] = h_ref[...]

    a = jnp.dot(v_ref[...], wu_ref[...], preferred_element_type=jnp.float32)
    a = jnp.square(jnp.maximum(a, 0.0)).astype(jnp.bfloat16)
    o_ref[...] += jnp.dot(a, wd_ref[...], preferred_element_type=jnp.float32)


def _mlp(v, h, w_up, w_down, layer):
    s, d = v.shape
    dff = w_up.shape[-1]
    tm = _tile(s, 512)
    tf = _tile(dff, 512)
    return pl.pallas_call(
        _mlp_kernel,
        out_shape=jax.ShapeDtypeStruct((s, d), jnp.float32),
        grid=(s // tm, dff // tf),
        in_specs=[pl.BlockSpec((tm, d), lambda i, f: (i, 0)),
                  pl.BlockSpec((None, d, tf), lambda i, f: (layer, 0, f)),
                  pl.BlockSpec((None, tf, d), lambda i, f: (layer, f, 0)),
                  pl.BlockSpec((tm, d), lambda i, f: (i, 0), pipeline_mode=pl.Buffered(1))],
        out_specs=pl.BlockSpec((tm, d), lambda i, f: (i, 0)),
        compiler_params=_params(("parallel", "arbitrary")),
        name="mlp",
    )(v, w_up, w_down, h)


A_TQ = 256
A_BLK = 64


def _attn_a_kernel(q_ref, kp_ref, km_ref, kn_ref, vp_ref, vm_ref, vn_ref, o_ref, lse_ref, *, dil, tq):
    i = pl.program_id(1)
    last = pl.num_programs(1) - 1
    nk = tq + 2 * A_BLK
    a = lax.broadcasted_iota(jnp.int32, (tq, nk), 0)
    b = lax.broadcasted_iota(jnp.int32, (tq, nk), 1)
    dist = jnp.abs(a - (b - A_BLK))
    valid = (dist <= A_BLK) & ((b >= A_BLK) | (i > 0)) & ((b < tq + A_BLK) | (i < last))
    distf = (dil * dist).astype(jnp.float32)
    lane = lax.broadcasted_iota(jnp.int32, (tq, LANE), 1)
    lse_tile = jnp.zeros((tq, LANE), jnp.float32)
    slopes = _alibi_slopes(A_HEADS)
    for h in range(A_HEADS):
        sl = slice(h * HEAD_DIM, (h + 1) * HEAD_DIM)
        q = q_ref[:, sl]
        k = jnp.concatenate([kp_ref[:, sl], km_ref[:, sl], kn_ref[:, sl]], axis=0)
        v = jnp.concatenate([vp_ref[:, sl], vm_ref[:, sl], vn_ref[:, sl]], axis=0)
        s = lax.dot_general(q, k, (((1,), (1,)), ((), ())), preferred_element_type=jnp.float32)
        s = s * (HEAD_DIM ** -0.5) - slopes[h] * distf
        s = jnp.where(valid, s, NEG)
        m = jnp.max(s, axis=-1, keepdims=True)
        e = jnp.exp(s - m)
        denom = jnp.sum(e, axis=-1, keepdims=True)
        p = (e / denom).astype(jnp.bfloat16)
        o_ref[:, sl] = jnp.dot(p, v, preferred_element_type=jnp.float32)
        lse_tile = jnp.where(lane == h, m + jnp.log(denom), lse_tile)
    lse_ref[...] = lse_tile


def _attn_a_pattern(pa, g):
    dil, ls, _ = pa.shape
    tq = _tile(ls, A_TQ)
    r64 = tq // A_BLK
    nblk64 = ls // A_BLK

    def main(c):
        return pl.BlockSpec((None, tq, A_OUT), lambda r, i: (r, i, c))

    def prev(c):
        return pl.BlockSpec((None, A_BLK, A_OUT), lambda r, i: (r, jnp.maximum(i * r64 - 1, 0), c))

    def nxt(c):
        return pl.BlockSpec((None, A_BLK, A_OUT),
                            lambda r, i: (r, jnp.minimum((i + 1) * r64, nblk64 - 1), c))

    return pl.pallas_call(
        functools.partial(_attn_a_kernel, dil=dil, tq=tq),
        out_shape=(jax.ShapeDtypeStruct((dil, ls, A_OUT), jnp.float32),
                   jax.ShapeDtypeStruct((dil, ls, LANE), jnp.float32)),
        grid=(dil, ls // tq),
        in_specs=[main(0), prev(1), main(1), nxt(1), prev(2), main(2), nxt(2)],
        out_specs=(pl.BlockSpec((None, tq, A_OUT), lambda r, i: (r, i, 0)),
                   pl.BlockSpec((None, tq, LANE), lambda r, i: (r, i, 0))),
        compiler_params=_params(("parallel", "parallel")),
        name=f"attn_a{g}",
    )(pa, pa, pa, pa, pa, pa, pa)


def _combine_a_kernel(o0_ref, o1_ref, o2_ref, l0_ref, l1_ref, l2_ref, o_ref, *scratch, dils, tr):
    o_in, l_in = [o0_ref, o1_ref, o2_ref], [l0_ref, l1_ref, l2_ref]
    nat = [scratch[g] if dil > 1 else None for g, dil in enumerate(dils)]

    def ungroup(g, src):
        dil = dils[g]
        if dil == 1:
            return src(0)
        for r in range(dil):
            nat[g][pl.ds(r, tr // dil, stride=dil), :] = src(r)
        return nat[g][...]

    l0, l1, l2 = (ungroup(g, lambda r, g=g: l_in[g][r]) for g in range(len(dils)))
    mx = jnp.maximum(jnp.maximum(l0, l1), l2)
    e0, e1, e2 = jnp.exp(l0 - mx), jnp.exp(l1 - mx), jnp.exp(l2 - mx)
    inv = 1.0 / (e0 + e1 + e2)
    w = [e0 * inv, e1 * inv, e2 * inv]
    for h in range(A_HEADS):
        sl = slice(h * HEAD_DIM, (h + 1) * HEAD_DIM)
        o = sum(w[g][:, h:h + 1] * ungroup(g, lambda r, g=g: o_in[g][r, :, sl]) for g in range(len(dils)))
        o_ref[:, sl] = o.astype(o_ref.dtype)


def _attn_a(pas):
    dils = tuple(p.shape[0] for p in pas)
    s = pas[0].shape[0] * pas[0].shape[1]
    outs = [_attn_a_pattern(p, g) for g, p in enumerate(pas)]
    tr = _tile(s, 512)
    ospec = lambda dil: pl.BlockSpec((dil, tr // dil, A_OUT), lambda i: (0, i, 0))
    lspec = lambda dil: pl.BlockSpec((dil, tr // dil, LANE), lambda i: (0, i, 0))
    scratch = [pltpu.VMEM((tr, LANE), jnp.float32) for _ in dils]
    return pl.pallas_call(
        functools.partial(_combine_a_kernel, dils=dils, tr=tr),
        out_shape=jax.ShapeDtypeStruct((s, A_OUT), jnp.bfloat16),
        grid=(s // tr,),
        in_specs=[ospec(d) for d in dils] + [lspec(d) for d in dils],
        out_specs=pl.BlockSpec((tr, A_OUT), lambda i: (i, 0)),
        scratch_shapes=scratch,
        compiler_params=_params(("parallel",)),
        name="combine_a",
    )(*[o for o, _ in outs], *[l for _, l in outs])


B_TQ = 256


def _attn_b_kernel(slope_ref, sink_ref, q_ref, kp_ref, km_ref, kn_ref, vp_ref, vm_ref, vn_ref, o_ref, *, tq):
    c = pl.program_id(0)
    i = pl.program_id(1)
    last = pl.num_programs(1) - 1
    nk = tq + 2 * B_HALF
    a = lax.broadcasted_iota(jnp.int32, (tq, nk), 0)
    b = lax.broadcasted_iota(jnp.int32, (tq, nk), 1)
    dist = jnp.abs(a - (b - B_HALF))
    valid = (dist <= B_HALF) & ((b >= B_HALF) | (i > 0)) & ((b < tq + B_HALF) | (i < last))
    distf = dist.astype(jnp.float32)
    k = jnp.concatenate([kp_ref[...], km_ref[...], kn_ref[...]], axis=0)
    v = jnp.concatenate([vp_ref[...], vm_ref[...], vn_ref[...]], axis=0)
    for g in range(B_GROUP):
        sl = slice(g * HEAD_DIM, (g + 1) * HEAD_DIM)
        slope = slope_ref[c * B_GROUP + g]
        sink = sink_ref[c * B_GROUP + g]
        s = lax.dot_general(q_ref[:, sl], k, (((1,), (1,)), ((), ())),
                            preferred_element_type=jnp.float32)
        s = s * (HEAD_DIM ** -0.5) - slope * distf
        s = jnp.where(valid, s, NEG)
        m = jnp.maximum(jnp.max(s, axis=-1, keepdims=True), sink)
        e = jnp.exp(s - m)
        denom = jnp.sum(e, axis=-1, keepdims=True) + jnp.exp(sink - m)
        p = (e / denom).astype(jnp.bfloat16)
        o_ref[:, sl] = jnp.dot(p, v, preferred_element_type=jnp.float32).astype(o_ref.dtype)


def _attn_b(qkv, sinks):
    s, nc = qkv.shape
    tq = _tile(s, B_TQ)
    r128 = tq // B_HALF
    nblk = s // B_HALF
    gw = B_GROUP * HEAD_DIM
    cq0 = 0
    ck0 = B_HEADS
    cv0 = ck0 + B_KV_HEADS
    smem = pl.BlockSpec(memory_space=pltpu.SMEM)

    def main(c0):
        return pl.BlockSpec((tq, HEAD_DIM), lambda c, i: (i, c0 + c))

    def prev(c0):
        return pl.BlockSpec((B_HALF, HEAD_DIM), lambda c, i: (jnp.maximum(i * r128 - 1, 0), c0 + c))

    def nxt(c0):
        return pl.BlockSpec((B_HALF, HEAD_DIM),
                            lambda c, i: (jnp.minimum((i + 1) * r128, nblk - 1), c0 + c))

    slopes = jnp.asarray(_alibi_slopes(B_HEADS), jnp.float32)
    return pl.pallas_call(
        functools.partial(_attn_b_kernel, tq=tq),
        out_shape=jax.ShapeDtypeStruct((s, B_OUT), jnp.bfloat16),
        grid=(B_KV_HEADS, s // tq),
        in_specs=[smem, smem,
                  pl.BlockSpec((tq, gw), lambda c, i: (i, cq0 + c)),
                  prev(ck0), main(ck0), nxt(ck0), prev(cv0), main(cv0), nxt(cv0)],
        out_specs=pl.BlockSpec((tq, gw), lambda c, i: (i, c)),
        compiler_params=_params(("parallel", "parallel")),
        name="attn_b",
    )(slopes, sinks.astype(jnp.float32), qkv, qkv, qkv, qkv, qkv, qkv, qkv)


C_TQ = 256
C_TK = 512
LOG2E = 1.4426950408889634
C_AUG = 16
C_UNDERFLOW = 170.0


def _attn_c_kernel(slope_ref, inv_ref, q_ref, k_ref, v_ref, lq1_ref, lk1_ref, lq2_ref, lk2_ref, g_ref, o_ref,
                   vt_ref, qzt_ref, m_ref, l_ref, acc_ref, s0_ref, s1_ref, mx0_ref, mx1_ref, kn2_ref,
                   *, tq, tk, nkv, lam_init):
    f32, bf16 = jnp.float32, jnp.bfloat16
    h = pl.program_id(0)
    i = pl.program_id(1)
    slope = slope_ref[h]
    half = tk // 2
    lane_k = lax.broadcasted_iota(jnp.int32, (tk, 2 * C_QK_DIM), 1)

    @pl.when(i == 0)
    def _():
        def tbody(c, kn2):
            st = pl.multiple_of(c * tk, tk)
            vt_ref[c] = v_ref[pl.ds(st, tk), :].astype(f32).T.astype(bf16)
            kk = k_ref[pl.ds(st, tk), :].astype(f32)
            sq = kk * kk
            n_all = jnp.sum(sq, axis=1, keepdims=True)
            n_0 = jnp.sum(jnp.where(lane_k < C_QK_DIM, sq, 0.0), axis=1, keepdims=True)
            return jnp.maximum(kn2, jnp.max(jnp.maximum(n_0, n_all - n_0)))
        kn2_ref[0] = lax.fori_loop(0, nkv, tbody, jnp.float32(0.0))

    qt = q_ref[...].astype(f32).T
    row = lax.broadcasted_iota(jnp.int32, qt.shape, 0)
    qzt_ref[...] = jnp.concatenate([jnp.where(row < C_QK_DIM, qt, 0.0), jnp.where(row >= C_QK_DIM, qt, 0.0)],
                                   axis=1).astype(bf16)
    m_ref[...] = jnp.full(m_ref.shape, NEG, f32)
    l_ref[...] = jnp.zeros(l_ref.shape, f32)
    acc_ref[...] = jnp.zeros(acc_ref.shape, f32)

    def chunk_at(t):
        j = jlo + t - 1
        return jnp.where(t == 0, jd, j + (j >= jd).astype(jnp.int32))

    def score_stage(t, s_ref, mx_ref):
        j = chunk_at(t)
        dc = (j * tk + half - i * tq).astype(f32)
        coef = jnp.where(j > jd, -slope, slope)
        base = jnp.where(rowq < 3, coef, -coef * (a_q - dc))
        p1 = base.astype(bf16)
        r1 = base - p1.astype(f32)
        p2 = r1.astype(bf16)
        p3 = (r1 - p2.astype(f32)).astype(bf16)
        piece = rowq % 3
        aug_q = jnp.where(rowq < 6, jnp.where(piece == 0, p1, jnp.where(piece == 1, p2, p3)),
                          jnp.zeros_like(p1))
        kc = k_ref[pl.ds(pl.multiple_of(j * tk, tk), tk), :]
        lhs = jnp.concatenate([kc, aug_k], axis=1)
        rhs = jnp.concatenate([qzt_ref[...], aug_q, zpad], axis=0)
        st = jnp.dot(lhs, rhs, preferred_element_type=f32)
        s_ref[...] = st
        mx_ref[...] = jnp.max(st, axis=0, keepdims=True)

    def softmax_stage(t, s_ref, mx_ref):
        m_old = m_ref[...]
        m_new = jnp.maximum(m_old, mx_ref[...])
        alpha = jnp.exp2(m_old - m_new)
        p = jnp.exp2(s_ref[...] - m_new)
        l_ref[...] = alpha * l_ref[...] + jnp.sum(p, axis=0, keepdims=True)
        acc_ref[...] = alpha * acc_ref[...] + jnp.dot(vt_ref[chunk_at(t)], p.astype(bf16),
                                                      preferred_element_type=f32)
        m_ref[...] = m_new

    colk = lax.broadcasted_iota(jnp.int32, (tk, LANE), 1)
    bk = (lax.broadcasted_iota(jnp.int32, (tk, LANE), 0) - half).astype(f32)
    aug_k = jnp.where(colk < 3, bk, jnp.where(colk < 6, 1.0, 0.0)).astype(bf16)
    rowq = lax.broadcasted_iota(jnp.int32, (C_AUG, 2 * tq), 0)
    a_q = (lax.broadcasted_iota(jnp.int32, (C_AUG, 2 * tq), 1) % tq).astype(f32)
    zpad = jnp.zeros((2 * LANE - 2 * C_QK_DIM - C_AUG, 2 * tq), bf16)

    jd = (i * tq) // tk
    kd = k_ref[pl.ds(pl.multiple_of(jd * tk, tk), tk), :]
    sd = jnp.dot(kd, qzt_ref[...], preferred_element_type=f32)
    rel = (lax.broadcasted_iota(jnp.int32, (tk, tq), 0) - lax.broadcasted_iota(jnp.int32, (tk, tq), 1)
           + (jd * tk - i * tq))
    bias = slope * jnp.abs(rel).astype(f32)
    sd = sd - jnp.concatenate([bias, bias], axis=1)
    s0_ref[...] = sd
    mxd = jnp.max(sd, axis=0, keepdims=True)
    mx0_ref[...] = mxd

    sqq = qt * qt
    n_all = jnp.sum(sqq, axis=0, keepdims=True)
    n_0 = jnp.sum(jnp.where(row < C_QK_DIM, sqq, 0.0), axis=0, keepdims=True)
    s_max = jnp.max(jnp.sqrt(jnp.maximum(n_0, n_all - n_0) * kn2_ref[0]))
    reach = (s_max + C_UNDERFLOW - jnp.min(mxd)) * inv_ref[h]
    wnd = jnp.minimum(reach, float(nkv)).astype(jnp.int32) + 1
    jlo = jnp.maximum(jd - wnd, 0)
    jhi = jnp.minimum(jd + wnd, nkv - 1)
    odd = (jhi - jlo) % 2 == 0
    grow_hi = odd & (jhi < nkv - 1)
    jhi = jhi + grow_hi.astype(jnp.int32)
    jlo = jlo - (odd & ~grow_hi).astype(jnp.int32)
    nvis = jhi - jlo + 1

    def pair(u):
        score_stage(2 * u + 1, s1_ref, mx1_ref)
        softmax_stage(2 * u, s0_ref, mx0_ref)
        score_stage(2 * u + 2, s0_ref, mx0_ref)
        softmax_stage(2 * u + 1, s1_ref, mx1_ref)

    def body(w, carry):
        pair(2 * w)
        pair(2 * w + 1)
        return carry

    npair = nvis // 2 - 1
    lax.fori_loop(0, npair // 2, body, 0)

    @pl.when(npair % 2 == 1)
    def _():
        pair(npair - 1)

    score_stage(nvis - 1, s1_ref, mx1_ref)
    softmax_stage(nvis - 2, s0_ref, mx0_ref)
    softmax_stage(nvis - 1, s1_ref, mx1_ref)

    lam = (jnp.exp(jnp.sum(lq1_ref[...] * lk1_ref[...], axis=-1, keepdims=True))
           - jnp.exp(jnp.sum(lq2_ref[...] * lk2_ref[...], axis=-1, keepdims=True)) + lam_init)
    o = acc_ref[...] / l_ref[...]
    o = o[:, :tq] - lam * o[:, tq:]
    y = o * lax.rsqrt(jnp.mean(o * o, axis=0, keepdims=True) + EPS)
    o_ref[...] = (y * g_ref[...] * (1.0 - lam_init)).T.astype(o_ref.dtype)


def _attn_c(qkv, lq1, lk1, lq2, lk2, norm_g, lam_init):
    s, nc = qkv.shape
    tq = _tile(s, C_TQ)
    tk = _tile(s, C_TK)
    nkv = s // tk
    assert nkv % 2 == 0 and tk % tq == 0, (s, tq, tk)
    cq0 = B_COLS // HEAD_DIM
    ck0 = cq0 + C_HEADS
    cv0 = ck0 + C_HEADS
    smem = pl.BlockSpec(memory_space=pltpu.SMEM)
    vec = lambda n: pl.BlockSpec((1, n), lambda h, i: (0, 0))
    slopes_l2 = [LOG2E * sl for sl in _alibi_slopes(C_HEADS)]
    slopes = jnp.asarray(slopes_l2, jnp.float32)
    inv_reach = jnp.asarray([1.0 / (sl * tk) for sl in slopes_l2], jnp.float32)
    return pl.pallas_call(
        functools.partial(_attn_c_kernel, tq=tq, tk=tk, nkv=nkv, lam_init=lam_init),
        out_shape=jax.ShapeDtypeStruct((s, C_OUT), jnp.bfloat16),
        grid=(C_HEADS, s // tq),
        in_specs=[smem, smem,
                  pl.BlockSpec((tq, HEAD_DIM), lambda h, i: (i, cq0 + h)),
                  pl.BlockSpec((s, HEAD_DIM), lambda h, i: (0, ck0 + h)),
                  pl.BlockSpec((s, HEAD_DIM), lambda h, i: (0, cv0 + h)),
                  vec(C_QK_DIM), vec(C_QK_DIM), vec(C_QK_DIM), vec(C_QK_DIM),
                  pl.BlockSpec((C_V_DIM, 1), lambda h, i: (0, 0))],
        out_specs=pl.BlockSpec((tq, C_V_DIM), lambda h, i: (i, h)),
        scratch_shapes=[pltpu.VMEM((nkv, C_V_DIM, tk), jnp.bfloat16),
                        pltpu.VMEM((2 * C_QK_DIM, 2 * tq), jnp.bfloat16),
                        pltpu.VMEM((1, 2 * tq), jnp.float32),
                        pltpu.VMEM((1, 2 * tq), jnp.float32),
                        pltpu.VMEM((C_V_DIM, 2 * tq), jnp.float32),
                        pltpu.VMEM((tk, 2 * tq), jnp.float32),
                        pltpu.VMEM((tk, 2 * tq), jnp.float32),
                        pltpu.VMEM((1, 2 * tq), jnp.float32),
                        pltpu.VMEM((1, 2 * tq), jnp.float32),
                        pltpu.SMEM((1,), jnp.float32)],
        compiler_params=_params(("parallel", "arbitrary")),
        name="attn_c",
    )(slopes, inv_reach, qkv, qkv, qkv, lq1.reshape(1, -1), lk1.reshape(1, -1), lq2.reshape(1, -1),
      lk2.reshape(1, -1), norm_g.reshape(-1, 1))


def kernel(x, mix_norm_g, w_in, b_sink, diff_lq1, diff_lk1, diff_lq2, diff_lk2, diff_norm_g,
           w_branch_a, w_branch_b, w_branch_c, w_out, mlp_norm_g, w_up, w_down, final_norm_g):
    bsz, seq, d = x.shape
    depth = w_in.shape[0]
    bf16 = jnp.bfloat16
    w_in_b, w_a_b, w_b_b, w_c_b = (w.astype(bf16) for w in (w_in, w_branch_a, w_branch_b, w_branch_c))
    w_out_b, w_up_b, w_down_b = (w.astype(bf16) for w in (w_out, w_up, w_down))
    col = jnp.arange(B_COLS + C_COLS)
    cq = (col >= B_COLS) & (col < B_COLS + C_HEADS * 2 * C_QK_DIM)
    bc_scale = jnp.where(cq, C_QK_DIM ** -0.5 * LOG2E, 1.0).astype(jnp.float32).reshape(1, -1)
    dils = tuple(dil for _, dil in A_PATTERNS)
    pat_cols = 3 * A_OUT
    xs = x.reshape(bsz * seq, d)
    outs = []
    for b in range(bsz):
        h = xs[b * seq:(b + 1) * seq]
        for l in range(depth):
            u, *u_dil = _rmsnorm_mix(h, mix_norm_g[l], tuple(dl for dl in dils if dl > 1))
            u_by_dil = {1: u, **{dl: ud.reshape(seq, d) for dl, ud in zip([dl for dl in dils if dl > 1], u_dil)}}
            pas = [_matmul(u_by_dil[dl], w_in_b, l, g * pat_cols, pat_cols, bf16,
                           name=f"proj_a{g}").reshape(dl, seq // dl, pat_cols)
                   for g, dl in enumerate(dils)]
            pbc = _matmul(u, w_in_b, l, OFF_B, B_COLS + C_COLS, bf16, col_scale=bc_scale, name="proj_bc")
            gates = _matmul(u, w_in_b, l, OFF_G, N_BRANCH * d, bf16, epilogue="sigmoid", name="proj_gates")
            o_a = _attn_a(pas)
            o_b = _attn_b(pbc, b_sink[l])
            lam_init = 0.8 - 0.6 * math.exp(-0.3 * l)
            o_c = _attn_c(pbc, diff_lq1[l], diff_lk1[l], diff_lq2[l], diff_lk2[l], diff_norm_g[l], lam_init)
            merged = _merge(o_a, o_b, o_c, gates, w_a_b, w_b_b, w_c_b, l)
            h = _matmul(merged, w_out_b, l, 0, d, jnp.float32, residual=h, name="out_proj")
            v = _rmsnorm(h, mlp_norm_g[l], bf16)
            h = _mlp(v, h, w_up_b, w_down_b, l)
        outs.append(_rmsnorm(h, final_norm_g, x.dtype))
    return jnp.concatenate(outs, axis=0).reshape(bsz, seq, d)
```

```python
import functools
import math

import jax
import jax.numpy as jnp
from jax import lax
from jax.experimental import pallas as pl
from jax.experimental.pallas import tpu as pltpu

HEAD_DIM = 128
A_PATTERNS = ((128, 1), (512, 4), (2048, 16))
A_HEADS = 8
N_PAT = len(A_PATTERNS)
B_HEADS = 12
B_KV_HEADS = 4
B_GROUP = B_HEADS // B_KV_HEADS
B_HALF = 128
C_HEADS = 12
C_QK_DIM = 64
C_V_DIM = 2 * C_QK_DIM
N_BRANCH = 3
EPS = 1e-6
NEG = -1e30

A_COLS = N_PAT * 3 * A_HEADS * HEAD_DIM
B_COLS = (B_HEADS + 2 * B_KV_HEADS) * HEAD_DIM
C_COLS = C_HEADS * (4 * C_QK_DIM + C_V_DIM)
OFF_B = A_COLS
OFF_C = OFF_B + B_COLS
OFF_G = OFF_C + C_COLS
A_OUT = A_HEADS * HEAD_DIM
B_OUT = B_HEADS * HEAD_DIM
C_OUT = C_HEADS * C_V_DIM

LANE = 128
VMEM_LIMIT = 56 * 1024 * 1024


def _tile(n, pref):
    t = pref
    while t > 1 and n % t:
        t //= 2
    return t


def _params(sem):
    return pltpu.CompilerParams(dimension_semantics=sem, vmem_limit_bytes=VMEM_LIMIT)


def _alibi_slopes(n):
    return [2.0 ** (-8.0 * i / n) for i in range(1, n + 1)]


def _rmsnorm_kernel(x_ref, g_ref, o_ref):
    x = x_ref[...]
    y = x * lax.rsqrt(jnp.mean(x * x, axis=-1, keepdims=True) + EPS)
    o_ref[...] = (y * g_ref[...]).astype(o_ref.dtype)


def _rmsnorm(x, g, out_dtype):
    s, d = x.shape
    tr = _tile(s, 256)
    return pl.pallas_call(
        _rmsnorm_kernel,
        out_shape=jax.ShapeDtypeStruct((s, d), out_dtype),
        grid=(s // tr,),
        in_specs=[pl.BlockSpec((tr, d), lambda i: (i, 0)),
                  pl.BlockSpec((1, d), lambda i: (0, 0))],
        out_specs=pl.BlockSpec((tr, d), lambda i: (i, 0)),
        compiler_params=_params(("parallel",)),
        name="rmsnorm",
    )(x, g.reshape(1, d))


def _rmsnorm_mix_kernel(x_ref, g_ref, o_ref, *grouped_refs, dils, tr):
    x = x_ref[...]
    y = (x * lax.rsqrt(jnp.mean(x * x, axis=-1, keepdims=True) + EPS) * g_ref[...]).astype(o_ref.dtype)
    o_ref[...] = y
    dst = lax.broadcasted_iota(jnp.int32, (tr, tr), 0)
    src = lax.broadcasted_iota(jnp.int32, (tr, tr), 1)
    for dil, s_ref in zip(dils, grouped_refs):
        n = tr // dil
        perm = (src == (dst % n) * dil + dst // n).astype(y.dtype)
        yp = jnp.dot(perm, y, preferred_element_type=jnp.float32).astype(y.dtype)
        for r in range(dil):
            s_ref[r] = yp[r * n:(r + 1) * n]


def _rmsnorm_mix(x, g, dils):
    s, d = x.shape
    tr = _tile(s, 256)
    out_shape = [jax.ShapeDtypeStruct((s, d), jnp.bfloat16)]
    out_specs = [pl.BlockSpec((tr, d), lambda i: (i, 0))]
    for dil in dils:
        out_shape.append(jax.ShapeDtypeStruct((dil, s // dil, d), jnp.bfloat16))
        out_specs.append(pl.BlockSpec((dil, tr // dil, d), lambda i: (0, i, 0)))
    return pl.pallas_call(
        functools.partial(_rmsnorm_mix_kernel, dils=dils, tr=tr),
        out_shape=out_shape,
        grid=(s // tr,),
        in_specs=[pl.BlockSpec((tr, d), lambda i: (i, 0)),
                  pl.BlockSpec((1, d), lambda i: (0, 0))],
        out_specs=out_specs,
        compiler_params=_params(("parallel",)),
        name="rmsnorm_mix",
    )(x, g.reshape(1, d))


def _matmul_kernel(x_ref, w_ref, o_ref, *, epilogue):
    acc = jnp.dot(x_ref[...], w_ref[...], preferred_element_type=jnp.float32)
    if epilogue == "sigmoid":
        acc = jax.nn.sigmoid(acc)
    o_ref[...] = acc.astype(o_ref.dtype)


def _matmul_scale_kernel(x_ref, w_ref, c_ref, o_ref):
    acc = jnp.dot(x_ref[...], w_ref[...], preferred_element_type=jnp.float32)
    o_ref[...] = (acc * c_ref[...]).astype(o_ref.dtype)


def _matmul_res_kernel(x_ref, w_ref, r_ref, o_ref):
    acc = jnp.dot(x_ref[...], w_ref[...], preferred_element_type=jnp.float32)
    o_ref[...] = r_ref[...] + acc


def _matmul(x, w, layer, col_off, ncols, out_dtype, epilogue="none", residual=None, col_scale=None,
            name="matmul"):
    m, k = x.shape
    tm = _tile(m, 1024)
    tn = _tile(math.gcd(ncols, col_off) if col_off else ncols, 1024)
    off = col_off // tn
    in_specs = [pl.BlockSpec((tm, k), lambda i, j: (i, 0)),
                pl.BlockSpec((None, k, tn), lambda i, j: (layer, 0, off + j))]
    args = [x, w]
    if col_scale is not None:
        kern = _matmul_scale_kernel
        in_specs.append(pl.BlockSpec((1, tn), lambda i, j: (0, j)))
        args.append(col_scale)
    elif residual is None:
        kern = functools.partial(_matmul_kernel, epilogue=epilogue)
    else:
        kern = _matmul_res_kernel
        in_specs.append(pl.BlockSpec((tm, tn), lambda i, j: (i, j)))
        args.append(residual)
    return pl.pallas_call(
        kern,
        out_shape=jax.ShapeDtypeStruct((m, ncols), out_dtype),
        grid=(m // tm, ncols // tn),
        in_specs=in_specs,
        out_specs=pl.BlockSpec((tm, tn), lambda i, j: (i, j)),
        compiler_params=_params(("parallel", "parallel")),
        name=name,
    )(*args)


def _merge_kernel(oa_ref, ob_ref, oc_ref, wa_ref, wb_ref, wc_ref, ga_ref, gb_ref, gc_ref, o_ref):
    f32 = jnp.float32
    ya = jnp.dot(oa_ref[...], wa_ref[...], preferred_element_type=f32)
    yb = jnp.dot(ob_ref[...], wb_ref[...], preferred_element_type=f32)
    yc = jnp.dot(oc_ref[...], wc_ref[...], preferred_element_type=f32)
    merged = (ga_ref[...].astype(f32) * ya + gb_ref[...].astype(f32) * yb
              + gc_ref[...].astype(f32) * yc)
    o_ref[...] = merged.astype(o_ref.dtype)


def _merge(o_a, o_b, o_c, gates, w_a, w_b, w_c, layer):
    s = o_a.shape[0]
    d = w_a.shape[-1]
    tm = _tile(s, 1024)
    tn = _tile(d, 512)
    nj = d // tn
    row = lambda width: pl.BlockSpec((tm, width), lambda i, j: (i, 0))
    wspec = lambda kk: pl.BlockSpec((None, kk, tn), lambda i, j: (layer, 0, j))
    gspec = lambda b: pl.BlockSpec((tm, tn), lambda i, j: (i, b * nj + j))
    return pl.pallas_call(
        _merge_kernel,
        out_shape=jax.ShapeDtypeStruct((s, d), jnp.bfloat16),
        grid=(s // tm, nj),
        in_specs=[row(A_OUT), row(B_OUT), row(C_OUT),
                  wspec(A_OUT), wspec(B_OUT), wspec(C_OUT),
                  gspec(0), gspec(1), gspec(2)],
        out_specs=pl.BlockSpec((tm, tn), lambda i, j: (i, j)),
        compiler_params=_params(("parallel", "parallel")),
        name="merge",
    )(o_a, o_b, o_c, w_a, w_b, w_c, gates, gates, gates)


def _mlp_kernel(v_ref, wu_ref, wd_ref, h_ref, o_ref):
    f = pl.program_id(1)

    @pl.when(f == 0)
    def _():
        o_ref[...] = h_ref[...]

    a = jnp.dot(v_ref[...], wu_ref[...], preferred_element_type=jnp.float32)
    a = jnp.square(jnp.maximum(a, 0.0)).astype(jnp.bfloat16)
    o_ref[...] += jnp.dot(a, wd_ref[...], preferred_element_type=jnp.float32)


def _mlp(v, h, w_up, w_down, layer):
    s, d = v.shape
    dff = w_up.shape[-1]
    tm = _tile(s, 512)
    tf = _tile(dff, 512)
    return pl.pallas_call(
        _mlp_kernel,
        out_shape=jax.ShapeDtypeStruct((s, d), jnp.float32),
        grid=(s // tm, dff // tf),
        in_specs=[pl.BlockSpec((tm, d), lambda i, f: (i, 0)),
                  pl.BlockSpec((None, d, tf), lambda i, f: (layer, 0, f)),
                  pl.BlockSpec((None, tf, d), lambda i, f: (layer, f, 0)),
                  pl.BlockSpec((tm, d), lambda i, f: (i, 0), pipeline_mode=pl.Buffered(1))],
        out_specs=pl.BlockSpec((tm, d), lambda i, f: (i, 0)),
        compiler_params=_params(("parallel", "arbitrary")),
        name="mlp",
    )(v, w_up, w_down, h)


A_TQ = 256
A_SUB = 128
A_BLK = 64
LOG2E = 1.4426950408889634
FAR = 3e32


def _halo_window(p_ref, m_ref, n_ref, cols, r0, nk, halo, tq):
    lo, hi = r0, r0 + nk
    pieces = []
    if lo < halo:
        pieces.append(p_ref[lo:min(hi, halo), cols])
    a, b = max(lo, halo), min(hi, halo + tq)
    if a < b:
        pieces.append(m_ref[a - halo:b - halo, cols])
    a = max(lo, halo + tq)
    if a < hi:
        pieces.append(n_ref[a - halo - tq:hi - halo - tq, cols])
    return pieces[0] if len(pieces) == 1 else jnp.concatenate(pieces, axis=0)


def _band_distance(i, last, sub, nsub_rows, halo, tq):
    nk = nsub_rows + 2 * halo
    a = lax.broadcasted_iota(jnp.int32, (nsub_rows, nk), 0)
    c = lax.broadcasted_iota(jnp.int32, (nsub_rows, nk), 1)
    dist = jnp.abs(c - halo - a)
    row = c + sub * nsub_rows
    valid = (dist <= halo) & ((row >= halo) | (i > 0)) & ((row < tq + halo) | (i < last))
    return jnp.where(valid, dist.astype(jnp.float32), FAR)


def _attn_a_kernel(q_ref, kp_ref, km_ref, kn_ref, vp_ref, vm_ref, vn_ref, o_ref, lse_ref, *, dil, tq):
    i = pl.program_id(1)
    last = pl.num_programs(1) - 1
    nsub = tq // A_SUB if tq >= A_SUB else 1
    rows = tq // nsub
    nk = rows + 2 * A_BLK
    lane = lax.broadcasted_iota(jnp.int32, (rows, LANE), 1)
    slopes = [LOG2E * dil * sl for sl in _alibi_slopes(A_HEADS)]
    probs = [(sub, h) for sub in range(nsub) for h in range(A_HEADS)]
    cols = lambda h: slice(h * HEAD_DIM, (h + 1) * HEAD_DIM)
    dists = [_band_distance(i, last, sub, rows, A_BLK, tq) for sub in range(nsub)]
    scores = []
    for sub, h in probs:
        k = _halo_window(kp_ref, km_ref, kn_ref, cols(h), sub * rows, nk, A_BLK, tq)
        s = lax.dot_general(q_ref[sub * rows:(sub + 1) * rows, cols(h)], k, (((1,), (1,)), ((), ())),
                            preferred_element_type=jnp.float32)
        scores.append(s - slopes[h] * dists[sub])
    es, denoms = [], []
    lse_tiles = [jnp.zeros((rows, LANE), jnp.float32) for _ in range(nsub)]
    for (sub, h), s in zip(probs, scores):
        m = jnp.max(s, axis=-1, keepdims=True)
        e = jnp.exp2(s - m)
        denom = jnp.sum(e, axis=-1, keepdims=True)
        es.append(e.astype(jnp.bfloat16))
        denoms.append(denom)
        lse_tiles[sub] = jnp.where(lane == h, m + jnp.log2(denom), lse_tiles[sub])
    outs = []
    for (sub, h), e, denom in zip(probs, es, denoms):
        v = _halo_window(vp_ref, vm_ref, vn_ref, cols(h), sub * rows, nk, A_BLK, tq)
        outs.append(jnp.dot(e, v, preferred_element_type=jnp.float32) / denom)
    o_rows = [jnp.concatenate(outs[sub * A_HEADS:(sub + 1) * A_HEADS], axis=1) for sub in range(nsub)]
    o_ref[...] = o_rows[0] if nsub == 1 else jnp.concatenate(o_rows, axis=0)
    lse_ref[...] = lse_tiles[0] if nsub == 1 else jnp.concatenate(lse_tiles, axis=0)


def _attn_a_pattern(pa, g):
    dil, ls, _ = pa.shape
    tq = _tile(ls, A_TQ)
    r64 = tq // A_BLK
    nblk64 = ls // A_BLK

    def main(c):
        return pl.BlockSpec((None, tq, A_OUT), lambda r, i: (r, i, c))

    def prev(c):
        return pl.BlockSpec((None, A_BLK, A_OUT), lambda r, i: (r, jnp.maximum(i * r64 - 1, 0), c))

    def nxt(c):
        return pl.BlockSpec((None, A_BLK, A_OUT),
                            lambda r, i: (r, jnp.minimum((i + 1) * r64, nblk64 - 1), c))

    return pl.pallas_call(
        functools.partial(_attn_a_kernel, dil=dil, tq=tq),
        out_shape=(jax.ShapeDtypeStruct((dil, ls, A_OUT), jnp.float32),
                   jax.ShapeDtypeStruct((dil, ls, LANE), jnp.float32)),
        grid=(dil, ls // tq),
        in_specs=[main(0), prev(1), main(1), nxt(1), prev(2), main(2), nxt(2)],
        out_specs=(pl.BlockSpec((None, tq, A_OUT), lambda r, i: (r, i, 0)),
                   pl.BlockSpec((None, tq, LANE), lambda r, i: (r, i, 0))),
        compiler_params=_params(("parallel", "parallel")),
        name=f"attn_a{g}",
    )(pa, pa, pa, pa, pa, pa, pa)


def _combine_a_kernel(o0_ref, o1_ref, o2_ref, l0_ref, l1_ref, l2_ref, o_ref, *scratch, dils, tr):
    o_in, l_in = [o0_ref, o1_ref, o2_ref], [l0_ref, l1_ref, l2_ref]
    nat = [scratch[g] if dil > 1 else None for g, dil in enumerate(dils)]

    def ungroup(g, src):
        dil = dils[g]
        if dil == 1:
            return src(0)
        for r in range(dil):
            nat[g][pl.ds(r, tr // dil, stride=dil), :] = src(r)
        return nat[g][...]

    l0, l1, l2 = (ungroup(g, lambda r, g=g: l_in[g][r]) for g in range(len(dils)))
    mx = jnp.maximum(jnp.maximum(l0, l1), l2)
    e0, e1, e2 = jnp.exp2(l0 - mx), jnp.exp2(l1 - mx), jnp.exp2(l2 - mx)
    inv = 1.0 / (e0 + e1 + e2)
    w = [e0 * inv, e1 * inv, e2 * inv]
    for h in range(A_HEADS):
        sl = slice(h * HEAD_DIM, (h + 1) * HEAD_DIM)
        o = sum(w[g][:, h:h + 1] * ungroup(g, lambda r, g=g: o_in[g][r, :, sl]) for g in range(len(dils)))
        o_ref[:, sl] = o.astype(o_ref.dtype)


def _attn_a(pas):
    dils = tuple(p.shape[0] for p in pas)
    s = pas[0].shape[0] * pas[0].shape[1]
    outs = [_attn_a_pattern(p, g) for g, p in enumerate(pas)]
    tr = _tile(s, 512)
    ospec = lambda dil: pl.BlockSpec((dil, tr // dil, A_OUT), lambda i: (0, i, 0))
    lspec = lambda dil: pl.BlockSpec((dil, tr // dil, LANE), lambda i: (0, i, 0))
    scratch = [pltpu.VMEM((tr, LANE), jnp.float32) for _ in dils]
    return pl.pallas_call(
        functools.partial(_combine_a_kernel, dils=dils, tr=tr),
        out_shape=jax.ShapeDtypeStruct((s, A_OUT), jnp.bfloat16),
        grid=(s // tr,),
        in_specs=[ospec(d) for d in dils] + [lspec(d) for d in dils],
        out_specs=pl.BlockSpec((tr, A_OUT), lambda i: (i, 0)),
        scratch_shapes=scratch,
        compiler_params=_params(("parallel",)),
        name="combine_a",
    )(*[o for o, _ in outs], *[l for _, l in outs])


B_TQ = 256


def _attn_b_kernel(slope_ref, sink_ref, q_ref, kp_ref, km_ref, kn_ref, vp_ref, vm_ref, vn_ref, o_ref, *, tq):
    c = pl.program_id(0)
    i = pl.program_id(1)
    last = pl.num_programs(1) - 1
    nsub = tq // B_HALF
    nk = 3 * B_HALF
    allc = slice(None)
    probs = [(sub, g) for sub in range(nsub) for g in range(B_GROUP)]
    cols = lambda g: slice(g * HEAD_DIM, (g + 1) * HEAD_DIM)
    slopes = [slope_ref[c * B_GROUP + g] for g in range(B_GROUP)]
    sinks = [sink_ref[c * B_GROUP + g] * LOG2E for g in range(B_GROUP)]
    dists = [_band_distance(i, last, sub, B_HALF, B_HALF, tq) for sub in range(nsub)]
    ks = [_halo_window(kp_ref, km_ref, kn_ref, allc, sub * B_HALF, nk, B_HALF, tq) for sub in range(nsub)]
    scores = []
    for sub, g in probs:
        s = lax.dot_general(q_ref[sub * B_HALF:(sub + 1) * B_HALF, cols(g)], ks[sub],
                            (((1,), (1,)), ((), ())), preferred_element_type=jnp.float32)
        scores.append(s - slopes[g] * dists[sub])
    es, denoms = [], []
    for (sub, g), s in zip(probs, scores):
        m = jnp.maximum(jnp.max(s, axis=-1, keepdims=True), sinks[g])
        e = jnp.exp2(s - m)
        es.append(e.astype(jnp.bfloat16))
        denoms.append(jnp.sum(e, axis=-1, keepdims=True) + jnp.exp2(sinks[g] - m))
    vs = [_halo_window(vp_ref, vm_ref, vn_ref, allc, sub * B_HALF, nk, B_HALF, tq) for sub in range(nsub)]
    outs = [(jnp.dot(e, vs[sub], preferred_element_type=jnp.float32) / denom).astype(o_ref.dtype)
            for (sub, g), e, denom in zip(probs, es, denoms)]
    o_rows = [jnp.concatenate(outs[sub * B_GROUP:(sub + 1) * B_GROUP], axis=1) for sub in range(nsub)]
    o_ref[...] = o_rows[0] if nsub == 1 else jnp.concatenate(o_rows, axis=0)


def _attn_b(qkv, sinks):
    s, nc = qkv.shape
    tq = _tile(s, B_TQ)
    r128 = tq // B_HALF
    nblk = s // B_HALF
    gw = B_GROUP * HEAD_DIM
    cq0 = 0
    ck0 = B_HEADS
    cv0 = ck0 + B_KV_HEADS
    smem = pl.BlockSpec(memory_space=pltpu.SMEM)

    def main(c0):
        return pl.BlockSpec((tq, HEAD_DIM), lambda c, i: (i, c0 + c))

    def prev(c0):
        return pl.BlockSpec((B_HALF, HEAD_DIM), lambda c, i: (jnp.maximum(i * r128 - 1, 0), c0 + c))

    def nxt(c0):
        return pl.BlockSpec((B_HALF, HEAD_DIM),
                            lambda c, i: (jnp.minimum((i + 1) * r128, nblk - 1), c0 + c))

    slopes = jnp.asarray([LOG2E * sl for sl in _alibi_slopes(B_HEADS)], jnp.float32)
    return pl.pallas_call(
        functools.partial(_attn_b_kernel, tq=tq),
        out_shape=jax.ShapeDtypeStruct((s, B_OUT), jnp.bfloat16),
        grid=(B_KV_HEADS, s // tq),
        in_specs=[smem, smem,
                  pl.BlockSpec((tq, gw), lambda c, i: (i, cq0 + c)),
                  prev(ck0), main(ck0), nxt(ck0), prev(cv0), main(cv0), nxt(cv0)],
        out_specs=pl.BlockSpec((tq, gw), lambda c, i: (i, c)),
        compiler_params=_params(("parallel", "parallel")),
        name="attn_b",
    )(slopes, sinks.astype(jnp.float32), qkv, qkv, qkv, qkv, qkv, qkv, qkv)


C_TQ = 512
C_TK = 512
C_AUG = 16
C_UNDERFLOW = 170.0


def _attn_c_kernel(slope_ref, inv_ref, q_ref, k_ref, v_ref, lq1_ref, lk1_ref, lq2_ref, lk2_ref, g_ref, o_ref,
                   vt_ref, qzt_ref, m_ref, l_ref, acc_ref, s0_ref, s1_ref, mx0_ref, mx1_ref, kn2_ref,
                   *, tq, tk, nkv, lam_init):
    f32, bf16 = jnp.float32, jnp.bfloat16
    h = pl.program_id(0)
    i = pl.program_id(1)
    slope = slope_ref[h]
    half = tk // 2
    lane_k = lax.broadcasted_iota(jnp.int32, (tk, 2 * C_QK_DIM), 1)

    @pl.when(i == 0)
    def _():
        def tbody(c, kn2):
            st = pl.multiple_of(c * tk, tk)
            vt_ref[c] = v_ref[pl.ds(st, tk), :].astype(f32).T.astype(bf16)
            kk = k_ref[pl.ds(st, tk), :].astype(f32)
            sq = kk * kk
            n_all = jnp.sum(sq, axis=1, keepdims=True)
            n_0 = jnp.sum(jnp.where(lane_k < C_QK_DIM, sq, 0.0), axis=1, keepdims=True)
            return jnp.maximum(kn2, jnp.max(jnp.maximum(n_0, n_all - n_0)))
        kn2_ref[0] = lax.fori_loop(0, nkv, tbody, jnp.float32(0.0))

    qt = q_ref[...].astype(f32).T
    row = lax.broadcasted_iota(jnp.int32, qt.shape, 0)
    qzt_ref[...] = jnp.concatenate([jnp.where(row < C_QK_DIM, qt, 0.0), jnp.where(row >= C_QK_DIM, qt, 0.0)],
                                   axis=1).astype(bf16)
    m_ref[...] = jnp.full(m_ref.shape, NEG, f32)
    l_ref[...] = jnp.zeros(l_ref.shape, f32)
    acc_ref[...] = jnp.zeros(acc_ref.shape, f32)

    def chunk_at(t):
        j = jlo + t - 1
        return jnp.where(t == 0, jd, j + (j >= jd).astype(jnp.int32))

    def score_stage(t, s_ref, mx_ref):
        j = chunk_at(t)
        dc = (j * tk + half - i * tq).astype(f32)
        coef = jnp.where(j > jd, -slope, slope)
        base = jnp.where(rowq < 3, coef, -coef * (a_q - dc))
        p1 = base.astype(bf16)
        r1 = base - p1.astype(f32)
        p2 = r1.astype(bf16)
        p3 = (r1 - p2.astype(f32)).astype(bf16)
        piece = rowq % 3
        aug_q = jnp.where(rowq < 6, jnp.where(piece == 0, p1, jnp.where(piece == 1, p2, p3)),
                          jnp.zeros_like(p1))
        kc = k_ref[pl.ds(pl.multiple_of(j * tk, tk), tk), :]
        lhs = jnp.concatenate([kc, aug_k], axis=1)
        rhs = jnp.concatenate([qzt_ref[...], aug_q, zpad], axis=0)
        st = jnp.dot(lhs, rhs, preferred_element_type=f32)
        s_ref[...] = st
        mx_ref[...] = jnp.max(st, axis=0, keepdims=True)

    def softmax_stage(t, s_ref, mx_ref):
        m_old = m_ref[...]
        m_new = jnp.maximum(m_old, mx_ref[...])
        alpha = jnp.exp2(m_old - m_new)
        p = jnp.exp2(s_ref[...] - m_new)
        l_ref[...] = alpha * l_ref[...] + jnp.sum(p, axis=0, keepdims=True)
        acc_ref[...] = alpha * acc_ref[...] + jnp.dot(vt_ref[chunk_at(t)], p.astype(bf16),
                                                      preferred_element_type=f32)
        m_ref[...] = m_new

    colk = lax.broadcasted_iota(jnp.int32, (tk, LANE), 1)
    bk = (lax.broadcasted_iota(jnp.int32, (tk, LANE), 0) - half).astype(f32)
    aug_k = jnp.where(colk < 3, bk, jnp.where(colk < 6, 1.0, 0.0)).astype(bf16)
    rowq = lax.broadcasted_iota(jnp.int32, (C_AUG, 2 * tq), 0)
    a_q = (lax.broadcasted_iota(jnp.int32, (C_AUG, 2 * tq), 1) % tq).astype(f32)
    zpad = jnp.zeros((2 * LANE - 2 * C_QK_DIM - C_AUG, 2 * tq), bf16)

    jd = (i * tq) // tk
    kd = k_ref[pl.ds(pl.multiple_of(jd * tk, tk), tk), :]
    sd = jnp.dot(kd, qzt_ref[...], preferred_element_type=f32)
    rel = (lax.broadcasted_iota(jnp.int32, (tk, tq), 0) - lax.broadcasted_iota(jnp.int32, (tk, tq), 1)
           + (jd * tk - i * tq))
    bias = slope * jnp.abs(rel).astype(f32)
    sd = sd - jnp.concatenate([bias, bias], axis=1)
    s0_ref[...] = sd
    mxd = jnp.max(sd, axis=0, keepdims=True)
    mx0_ref[...] = mxd

    sqq = qt * qt
    n_all = jnp.sum(sqq, axis=0, keepdims=True)
    n_0 = jnp.sum(jnp.where(row < C_QK_DIM, sqq, 0.0), axis=0, keepdims=True)
    s_max = jnp.max(jnp.sqrt(jnp.maximum(n_0, n_all - n_0) * kn2_ref[0]))
    reach = (s_max + C_UNDERFLOW - jnp.min(mxd)) * inv_ref[h]
    wnd = jnp.minimum(reach, float(nkv)).astype(jnp.int32) + 1
    jlo = jnp.maximum(jd - wnd, 0)
    jhi = jnp.minimum(jd + wnd, nkv - 1)
    odd = (jhi - jlo) % 2 == 0
    grow_hi = odd & (jhi < nkv - 1)
    jhi = jhi + grow_hi.astype(jnp.int32)
    jlo = jlo - (odd & ~grow_hi).astype(jnp.int32)
    nvis = jhi - jlo + 1

    def pair(u):
        score_stage(2 * u + 1, s1_ref, mx1_ref)
        softmax_stage(2 * u, s0_ref, mx0_ref)
        score_stage(2 * u + 2, s0_ref, mx0_ref)
        softmax_stage(2 * u + 1, s1_ref, mx1_ref)

    def body(w, carry):
        pair(2 * w)
        pair(2 * w + 1)
        return carry

    npair = nvis // 2 - 1
    lax.fori_loop(0, npair // 2, body, 0)

    @pl.when(npair % 2 == 1)
    def _():
        pair(npair - 1)

    score_stage(nvis - 1, s1_ref, mx1_ref)
    softmax_stage(nvis - 2, s0_ref, mx0_ref)
    softmax_stage(nvis - 1, s1_ref, mx1_ref)

    lam = (jnp.exp(jnp.sum(lq1_ref[...] * lk1_ref[...], axis=-1, keepdims=True))
           - jnp.exp(jnp.sum(lq2_ref[...] * lk2_ref[...], axis=-1, keepdims=True)) + lam_init)
    o = acc_ref[...] / l_ref[...]
    o = o[:, :tq] - lam * o[:, tq:]
    y = o * lax.rsqrt(jnp.mean(o * o, axis=0, keepdims=True) + EPS)
    o_ref[...] = (y * g_ref[...] * (1.0 - lam_init)).T.astype(o_ref.dtype)


def _attn_c(qkv, lq1, lk1, lq2, lk2, norm_g, lam_init):
    s, nc = qkv.shape
    tq = _tile(s, C_TQ)
    tk = _tile(s, C_TK)
    nkv = s // tk
    assert nkv % 2 == 0 and tk % tq == 0, (s, tq, tk)
    cq0 = B_COLS // HEAD_DIM
    ck0 = cq0 + C_HEADS
    cv0 = ck0 + C_HEADS
    smem = pl.BlockSpec(memory_space=pltpu.SMEM)
    vec = lambda n: pl.BlockSpec((1, n), lambda h, i: (0, 0))
    slopes_l2 = [LOG2E * sl for sl in _alibi_slopes(C_HEADS)]
    slopes = jnp.asarray(slopes_l2, jnp.float32)
    inv_reach = jnp.asarray([1.0 / (sl * tk) for sl in slopes_l2], jnp.float32)
    return pl.pallas_call(
        functools.partial(_attn_c_kernel, tq=tq, tk=tk, nkv=nkv, lam_init=lam_init),
        out_shape=jax.ShapeDtypeStruct((s, C_OUT), jnp.bfloat16),
        grid=(C_HEADS, s // tq),
        in_specs=[smem, smem,
                  pl.BlockSpec((tq, HEAD_DIM), lambda h, i: (i, cq0 + h)),
                  pl.BlockSpec((s, HEAD_DIM), lambda h, i: (0, ck0 + h)),
                  pl.BlockSpec((s, HEAD_DIM), lambda h, i: (0, cv0 + h)),
                  vec(C_QK_DIM), vec(C_QK_DIM), vec(C_QK_DIM), vec(C_QK_DIM),
                  pl.BlockSpec((C_V_DIM, 1), lambda h, i: (0, 0))],
        out_specs=pl.BlockSpec((tq, C_V_DIM), lambda h, i: (i, h)),
        scratch_shapes=[pltpu.VMEM((nkv, C_V_DIM, tk), jnp.bfloat16),
                        pltpu.VMEM((2 * C_QK_DIM, 2 * tq), jnp.bfloat16),
                        pltpu.VMEM((1, 2 * tq), jnp.float32),
                        pltpu.VMEM((1, 2 * tq), jnp.float32),
                        pltpu.VMEM((C_V_DIM, 2 * tq), jnp.float32),
                        pltpu.VMEM((tk, 2 * tq), jnp.float32),
                        pltpu.VMEM((tk, 2 * tq), jnp.float32),
                        pltpu.VMEM((1, 2 * tq), jnp.float32),
                        pltpu.VMEM((1, 2 * tq), jnp.float32),
                        pltpu.SMEM((1,), jnp.float32)],
        compiler_params=_params(("parallel", "arbitrary")),
        name="attn_c",
    )(slopes, inv_reach, qkv, qkv, qkv, lq1.reshape(1, -1), lk1.reshape(1, -1), lq2.reshape(1, -1),
      lk2.reshape(1, -1), norm_g.reshape(-1, 1))


def kernel(x, mix_norm_g, w_in, b_sink, diff_lq1, diff_lk1, diff_lq2, diff_lk2, diff_norm_g,
           w_branch_a, w_branch_b, w_branch_c, w_out, mlp_norm_g, w_up, w_down, final_norm_g):
    bsz, seq, d = x.shape
    depth = w_in.shape[0]
    bf16 = jnp.bfloat16
    w_in_b, w_a_b, w_b_b, w_c_b = (w.astype(bf16) for w in (w_in, w_branch_a, w_branch_b, w_branch_c))
    w_out_b, w_up_b, w_down_b = (w.astype(bf16) for w in (w_out, w_up, w_down))
    col = jnp.arange(B_COLS + C_COLS)
    bc_scale = jnp.where(col < B_HEADS * HEAD_DIM, HEAD_DIM ** -0.5 * LOG2E,
                         jnp.where((col >= B_COLS) & (col < B_COLS + C_HEADS * 2 * C_QK_DIM),
                                   C_QK_DIM ** -0.5 * LOG2E, 1.0)).astype(jnp.float32).reshape(1, -1)
    dils = tuple(dil for _, dil in A_PATTERNS)
    pat_cols = 3 * A_OUT
    a_scale = jnp.where(jnp.arange(pat_cols) < A_OUT, HEAD_DIM ** -0.5 * LOG2E,
                        1.0).astype(jnp.float32).reshape(1, -1)
    xs = x.reshape(bsz * seq, d)
    outs = []
    for b in range(bsz):
        h = xs[b * seq:(b + 1) * seq]
        for l in range(depth):
            u, *u_dil = _rmsnorm_mix(h, mix_norm_g[l], tuple(dl for dl in dils if dl > 1))
            u_by_dil = {1: u, **{dl: ud.reshape(seq, d) for dl, ud in zip([dl for dl in dils if dl > 1], u_dil)}}
            pas = [_matmul(u_by_dil[dl], w_in_b, l, g * pat_cols, pat_cols, bf16, col_scale=a_scale,
                           name=f"proj_a{g}").reshape(dl, seq // dl, pat_cols)
                   for g, dl in enumerate(dils)]
            pbc = _matmul(u, w_in_b, l, OFF_B, B_COLS + C_COLS, bf16, col_scale=bc_scale, name="proj_bc")
            gates = _matmul(u, w_in_b, l, OFF_G, N_BRANCH * d, bf16, epilogue="sigmoid", name="proj_gates")
            o_a = _attn_a(pas)
            o_b = _attn_b(pbc, b_sink[l])
            lam_init = 0.8 - 0.6 * math.exp(-0.3 * l)
            o_c = _attn_c(pbc, diff_lq1[l], diff_lk1[l], diff_lq2[l], diff_lk2[l], diff_norm_g[l], lam_init)
            merged = _merge(o_a, o_b, o_c, gates, w_a_b, w_b_b, w_c_b, l)
            h = _matmul(merged, w_out_b, l, 0, d, jnp.float32, residual=h, name="out_proj")
            v = _rmsnorm(h, mlp_norm_g[l], bf16)
            h = _mlp(v, h, w_up_b, w_down_b, l)
        outs.append(_rmsnorm(h, final_norm_g, x.dtype))
    return jnp.concatenate(outs, axis=0).reshape(bsz, seq, d)
```

```python
import functools
import math

import jax
import jax.numpy as jnp
from jax import lax
from jax.experimental import pallas as pl
from jax.experimental.pallas import tpu as pltpu

HEAD_DIM = 128
A_PATTERNS = ((128, 1), (512, 4), (2048, 16))
A_HEADS = 8
N_PAT = len(A_PATTERNS)
B_HEADS = 12
B_KV_HEADS = 4
B_GROUP = B_HEADS // B_KV_HEADS
B_HALF = 128
C_HEADS = 12
C_QK_DIM = 64
C_V_DIM = 2 * C_QK_DIM
N_BRANCH = 3
EPS = 1e-6
NEG = -1e30

A_COLS = N_PAT * 3 * A_HEADS * HEAD_DIM
B_COLS = (B_HEADS + 2 * B_KV_HEADS) * HEAD_DIM
C_COLS = C_HEADS * (4 * C_QK_DIM + C_V_DIM)
OFF_B = A_COLS
OFF_C = OFF_B + B_COLS
OFF_G = OFF_C + C_COLS
A_OUT = A_HEADS * HEAD_DIM
B_OUT = B_HEADS * HEAD_DIM
C_OUT = C_HEADS * C_V_DIM

LANE = 128
VMEM_LIMIT = 56 * 1024 * 1024


def _tile(n, pref):
    t = pref
    while t > 1 and n % t:
        t //= 2
    return t


def _params(sem):
    return pltpu.CompilerParams(dimension_semantics=sem, vmem_limit_bytes=VMEM_LIMIT)


def _alibi_slopes(n):
    return [2.0 ** (-8.0 * i / n) for i in range(1, n + 1)]


def _rmsnorm_kernel(x_ref, g_ref, o_ref):
    x = x_ref[...]
    y = x * lax.rsqrt(jnp.mean(x * x, axis=-1, keepdims=True) + EPS)
    o_ref[...] = (y * g_ref[...]).astype(o_ref.dtype)


def _rmsnorm(x, g, out_dtype):
    s, d = x.shape
    tr = _tile(s, 256)
    return pl.pallas_call(
        _rmsnorm_kernel,
        out_shape=jax.ShapeDtypeStruct((s, d), out_dtype),
        grid=(s // tr,),
        in_specs=[pl.BlockSpec((tr, d), lambda i: (i, 0)),
                  pl.BlockSpec((1, d), lambda i: (0, 0))],
        out_specs=pl.BlockSpec((tr, d), lambda i: (i, 0)),
        compiler_params=_params(("parallel",)),
        name="rmsnorm",
    )(x, g.reshape(1, d))


def _rmsnorm_mix_kernel(x_ref, g_ref, o_ref, *grouped_refs, dils, tr):
    x = x_ref[...]
    y = (x * lax.rsqrt(jnp.mean(x * x, axis=-1, keepdims=True) + EPS) * g_ref[...]).astype(o_ref.dtype)
    o_ref[...] = y
    dst = lax.broadcasted_iota(jnp.int32, (tr, tr), 0)
    src = lax.broadcasted_iota(jnp.int32, (tr, tr), 1)
    for dil, s_ref in zip(dils, grouped_refs):
        n = tr // dil
        perm = (src == (dst % n) * dil + dst // n).astype(y.dtype)
        yp = jnp.dot(perm, y, preferred_element_type=jnp.float32).astype(y.dtype)
        for r in range(dil):
            s_ref[r] = yp[r * n:(r + 1) * n]


def _rmsnorm_mix(x, g, dils):
    s, d = x.shape
    tr = _tile(s, 256)
    out_shape = [jax.ShapeDtypeStruct((s, d), jnp.bfloat16)]
    out_specs = [pl.BlockSpec((tr, d), lambda i: (i, 0))]
    for dil in dils:
        out_shape.append(jax.ShapeDtypeStruct((dil, s // dil, d), jnp.bfloat16))
        out_specs.append(pl.BlockSpec((dil, tr // dil, d), lambda i: (0, i, 0)))
    return pl.pallas_call(
        functools.partial(_rmsnorm_mix_kernel, dils=dils, tr=tr),
        out_shape=out_shape,
        grid=(s // tr,),
        in_specs=[pl.BlockSpec((tr, d), lambda i: (i, 0)),
                  pl.BlockSpec((1, d), lambda i: (0, 0))],
        out_specs=out_specs,
        compiler_params=_params(("parallel",)),
        name="rmsnorm_mix",
    )(x, g.reshape(1, d))


def _matmul_kernel(x_ref, w_ref, *rest, epilogue):
    *extra, o_ref, wb_ref = rest

    @pl.when(pl.program_id(1) == 0)
    def _():
        wb_ref[...] = w_ref[...].astype(wb_ref.dtype)

    acc = jnp.dot(x_ref[...], wb_ref[...], preferred_element_type=jnp.float32)
    if epilogue == "sigmoid":
        acc = jax.nn.sigmoid(acc)
    elif epilogue == "col_scale":
        acc = acc * extra[0][...]
    elif epilogue == "residual":
        acc = extra[0][...] + acc
    o_ref[...] = acc.astype(o_ref.dtype)


def _matmul(x, w, layer, col_off, ncols, out_dtype, epilogue="none", residual=None, col_scale=None,
            name="matmul"):
    m, k = x.shape
    tm = _tile(m, 1024)
    tn = _tile(math.gcd(ncols, col_off) if col_off else ncols, 512)
    off = col_off // tn
    in_specs = [pl.BlockSpec((tm, k), lambda j, i: (i, 0)),
                pl.BlockSpec((None, k, tn), lambda j, i: (layer, 0, off + j))]
    args = [x, w]
    if col_scale is not None:
        epilogue = "col_scale"
        in_specs.append(pl.BlockSpec((1, tn), lambda j, i: (0, j)))
        args.append(col_scale)
    elif residual is not None:
        epilogue = "residual"
        in_specs.append(pl.BlockSpec((tm, tn), lambda j, i: (i, j)))
        args.append(residual)
    return pl.pallas_call(
        functools.partial(_matmul_kernel, epilogue=epilogue),
        out_shape=jax.ShapeDtypeStruct((m, ncols), out_dtype),
        grid=(ncols // tn, m // tm),
        in_specs=in_specs,
        out_specs=pl.BlockSpec((tm, tn), lambda j, i: (i, j)),
        scratch_shapes=[pltpu.VMEM((k, tn), jnp.bfloat16)],
        compiler_params=_params(("parallel", "arbitrary")),
        name=name,
    )(*args)


def _merge_kernel(oa_ref, ob_ref, oc_ref, wa_ref, wb_ref, wc_ref, ga_ref, gb_ref, gc_ref, o_ref):
    f32 = jnp.float32
    ya = jnp.dot(oa_ref[...], wa_ref[...], preferred_element_type=f32)
    yb = jnp.dot(ob_ref[...], wb_ref[...], preferred_element_type=f32)
    yc = jnp.dot(oc_ref[...], wc_ref[...], preferred_element_type=f32)
    merged = (ga_ref[...].astype(f32) * ya + gb_ref[...].astype(f32) * yb
              + gc_ref[...].astype(f32) * yc)
    o_ref[...] = merged.astype(o_ref.dtype)


def _merge(o_a, o_b, o_c, gates, w_a, w_b, w_c, layer):
    s = o_a.shape[0]
    d = w_a.shape[-1]
    tm = _tile(s, 1024)
    tn = _tile(d, 512)
    nj = d // tn
    row = lambda width: pl.BlockSpec((tm, width), lambda i, j: (i, 0))
    wspec = lambda kk: pl.BlockSpec((None, kk, tn), lambda i, j: (layer, 0, j))
    gspec = lambda b: pl.BlockSpec((tm, tn), lambda i, j: (i, b * nj + j))
    return pl.pallas_call(
        _merge_kernel,
        out_shape=jax.ShapeDtypeStruct((s, d), jnp.bfloat16),
        grid=(s // tm, nj),
        in_specs=[row(A_OUT), row(B_OUT), row(C_OUT),
                  wspec(A_OUT), wspec(B_OUT), wspec(C_OUT),
                  gspec(0), gspec(1), gspec(2)],
        out_specs=pl.BlockSpec((tm, tn), lambda i, j: (i, j)),
        compiler_params=_params(("parallel", "parallel")),
        name="merge",
    )(o_a, o_b, o_c, w_a, w_b, w_c, gates, gates, gates)


NORM_ROWS = 64


def _mlp_kernel(v_ref, wu_ref, wd_ref, h_ref, *rest, final_norm):
    *g_ref, o_ref = rest
    f = pl.program_id(1)

    @pl.when(f == 0)
    def _():
        o_ref[...] = h_ref[...]

    a = jnp.dot(v_ref[...], wu_ref[...], preferred_element_type=jnp.float32)
    a = jnp.square(jnp.maximum(a, 0.0)).astype(jnp.bfloat16)
    o_ref[...] += jnp.dot(a, wd_ref[...], preferred_element_type=jnp.float32)

    if final_norm:
        @pl.when(f == pl.num_programs(1) - 1)
        def _():
            def norm_rows(c, carry):
                rows = pl.ds(pl.multiple_of(c * NORM_ROWS, NORM_ROWS), NORM_ROWS)
                x = o_ref[rows, :]
                o_ref[rows, :] = x * lax.rsqrt(jnp.mean(x * x, axis=-1, keepdims=True) + EPS) * g_ref[0][...]
                return carry
            lax.fori_loop(0, o_ref.shape[0] // NORM_ROWS, norm_rows, 0)


def _mlp(v, h, w_up, w_down, layer, final_g=None):
    s, d = v.shape
    dff = w_up.shape[-1]
    tm = _tile(s, 512)
    tf = _tile(dff, 512)
    in_specs = [pl.BlockSpec((tm, d), lambda i, f: (i, 0)),
                pl.BlockSpec((None, d, tf), lambda i, f: (layer, 0, f)),
                pl.BlockSpec((None, tf, d), lambda i, f: (layer, f, 0)),
                pl.BlockSpec((tm, d), lambda i, f: (i, 0), pipeline_mode=pl.Buffered(1))]
    args = [v, w_up, w_down, h]
    if final_g is not None:
        in_specs.append(pl.BlockSpec((1, d), lambda i, f: (0, 0)))
        args.append(final_g.reshape(1, d))
    return pl.pallas_call(
        functools.partial(_mlp_kernel, final_norm=final_g is not None),
        out_shape=jax.ShapeDtypeStruct((s, d), jnp.float32),
        grid=(s // tm, dff // tf),
        in_specs=in_specs,
        out_specs=pl.BlockSpec((tm, d), lambda i, f: (i, 0)),
        compiler_params=_params(("parallel", "arbitrary")),
        name="mlp",
    )(*args)


A_TQ = 256
A_SUB = 128
A_BLK = 64
LOG2E = 1.4426950408889634
FAR = 3e32


def _halo_window(p_ref, m_ref, n_ref, cols, r0, nk, halo, tq):
    lo, hi = r0, r0 + nk
    pieces = []
    if lo < halo:
        pieces.append(p_ref[lo:min(hi, halo), cols])
    a, b = max(lo, halo), min(hi, halo + tq)
    if a < b:
        pieces.append(m_ref[a - halo:b - halo, cols])
    a = max(lo, halo + tq)
    if a < hi:
        pieces.append(n_ref[a - halo - tq:hi - halo - tq, cols])
    return pieces[0] if len(pieces) == 1 else jnp.concatenate(pieces, axis=0)


def _band_distance(i, last, sub, nsub_rows, halo, tq):
    nk = nsub_rows + 2 * halo
    a = lax.broadcasted_iota(jnp.int32, (nsub_rows, nk), 0)
    c = lax.broadcasted_iota(jnp.int32, (nsub_rows, nk), 1)
    dist = jnp.abs(c - halo - a)
    row = c + sub * nsub_rows
    valid = (dist <= halo) & ((row >= halo) | (i > 0)) & ((row < tq + halo) | (i < last))
    return jnp.where(valid, dist.astype(jnp.float32), FAR)


def _attn_a_kernel(q_ref, kp_ref, km_ref, kn_ref, vp_ref, vm_ref, vn_ref, o_ref, lse_ref, *, dil, tq):
    i = pl.program_id(1)
    last = pl.num_programs(1) - 1
    nsub = tq // A_SUB if tq >= A_SUB else 1
    rows = tq // nsub
    nk = rows + 2 * A_BLK
    lane = lax.broadcasted_iota(jnp.int32, (rows, LANE), 1)
    slopes = [LOG2E * dil * sl for sl in _alibi_slopes(A_HEADS)]
    probs = [(sub, h) for sub in range(nsub) for h in range(A_HEADS)]
    cols = lambda h: slice(h * HEAD_DIM, (h + 1) * HEAD_DIM)
    dists = [_band_distance(i, last, sub, rows, A_BLK, tq) for sub in range(nsub)]
    scores = []
    for sub, h in probs:
        k = _halo_window(kp_ref, km_ref, kn_ref, cols(h), sub * rows, nk, A_BLK, tq)
        s = lax.dot_general(q_ref[sub * rows:(sub + 1) * rows, cols(h)], k, (((1,), (1,)), ((), ())),
                            preferred_element_type=jnp.float32)
        scores.append(s - slopes[h] * dists[sub])
    es, denoms = [], []
    lse_tiles = [jnp.zeros((rows, LANE), jnp.float32) for _ in range(nsub)]
    for (sub, h), s in zip(probs, scores):
        m = jnp.max(s, axis=-1, keepdims=True)
        e = jnp.exp2(s - m)
        denom = jnp.sum(e, axis=-1, keepdims=True)
        es.append(e.astype(jnp.bfloat16))
        denoms.append(denom)
        lse_tiles[sub] = jnp.where(lane == h, m + jnp.log2(denom), lse_tiles[sub])
    outs = []
    for (sub, h), e, denom in zip(probs, es, denoms):
        v = _halo_window(vp_ref, vm_ref, vn_ref, cols(h), sub * rows, nk, A_BLK, tq)
        outs.append(jnp.dot(e, v, preferred_element_type=jnp.float32) / denom)
    o_rows = [jnp.concatenate(outs[sub * A_HEADS:(sub + 1) * A_HEADS], axis=1) for sub in range(nsub)]
    o_ref[...] = o_rows[0] if nsub == 1 else jnp.concatenate(o_rows, axis=0)
    lse_ref[...] = lse_tiles[0] if nsub == 1 else jnp.concatenate(lse_tiles, axis=0)


def _attn_a_pattern(pa, g):
    dil, ls, _ = pa.shape
    tq = _tile(ls, A_TQ)
    r64 = tq // A_BLK
    nblk64 = ls // A_BLK

    def main(c):
        return pl.BlockSpec((None, tq, A_OUT), lambda r, i: (r, i, c))

    def prev(c):
        return pl.BlockSpec((None, A_BLK, A_OUT), lambda r, i: (r, jnp.maximum(i * r64 - 1, 0), c))

    def nxt(c):
        return pl.BlockSpec((None, A_BLK, A_OUT),
                            lambda r, i: (r, jnp.minimum((i + 1) * r64, nblk64 - 1), c))

    return pl.pallas_call(
        functools.partial(_attn_a_kernel, dil=dil, tq=tq),
        out_shape=(jax.ShapeDtypeStruct((dil, ls, A_OUT), jnp.float32),
                   jax.ShapeDtypeStruct((dil, ls, LANE), jnp.float32)),
        grid=(dil, ls // tq),
        in_specs=[main(0), prev(1), main(1), nxt(1), prev(2), main(2), nxt(2)],
        out_specs=(pl.BlockSpec((None, tq, A_OUT), lambda r, i: (r, i, 0)),
                   pl.BlockSpec((None, tq, LANE), lambda r, i: (r, i, 0))),
        compiler_params=_params(("parallel", "parallel")),
        name=f"attn_a{g}",
    )(pa, pa, pa, pa, pa, pa, pa)


def _combine_a_kernel(o0_ref, o1_ref, o2_ref, l0_ref, l1_ref, l2_ref, o_ref, *scratch, dils, tr):
    o_in, l_in = [o0_ref, o1_ref, o2_ref], [l0_ref, l1_ref, l2_ref]
    nat = [scratch[g] if dil > 1 else None for g, dil in enumerate(dils)]

    def ungroup(g, src):
        dil = dils[g]
        if dil == 1:
            return src(0)
        for r in range(dil):
            nat[g][pl.ds(r, tr // dil, stride=dil), :] = src(r)
        return nat[g][...]

    l0, l1, l2 = (ungroup(g, lambda r, g=g: l_in[g][r]) for g in range(len(dils)))
    mx = jnp.maximum(jnp.maximum(l0, l1), l2)
    e0, e1, e2 = jnp.exp2(l0 - mx), jnp.exp2(l1 - mx), jnp.exp2(l2 - mx)
    inv = 1.0 / (e0 + e1 + e2)
    w = [e0 * inv, e1 * inv, e2 * inv]
    for h in range(A_HEADS):
        sl = slice(h * HEAD_DIM, (h + 1) * HEAD_DIM)
        o = sum(w[g][:, h:h + 1] * ungroup(g, lambda r, g=g: o_in[g][r, :, sl]) for g in range(len(dils)))
        o_ref[:, sl] = o.astype(o_ref.dtype)


def _attn_a(pas):
    dils = tuple(p.shape[0] for p in pas)
    s = pas[0].shape[0] * pas[0].shape[1]
    outs = [_attn_a_pattern(p, g) for g, p in enumerate(pas)]
    tr = _tile(s, 512)
    ospec = lambda dil: pl.BlockSpec((dil, tr // dil, A_OUT), lambda i: (0, i, 0))
    lspec = lambda dil: pl.BlockSpec((dil, tr // dil, LANE), lambda i: (0, i, 0))
    scratch = [pltpu.VMEM((tr, LANE), jnp.float32) for _ in dils]
    return pl.pallas_call(
        functools.partial(_combine_a_kernel, dils=dils, tr=tr),
        out_shape=jax.ShapeDtypeStruct((s, A_OUT), jnp.bfloat16),
        grid=(s // tr,),
        in_specs=[ospec(d) for d in dils] + [lspec(d) for d in dils],
        out_specs=pl.BlockSpec((tr, A_OUT), lambda i: (i, 0)),
        scratch_shapes=scratch,
        compiler_params=_params(("parallel",)),
        name="combine_a",
    )(*[o for o, _ in outs], *[l for _, l in outs])


B_TQ = 256


def _attn_b_kernel(slope_ref, sink_ref, q_ref, kp_ref, km_ref, kn_ref, vp_ref, vm_ref, vn_ref, o_ref, *, tq):
    c = pl.program_id(0)
    i = pl.program_id(1)
    last = pl.num_programs(1) - 1
    nsub = tq // B_HALF
    nk = 3 * B_HALF
    allc = slice(None)
    probs = [(sub, g) for sub in range(nsub) for g in range(B_GROUP)]
    cols = lambda g: slice(g * HEAD_DIM, (g + 1) * HEAD_DIM)
    slopes = [slope_ref[c * B_GROUP + g] for g in range(B_GROUP)]
    sinks = [sink_ref[c * B_GROUP + g] * LOG2E for g in range(B_GROUP)]
    dists = [_band_distance(i, last, sub, B_HALF, B_HALF, tq) for sub in range(nsub)]
    ks = [_halo_window(kp_ref, km_ref, kn_ref, allc, sub * B_HALF, nk, B_HALF, tq) for sub in range(nsub)]
    scores = []
    for sub, g in probs:
        s = lax.dot_general(q_ref[sub * B_HALF:(sub + 1) * B_HALF, cols(g)], ks[sub],
                            (((1,), (1,)), ((), ())), preferred_element_type=jnp.float32)
        scores.append(s - slopes[g] * dists[sub])
    es, denoms = [], []
    for (sub, g), s in zip(probs, scores):
        m = jnp.maximum(jnp.max(s, axis=-1, keepdims=True), sinks[g])
        e = jnp.exp2(s - m)
        es.append(e.astype(jnp.bfloat16))
        denoms.append(jnp.sum(e, axis=-1, keepdims=True) + jnp.exp2(sinks[g] - m))
    vs = [_halo_window(vp_ref, vm_ref, vn_ref, allc, sub * B_HALF, nk, B_HALF, tq) for sub in range(nsub)]
    outs = [(jnp.dot(e, vs[sub], preferred_element_type=jnp.float32) / denom).astype(o_ref.dtype)
            for (sub, g), e, denom in zip(probs, es, denoms)]
    o_rows = [jnp.concatenate(outs[sub * B_GROUP:(sub + 1) * B_GROUP], axis=1) for sub in range(nsub)]
    o_ref[...] = o_rows[0] if nsub == 1 else jnp.concatenate(o_rows, axis=0)


def _attn_b(qkv, sinks):
    s, nc = qkv.shape
    tq = _tile(s, B_TQ)
    r128 = tq // B_HALF
    nblk = s // B_HALF
    gw = B_GROUP * HEAD_DIM
    cq0 = 0
    ck0 = B_HEADS
    cv0 = ck0 + B_KV_HEADS
    smem = pl.BlockSpec(memory_space=pltpu.SMEM)

    def main(c0):
        return pl.BlockSpec((tq, HEAD_DIM), lambda c, i: (i, c0 + c))

    def prev(c0):
        return pl.BlockSpec((B_HALF, HEAD_DIM), lambda c, i: (jnp.maximum(i * r128 - 1, 0), c0 + c))

    def nxt(c0):
        return pl.BlockSpec((B_HALF, HEAD_DIM),
                            lambda c, i: (jnp.minimum((i + 1) * r128, nblk - 1), c0 + c))

    slopes = jnp.asarray([LOG2E * sl for sl in _alibi_slopes(B_HEADS)], jnp.float32)
    return pl.pallas_call(
        functools.partial(_attn_b_kernel, tq=tq),
        out_shape=jax.ShapeDtypeStruct((s, B_OUT), jnp.bfloat16),
        grid=(B_KV_HEADS, s // tq),
        in_specs=[smem, smem,
                  pl.BlockSpec((tq, gw), lambda c, i: (i, cq0 + c)),
                  prev(ck0), main(ck0), nxt(ck0), prev(cv0), main(cv0), nxt(cv0)],
        out_specs=pl.BlockSpec((tq, gw), lambda c, i: (i, c)),
        compiler_params=_params(("parallel", "parallel")),
        name="attn_b",
    )(slopes, sinks.astype(jnp.float32), qkv, qkv, qkv, qkv, qkv, qkv, qkv)


C_TQ = 512
C_TK = 512
C_AUG = 16
C_SUMROWS = 16
C_UNDERFLOW = 170.0


def _attn_c_kernel(slope_ref, inv_ref, q_ref, k_ref, v_ref, lq1_ref, lk1_ref, lq2_ref, lk2_ref, g_ref, o_ref,
                   vt_ref, qzt_ref, m_ref, acc_ref, s0_ref, s1_ref, mx0_ref, mx1_ref, kn2_ref,
                   *, tq, tk, nkv, lam_init):
    f32, bf16 = jnp.float32, jnp.bfloat16
    h = pl.program_id(0)
    i = pl.program_id(1)
    slope = slope_ref[h]
    half = tk // 2
    lane_k = lax.broadcasted_iota(jnp.int32, (tk, 2 * C_QK_DIM), 1)

    @pl.when(i == 0)
    def _():
        def tbody(c, kn2):
            st = pl.multiple_of(c * tk, tk)
            vt = v_ref[pl.ds(st, tk), :].astype(f32).T.astype(bf16)
            ones_row = lax.broadcasted_iota(jnp.int32, (C_SUMROWS, tk), 0) == 0
            vt_ref[c] = jnp.concatenate([vt, ones_row.astype(bf16)], axis=0)
            kk = k_ref[pl.ds(st, tk), :].astype(f32)
            sq = kk * kk
            n_all = jnp.sum(sq, axis=1, keepdims=True)
            n_0 = jnp.sum(jnp.where(lane_k < C_QK_DIM, sq, 0.0), axis=1, keepdims=True)
            return jnp.maximum(kn2, jnp.max(jnp.maximum(n_0, n_all - n_0)))
        kn2_ref[0] = lax.fori_loop(0, nkv, tbody, jnp.float32(0.0))

    qt = q_ref[...].astype(f32).T
    row = lax.broadcasted_iota(jnp.int32, qt.shape, 0)
    qzt_ref[...] = jnp.concatenate([jnp.where(row < C_QK_DIM, qt, 0.0), jnp.where(row >= C_QK_DIM, qt, 0.0)],
                                   axis=1).astype(bf16)
    m_ref[...] = jnp.full(m_ref.shape, NEG, f32)
    acc_ref[...] = jnp.zeros(acc_ref.shape, f32)

    def chunk_at(t):
        j = jlo + t - 1
        return jnp.where(t == 0, jd, j + (j >= jd).astype(jnp.int32))

    def score_stage(t, s_ref, mx_ref):
        j = chunk_at(t)
        dc = (j * tk + half - i * tq).astype(f32)
        coef = jnp.where(j > jd, -slope, slope)
        base = jnp.where(rowq < 3, coef, -coef * (a_q - dc))
        p1 = base.astype(bf16)
        r1 = base - p1.astype(f32)
        p2 = r1.astype(bf16)
        p3 = (r1 - p2.astype(f32)).astype(bf16)
        piece = rowq % 3
        aug_q = jnp.where(rowq < 6, jnp.where(piece == 0, p1, jnp.where(piece == 1, p2, p3)),
                          jnp.zeros_like(p1))
        kc = k_ref[pl.ds(pl.multiple_of(j * tk, tk), tk), :]
        lhs = jnp.concatenate([kc, aug_k], axis=1)
        rhs = jnp.concatenate([qzt_ref[...], aug_q, zpad], axis=0)
        st = jnp.dot(lhs, rhs, preferred_element_type=f32)
        s_ref[...] = st
        mx_ref[...] = jnp.max(st, axis=0, keepdims=True)

    def softmax_stage(t, s_ref, mx_ref):
        m_old = m_ref[...]
        m_new = jnp.maximum(m_old, mx_ref[...])
        alpha = jnp.exp2(m_old - m_new)
        p = jnp.exp2(s_ref[...] - m_new).astype(bf16)
        acc_ref[...] = alpha * acc_ref[...] + jnp.dot(vt_ref[chunk_at(t)], p, preferred_element_type=f32)
        m_ref[...] = m_new

    colk = lax.broadcasted_iota(jnp.int32, (tk, LANE), 1)
    bk = (lax.broadcasted_iota(jnp.int32, (tk, LANE), 0) - half).astype(f32)
    aug_k = jnp.where(colk < 3, bk, jnp.where(colk < 6, 1.0, 0.0)).astype(bf16)
    rowq = lax.broadcasted_iota(jnp.int32, (C_AUG, 2 * tq), 0)
    a_q = (lax.broadcasted_iota(jnp.int32, (C_AUG, 2 * tq), 1) % tq).astype(f32)
    zpad = jnp.zeros((2 * LANE - 2 * C_QK_DIM - C_AUG, 2 * tq), bf16)

    jd = (i * tq) // tk
    kd = k_ref[pl.ds(pl.multiple_of(jd * tk, tk), tk), :]
    sd = jnp.dot(kd, qzt_ref[...], preferred_element_type=f32)
    rel = (lax.broadcasted_iota(jnp.int32, (tk, tq), 0) - lax.broadcasted_iota(jnp.int32, (tk, tq), 1)
           + (jd * tk - i * tq))
    bias = slope * jnp.abs(rel).astype(f32)
    sd = sd - jnp.concatenate([bias, bias], axis=1)
    s0_ref[...] = sd
    mxd = jnp.max(sd, axis=0, keepdims=True)
    mx0_ref[...] = mxd

    sqq = qt * qt
    n_all = jnp.sum(sqq, axis=0, keepdims=True)
    n_0 = jnp.sum(jnp.where(row < C_QK_DIM, sqq, 0.0), axis=0, keepdims=True)
    s_max = jnp.max(jnp.sqrt(jnp.maximum(n_0, n_all - n_0) * kn2_ref[0]))
    reach = (s_max + C_UNDERFLOW - jnp.min(mxd)) * inv_ref[h]
    wnd = jnp.minimum(reach, float(nkv)).astype(jnp.int32) + 1
    jlo = jnp.maximum(jd - wnd, 0)
    jhi = jnp.minimum(jd + wnd, nkv - 1)
    odd = (jhi - jlo) % 2 == 0
    grow_hi = odd & (jhi < nkv - 1)
    jhi = jhi + grow_hi.astype(jnp.int32)
    jlo = jlo - (odd & ~grow_hi).astype(jnp.int32)
    nvis = jhi - jlo + 1

    def pair(u):
        score_stage(2 * u + 1, s1_ref, mx1_ref)
        softmax_stage(2 * u, s0_ref, mx0_ref)
        score_stage(2 * u + 2, s0_ref, mx0_ref)
        softmax_stage(2 * u + 1, s1_ref, mx1_ref)

    def body(w, carry):
        pair(2 * w)
        pair(2 * w + 1)
        return carry

    npair = nvis // 2 - 1
    lax.fori_loop(0, npair // 2, body, 0)

    @pl.when(npair % 2 == 1)
    def _():
        pair(npair - 1)

    score_stage(nvis - 1, s1_ref, mx1_ref)
    softmax_stage(nvis - 2, s0_ref, mx0_ref)
    softmax_stage(nvis - 1, s1_ref, mx1_ref)

    lam = (jnp.exp(jnp.sum(lq1_ref[...] * lk1_ref[...], axis=-1, keepdims=True))
           - jnp.exp(jnp.sum(lq2_ref[...] * lk2_ref[...], axis=-1, keepdims=True)) + lam_init)
    o = acc_ref[:C_V_DIM, :] / acc_ref[C_V_DIM:C_V_DIM + 1, :]
    o = o[:, :tq] - lam * o[:, tq:]
    y = o * lax.rsqrt(jnp.mean(o * o, axis=0, keepdims=True) + EPS)
    o_ref[...] = (y * g_ref[...] * (1.0 - lam_init)).T.astype(o_ref.dtype)


def _attn_c(qkv, lq1, lk1, lq2, lk2, norm_g, lam_init):
    s, nc = qkv.shape
    tq = _tile(s, C_TQ)
    tk = _tile(s, C_TK)
    nkv = s // tk
    assert nkv % 2 == 0 and tk % tq == 0, (s, tq, tk)
    cq0 = B_COLS // HEAD_DIM
    ck0 = cq0 + C_HEADS
    cv0 = ck0 + C_HEADS
    smem = pl.BlockSpec(memory_space=pltpu.SMEM)
    vec = lambda n: pl.BlockSpec((1, n), lambda h, i: (0, 0))
    slopes_l2 = [LOG2E * sl for sl in _alibi_slopes(C_HEADS)]
    slopes = jnp.asarray(slopes_l2, jnp.float32)
    inv_reach = jnp.asarray([1.0 / (sl * tk) for sl in slopes_l2], jnp.float32)
    return pl.pallas_call(
        functools.partial(_attn_c_kernel, tq=tq, tk=tk, nkv=nkv, lam_init=lam_init),
        out_shape=jax.ShapeDtypeStruct((s, C_OUT), jnp.bfloat16),
        grid=(C_HEADS, s // tq),
        in_specs=[smem, smem,
                  pl.BlockSpec((tq, HEAD_DIM), lambda h, i: (i, cq0 + h)),
                  pl.BlockSpec((s, HEAD_DIM), lambda h, i: (0, ck0 + h)),
                  pl.BlockSpec((s, HEAD_DIM), lambda h, i: (0, cv0 + h)),
                  vec(C_QK_DIM), vec(C_QK_DIM), vec(C_QK_DIM), vec(C_QK_DIM),
                  pl.BlockSpec((C_V_DIM, 1), lambda h, i: (0, 0))],
        out_specs=pl.BlockSpec((tq, C_V_DIM), lambda h, i: (i, h)),
        scratch_shapes=[pltpu.VMEM((nkv, C_V_DIM + C_SUMROWS, tk), jnp.bfloat16),
                        pltpu.VMEM((2 * C_QK_DIM, 2 * tq), jnp.bfloat16),
                        pltpu.VMEM((1, 2 * tq), jnp.float32),
                        pltpu.VMEM((C_V_DIM + C_SUMROWS, 2 * tq), jnp.float32),
                        pltpu.VMEM((tk, 2 * tq), jnp.float32),
                        pltpu.VMEM((tk, 2 * tq), jnp.float32),
                        pltpu.VMEM((1, 2 * tq), jnp.float32),
                        pltpu.VMEM((1, 2 * tq), jnp.float32),
                        pltpu.SMEM((1,), jnp.float32)],
        compiler_params=_params(("parallel", "arbitrary")),
        name="attn_c",
    )(slopes, inv_reach, qkv, qkv, qkv, lq1.reshape(1, -1), lk1.reshape(1, -1), lq2.reshape(1, -1),
      lk2.reshape(1, -1), norm_g.reshape(-1, 1))


def kernel(x, mix_norm_g, w_in, b_sink, diff_lq1, diff_lk1, diff_lq2, diff_lk2, diff_norm_g,
           w_branch_a, w_branch_b, w_branch_c, w_out, mlp_norm_g, w_up, w_down, final_norm_g):
    bsz, seq, d = x.shape
    depth = w_in.shape[0]
    bf16 = jnp.bfloat16
    w_a_b, w_b_b, w_c_b = (w.astype(bf16) for w in (w_branch_a, w_branch_b, w_branch_c))
    w_up_b, w_down_b = (w.astype(bf16) for w in (w_up, w_down))
    col = jnp.arange(B_COLS + C_COLS)
    bc_scale = jnp.where(col < B_HEADS * HEAD_DIM, HEAD_DIM ** -0.5 * LOG2E,
                         jnp.where((col >= B_COLS) & (col < B_COLS + C_HEADS * 2 * C_QK_DIM),
                                   C_QK_DIM ** -0.5 * LOG2E, 1.0)).astype(jnp.float32).reshape(1, -1)
    dils = tuple(dil for _, dil in A_PATTERNS)
    pat_cols = 3 * A_OUT
    a_scale = jnp.where(jnp.arange(pat_cols) < A_OUT, HEAD_DIM ** -0.5 * LOG2E,
                        1.0).astype(jnp.float32).reshape(1, -1)
    xs = x.reshape(bsz * seq, d)
    outs = []
    for b in range(bsz):
        h = xs[b * seq:(b + 1) * seq]
        for l in range(depth):
            u, *u_dil = _rmsnorm_mix(h, mix_norm_g[l], tuple(dl for dl in dils if dl > 1))
            u_by_dil = {1: u, **{dl: ud.reshape(seq, d) for dl, ud in zip([dl for dl in dils if dl > 1], u_dil)}}
            pas = [_matmul(u_by_dil[dl], w_in, l, g * pat_cols, pat_cols, bf16, col_scale=a_scale,
                           name=f"proj_a{g}").reshape(dl, seq // dl, pat_cols)
                   for g, dl in enumerate(dils)]
            pbc = _matmul(u, w_in, l, OFF_B, B_COLS + C_COLS, bf16, col_scale=bc_scale, name="proj_bc")
            gates = _matmul(u, w_in, l, OFF_G, N_BRANCH * d, bf16, epilogue="sigmoid", name="proj_gates")
            o_a = _attn_a(pas)
            o_b = _attn_b(pbc, b_sink[l])
            lam_init = 0.8 - 0.6 * math.exp(-0.3 * l)
            o_c = _attn_c(pbc, diff_lq1[l], diff_lk1[l], diff_lq2[l], diff_lk2[l], diff_norm_g[l], lam_init)
            merged = _merge(o_a, o_b, o_c, gates, w_a_b, w_b_b, w_c_b, l)
            h = _matmul(merged, w_out, l, 0, d, jnp.float32, residual=h, name="out_proj")
            v = _rmsnorm(h, mlp_norm_g[l], bf16)
            h = _mlp(v, h, w_up_b, w_down_b, l, final_g=final_norm_g if l == depth - 1 else None)
        outs.append(h.astype(x.dtype))
    return jnp.concatenate(outs, axis=0).reshape(bsz, seq, d)
```

```python
import functools
import math

import jax
import jax.numpy as jnp
from jax import lax
from jax.experimental import pallas as pl
from jax.experimental.pallas import tpu as pltpu

HEAD_DIM = 128
A_PATTERNS = ((128, 1), (512, 4), (2048, 16))
A_HEADS = 8
N_PAT = len(A_PATTERNS)
B_HEADS = 12
B_KV_HEADS = 4
B_GROUP = B_HEADS // B_KV_HEADS
B_HALF = 128
C_HEADS = 12
C_QK_DIM = 64
C_V_DIM = 2 * C_QK_DIM
N_BRANCH = 3
EPS = 1e-6
NEG = -1e30

A_COLS = N_PAT * 3 * A_HEADS * HEAD_DIM
B_COLS = (B_HEADS + 2 * B_KV_HEADS) * HEAD_DIM
C_COLS = C_HEADS * (4 * C_QK_DIM + C_V_DIM)
OFF_B = A_COLS
OFF_C = OFF_B + B_COLS
OFF_G = OFF_C + C_COLS
A_OUT = A_HEADS * HEAD_DIM
B_OUT = B_HEADS * HEAD_DIM
C_OUT = C_HEADS * C_V_DIM

LANE = 128
VMEM_LIMIT = 56 * 1024 * 1024


def _tile(n, pref):
    t = pref
    while t > 1 and n % t:
        t //= 2
    return t


def _params(sem):
    return pltpu.CompilerParams(dimension_semantics=sem, vmem_limit_bytes=VMEM_LIMIT)


def _alibi_slopes(n):
    return [2.0 ** (-8.0 * i / n) for i in range(1, n + 1)]


def _rmsnorm_mix_kernel(x_ref, g_ref, o_ref, *grouped_refs, dils, tr):
    x = x_ref[...]
    y = (x * lax.rsqrt(jnp.mean(x * x, axis=-1, keepdims=True) + EPS) * g_ref[...]).astype(o_ref.dtype)
    o_ref[...] = y
    dst = lax.broadcasted_iota(jnp.int32, (tr, tr), 0)
    src = lax.broadcasted_iota(jnp.int32, (tr, tr), 1)
    for dil, s_ref in zip(dils, grouped_refs):
        n = tr // dil
        perm = (src == (dst % n) * dil + dst // n).astype(y.dtype)
        yp = jnp.dot(perm, y, preferred_element_type=jnp.float32).astype(y.dtype)
        for r in range(dil):
            s_ref[r] = yp[r * n:(r + 1) * n]


def _rmsnorm_mix(x, g, dils):
    s, d = x.shape
    tr = _tile(s, 256)
    out_shape = [jax.ShapeDtypeStruct((s, d), jnp.bfloat16)]
    out_specs = [pl.BlockSpec((tr, d), lambda i: (i, 0))]
    for dil in dils:
        out_shape.append(jax.ShapeDtypeStruct((dil, s // dil, d), jnp.bfloat16))
        out_specs.append(pl.BlockSpec((dil, tr // dil, d), lambda i: (0, i, 0)))
    return pl.pallas_call(
        functools.partial(_rmsnorm_mix_kernel, dils=dils, tr=tr),
        out_shape=out_shape,
        grid=(s // tr,),
        in_specs=[pl.BlockSpec((tr, d), lambda i: (i, 0)),
                  pl.BlockSpec((1, d), lambda i: (0, 0))],
        out_specs=out_specs,
        compiler_params=_params(("parallel",)),
        name="rmsnorm_mix",
    )(x, g.reshape(1, d))


def _matmul_kernel(x_ref, w_ref, *rest, epilogue):
    *extra, o_ref = rest
    acc = jnp.dot(x_ref[...], w_ref[...], preferred_element_type=jnp.float32)
    if epilogue == "sigmoid":
        acc = jax.nn.sigmoid(acc)
    elif epilogue == "col_scale":
        acc = acc * extra[0][...]
    elif epilogue == "residual":
        acc = extra[0][...] + acc
    o_ref[...] = acc.astype(o_ref.dtype)


def _matmul(x, w, layer, col_off, ncols, out_dtype, epilogue="none", residual=None, col_scale=None,
            name="matmul"):
    m, k = x.shape
    tm = _tile(m, 1024)
    tn = _tile(math.gcd(ncols, col_off) if col_off else ncols, 1024)
    off = col_off // tn
    in_specs = [pl.BlockSpec((tm, k), lambda i, j: (i, 0)),
                pl.BlockSpec((None, k, tn), lambda i, j: (layer, 0, off + j))]
    args = [x, w]
    if col_scale is not None:
        epilogue = "col_scale"
        in_specs.append(pl.BlockSpec((1, tn), lambda i, j: (0, j)))
        args.append(col_scale)
    elif residual is not None:
        epilogue = "residual"
        in_specs.append(pl.BlockSpec((tm, tn), lambda i, j: (i, j)))
        args.append(residual)
    return pl.pallas_call(
        functools.partial(_matmul_kernel, epilogue=epilogue),
        out_shape=jax.ShapeDtypeStruct((m, ncols), out_dtype),
        grid=(m // tm, ncols // tn),
        in_specs=in_specs,
        out_specs=pl.BlockSpec((tm, tn), lambda i, j: (i, j)),
        compiler_params=_params(("parallel", "parallel")),
        name=name,
    )(*args)


def _merge_kernel(oa_ref, ob_ref, oc_ref, wa_ref, wb_ref, wc_ref, ga_ref, gb_ref, gc_ref, o_ref):
    f32 = jnp.float32
    ya = jnp.dot(oa_ref[...], wa_ref[...], preferred_element_type=f32)
    yb = jnp.dot(ob_ref[...], wb_ref[...], preferred_element_type=f32)
    yc = jnp.dot(oc_ref[...], wc_ref[...], preferred_element_type=f32)
    merged = (ga_ref[...].astype(f32) * ya + gb_ref[...].astype(f32) * yb
              + gc_ref[...].astype(f32) * yc)
    o_ref[...] = merged.astype(o_ref.dtype)


def _merge(o_a, o_b, o_c, gates, w_a, w_b, w_c, layer):
    s = o_a.shape[0]
    d = w_a.shape[-1]
    tm = _tile(s, 1024)
    tn = _tile(d, 512)
    nj = d // tn
    row = lambda width: pl.BlockSpec((tm, width), lambda i, j: (i, 0))
    wspec = lambda kk: pl.BlockSpec((None, kk, tn), lambda i, j: (layer, 0, j))
    gspec = lambda b: pl.BlockSpec((tm, tn), lambda i, j: (i, b * nj + j))
    return pl.pallas_call(
        _merge_kernel,
        out_shape=jax.ShapeDtypeStruct((s, d), jnp.bfloat16),
        grid=(s // tm, nj),
        in_specs=[row(A_OUT), row(B_OUT), row(C_OUT),
                  wspec(A_OUT), wspec(B_OUT), wspec(C_OUT),
                  gspec(0), gspec(1), gspec(2)],
        out_specs=pl.BlockSpec((tm, tn), lambda i, j: (i, j)),
        compiler_params=_params(("parallel", "parallel")),
        name="merge",
    )(o_a, o_b, o_c, w_a, w_b, w_c, gates, gates, gates)


NORM_ROWS = 64


def _mlp_kernel(h_ref, gin_ref, wu_ref, wd_ref, *rest, final_norm):
    *gout_ref, o_ref, v_ref = rest
    f = pl.program_id(1)

    def norm_rows(src_ref, dst_ref, g_ref):
        def step(c, carry):
            rows = pl.ds(pl.multiple_of(c * NORM_ROWS, NORM_ROWS), NORM_ROWS)
            x = src_ref[rows, :]
            y = x * lax.rsqrt(jnp.mean(x * x, axis=-1, keepdims=True) + EPS) * g_ref[...]
            dst_ref[rows, :] = y.astype(dst_ref.dtype)
            return carry
        lax.fori_loop(0, src_ref.shape[0] // NORM_ROWS, step, 0)

    @pl.when(f == 0)
    def _():
        o_ref[...] = h_ref[...]
        norm_rows(h_ref, v_ref, gin_ref)

    a = jnp.dot(v_ref[...], wu_ref[...], preferred_element_type=jnp.float32)
    a = jnp.square(jnp.maximum(a, 0.0)).astype(jnp.bfloat16)
    o_ref[...] += jnp.dot(a, wd_ref[...], preferred_element_type=jnp.float32)

    if final_norm:
        @pl.when(f == pl.num_programs(1) - 1)
        def _():
            norm_rows(o_ref, o_ref, gout_ref[0])


def _mlp(h, g_in, w_up, w_down, layer, final_g=None):
    s, d = h.shape
    dff = w_up.shape[-1]
    tm = _tile(s, 512)
    tf = _tile(dff, 512)
    vec = pl.BlockSpec((1, d), lambda i, f: (0, 0))
    in_specs = [pl.BlockSpec((tm, d), lambda i, f: (i, 0), pipeline_mode=pl.Buffered(1)),
                vec,
                pl.BlockSpec((None, d, tf), lambda i, f: (layer, 0, f)),
                pl.BlockSpec((None, tf, d), lambda i, f: (layer, f, 0))]
    args = [h, g_in.reshape(1, d), w_up, w_down]
    if final_g is not None:
        in_specs.append(vec)
        args.append(final_g.reshape(1, d))
    return pl.pallas_call(
        functools.partial(_mlp_kernel, final_norm=final_g is not None),
        out_shape=jax.ShapeDtypeStruct((s, d), jnp.float32),
        grid=(s // tm, dff // tf),
        in_specs=in_specs,
        out_specs=pl.BlockSpec((tm, d), lambda i, f: (i, 0)),
        scratch_shapes=[pltpu.VMEM((tm, d), jnp.bfloat16)],
        compiler_params=_params(("parallel", "arbitrary")),
        name="mlp",
    )(*args)


A_TQ = 256
A_SUB = 128
A_BLK = 64
LOG2E = 1.4426950408889634
FAR = 3e32


def _halo_window(p_ref, m_ref, n_ref, cols, r0, nk, halo, tq):
    lo, hi = r0, r0 + nk
    pieces = []
    if lo < halo:
        pieces.append(p_ref[lo:min(hi, halo), cols])
    a, b = max(lo, halo), min(hi, halo + tq)
    if a < b:
        pieces.append(m_ref[a - halo:b - halo, cols])
    a = max(lo, halo + tq)
    if a < hi:
        pieces.append(n_ref[a - halo - tq:hi - halo - tq, cols])
    return pieces[0] if len(pieces) == 1 else jnp.concatenate(pieces, axis=0)


def _band_distance(i, last, sub, nsub_rows, halo, tq):
    nk = nsub_rows + 2 * halo
    a = lax.broadcasted_iota(jnp.int32, (nsub_rows, nk), 0)
    c = lax.broadcasted_iota(jnp.int32, (nsub_rows, nk), 1)
    dist = jnp.abs(c - halo - a)
    row = c + sub * nsub_rows
    valid = (dist <= halo) & ((row >= halo) | (i > 0)) & ((row < tq + halo) | (i < last))
    return jnp.where(valid, dist.astype(jnp.float32), FAR)


def _attn_a_kernel(q_ref, kp_ref, km_ref, kn_ref, vp_ref, vm_ref, vn_ref, o_ref, lse_ref, *, dil, tq):
    i = pl.program_id(1)
    last = pl.num_programs(1) - 1
    nsub = tq // A_SUB if tq >= A_SUB else 1
    rows = tq // nsub
    nk = rows + 2 * A_BLK
    lane = lax.broadcasted_iota(jnp.int32, (rows, LANE), 1)
    slopes = [LOG2E * dil * sl for sl in _alibi_slopes(A_HEADS)]
    probs = [(sub, h) for sub in range(nsub) for h in range(A_HEADS)]
    cols = lambda h: slice(h * HEAD_DIM, (h + 1) * HEAD_DIM)
    dists = [_band_distance(i, last, sub, rows, A_BLK, tq) for sub in range(nsub)]
    scores = []
    for sub, h in probs:
        k = _halo_window(kp_ref, km_ref, kn_ref, cols(h), sub * rows, nk, A_BLK, tq)
        s = lax.dot_general(q_ref[sub * rows:(sub + 1) * rows, cols(h)], k, (((1,), (1,)), ((), ())),
                            preferred_element_type=jnp.float32)
        scores.append(s - slopes[h] * dists[sub])
    es, denoms = [], []
    lse_tiles = [jnp.zeros((rows, LANE), jnp.float32) for _ in range(nsub)]
    for (sub, h), s in zip(probs, scores):
        m = jnp.max(s, axis=-1, keepdims=True)
        e = jnp.exp2(s - m)
        denom = jnp.sum(e, axis=-1, keepdims=True)
        es.append(e.astype(jnp.bfloat16))
        denoms.append(denom)
        lse_tiles[sub] = jnp.where(lane == h, m + jnp.log2(denom), lse_tiles[sub])
    outs = []
    for (sub, h), e, denom in zip(probs, es, denoms):
        v = _halo_window(vp_ref, vm_ref, vn_ref, cols(h), sub * rows, nk, A_BLK, tq)
        outs.append(jnp.dot(e, v, preferred_element_type=jnp.float32) / denom)
    o_rows = [jnp.concatenate(outs[sub * A_HEADS:(sub + 1) * A_HEADS], axis=1) for sub in range(nsub)]
    o_ref[...] = o_rows[0] if nsub == 1 else jnp.concatenate(o_rows, axis=0)
    lse_ref[...] = lse_tiles[0] if nsub == 1 else jnp.concatenate(lse_tiles, axis=0)


def _attn_a_pattern(pa, g):
    dil, ls, _ = pa.shape
    tq = _tile(ls, A_TQ)
    r64 = tq // A_BLK
    nblk64 = ls // A_BLK

    def main(c):
        return pl.BlockSpec((None, tq, A_OUT), lambda r, i: (r, i, c))

    def prev(c):
        return pl.BlockSpec((None, A_BLK, A_OUT), lambda r, i: (r, jnp.maximum(i * r64 - 1, 0), c))

    def nxt(c):
        return pl.BlockSpec((None, A_BLK, A_OUT),
                            lambda r, i: (r, jnp.minimum((i + 1) * r64, nblk64 - 1), c))

    return pl.pallas_call(
        functools.partial(_attn_a_kernel, dil=dil, tq=tq),
        out_shape=(jax.ShapeDtypeStruct((dil, ls, A_OUT), jnp.float32),
                   jax.ShapeDtypeStruct((dil, ls, LANE), jnp.float32)),
        grid=(dil, ls // tq),
        in_specs=[main(0), prev(1), main(1), nxt(1), prev(2), main(2), nxt(2)],
        out_specs=(pl.BlockSpec((None, tq, A_OUT), lambda r, i: (r, i, 0)),
                   pl.BlockSpec((None, tq, LANE), lambda r, i: (r, i, 0))),
        compiler_params=_params(("parallel", "parallel")),
        name=f"attn_a{g}",
    )(pa, pa, pa, pa, pa, pa, pa)


def _combine_a_kernel(o0_ref, o1_ref, o2_ref, l0_ref, l1_ref, l2_ref, o_ref, *scratch, dils, tr):
    o_in, l_in = [o0_ref, o1_ref, o2_ref], [l0_ref, l1_ref, l2_ref]
    nat = [scratch[g] if dil > 1 else None for g, dil in enumerate(dils)]

    def ungroup(g, src):
        dil = dils[g]
        if dil == 1:
            return src(0)
        for r in range(dil):
            nat[g][pl.ds(r, tr // dil, stride=dil), :] = src(r)
        return nat[g][...]

    l0, l1, l2 = (ungroup(g, lambda r, g=g: l_in[g][r]) for g in range(len(dils)))
    mx = jnp.maximum(jnp.maximum(l0, l1), l2)
    e0, e1, e2 = jnp.exp2(l0 - mx), jnp.exp2(l1 - mx), jnp.exp2(l2 - mx)
    inv = 1.0 / (e0 + e1 + e2)
    w = [e0 * inv, e1 * inv, e2 * inv]
    for h in range(A_HEADS):
        sl = slice(h * HEAD_DIM, (h + 1) * HEAD_DIM)
        o = sum(w[g][:, h:h + 1] * ungroup(g, lambda r, g=g: o_in[g][r, :, sl]) for g in range(len(dils)))
        o_ref[:, sl] = o.astype(o_ref.dtype)


def _attn_a(pas):
    dils = tuple(p.shape[0] for p in pas)
    s = pas[0].shape[0] * pas[0].shape[1]
    outs = [_attn_a_pattern(p, g) for g, p in enumerate(pas)]
    tr = _tile(s, 512)
    ospec = lambda dil: pl.BlockSpec((dil, tr // dil, A_OUT), lambda i: (0, i, 0))
    lspec = lambda dil: pl.BlockSpec((dil, tr // dil, LANE), lambda i: (0, i, 0))
    scratch = [pltpu.VMEM((tr, LANE), jnp.float32) for _ in dils]
    return pl.pallas_call(
        functools.partial(_combine_a_kernel, dils=dils, tr=tr),
        out_shape=jax.ShapeDtypeStruct((s, A_OUT), jnp.bfloat16),
        grid=(s // tr,),
        in_specs=[ospec(d) for d in dils] + [lspec(d) for d in dils],
        out_specs=pl.BlockSpec((tr, A_OUT), lambda i: (i, 0)),
        scratch_shapes=scratch,
        compiler_params=_params(("parallel",)),
        name="combine_a",
    )(*[o for o, _ in outs], *[l for _, l in outs])


B_TQ = 256


def _attn_b_kernel(slope_ref, sink_ref, q_ref, kp_ref, km_ref, kn_ref, vp_ref, vm_ref, vn_ref, o_ref, *, tq):
    c = pl.program_id(0)
    i = pl.program_id(1)
    last = pl.num_programs(1) - 1
    nsub = tq // B_HALF
    nk = 3 * B_HALF
    allc = slice(None)
    probs = [(sub, g) for sub in range(nsub) for g in range(B_GROUP)]
    cols = lambda g: slice(g * HEAD_DIM, (g + 1) * HEAD_DIM)
    slopes = [slope_ref[c * B_GROUP + g] for g in range(B_GROUP)]
    sinks = [sink_ref[c * B_GROUP + g] * LOG2E for g in range(B_GROUP)]
    dists = [_band_distance(i, last, sub, B_HALF, B_HALF, tq) for sub in range(nsub)]
    ks = [_halo_window(kp_ref, km_ref, kn_ref, allc, sub * B_HALF, nk, B_HALF, tq) for sub in range(nsub)]
    scores = []
    for sub, g in probs:
        s = lax.dot_general(q_ref[sub * B_HALF:(sub + 1) * B_HALF, cols(g)], ks[sub],
                            (((1,), (1,)), ((), ())), preferred_element_type=jnp.float32)
        scores.append(s - slopes[g] * dists[sub])
    es, denoms = [], []
    for (sub, g), s in zip(probs, scores):
        m = jnp.maximum(jnp.max(s, axis=-1, keepdims=True), sinks[g])
        e = jnp.exp2(s - m)
        es.append(e.astype(jnp.bfloat16))
        denoms.append(jnp.sum(e, axis=-1, keepdims=True) + jnp.exp2(sinks[g] - m))
    vs = [_halo_window(vp_ref, vm_ref, vn_ref, allc, sub * B_HALF, nk, B_HALF, tq) for sub in range(nsub)]
    outs = [(jnp.dot(e, vs[sub], preferred_element_type=jnp.float32) / denom).astype(o_ref.dtype)
            for (sub, g), e, denom in zip(probs, es, denoms)]
    o_rows = [jnp.concatenate(outs[sub * B_GROUP:(sub + 1) * B_GROUP], axis=1) for sub in range(nsub)]
    o_ref[...] = o_rows[0] if nsub == 1 else jnp.concatenate(o_rows, axis=0)


def _attn_b(qkv, sinks):
    s, nc = qkv.shape
    tq = _tile(s, B_TQ)
    r128 = tq // B_HALF
    nblk = s // B_HALF
    gw = B_GROUP * HEAD_DIM
    cq0 = 0
    ck0 = B_HEADS
    cv0 = ck0 + B_KV_HEADS
    smem = pl.BlockSpec(memory_space=pltpu.SMEM)

    def main(c0):
        return pl.BlockSpec((tq, HEAD_DIM), lambda c, i: (i, c0 + c))

    def prev(c0):
        return pl.BlockSpec((B_HALF, HEAD_DIM), lambda c, i: (jnp.maximum(i * r128 - 1, 0), c0 + c))

    def nxt(c0):
        return pl.BlockSpec((B_HALF, HEAD_DIM),
                            lambda c, i: (jnp.minimum((i + 1) * r128, nblk - 1), c0 + c))

    slopes = jnp.asarray([LOG2E * sl for sl in _alibi_slopes(B_HEADS)], jnp.float32)
    return pl.pallas_call(
        functools.partial(_attn_b_kernel, tq=tq),
        out_shape=jax.ShapeDtypeStruct((s, B_OUT), jnp.bfloat16),
        grid=(B_KV_HEADS, s // tq),
        in_specs=[smem, smem,
                  pl.BlockSpec((tq, gw), lambda c, i: (i, cq0 + c)),
                  prev(ck0), main(ck0), nxt(ck0), prev(cv0), main(cv0), nxt(cv0)],
        out_specs=pl.BlockSpec((tq, gw), lambda c, i: (i, c)),
        compiler_params=_params(("parallel", "parallel")),
        name="attn_b",
    )(slopes, sinks.astype(jnp.float32), qkv, qkv, qkv, qkv, qkv, qkv, qkv)


C_TQ = 512
C_TK = 512
C_AUG = 16
C_SUMROWS = 16
C_UNDERFLOW = 170.0


def _attn_c_kernel(slope_ref, inv_ref, q_ref, k_ref, v_ref, lq1_ref, lk1_ref, lq2_ref, lk2_ref, g_ref, o_ref,
                   vt_ref, qzt_ref, m_ref, acc_ref, s0_ref, s1_ref, mx0_ref, mx1_ref, kn2_ref,
                   *, tq, tk, nkv, lam_init):
    f32, bf16 = jnp.float32, jnp.bfloat16
    h = pl.program_id(0)
    i = pl.program_id(1)
    slope = slope_ref[h]
    half = tk // 2
    lane_k = lax.broadcasted_iota(jnp.int32, (tk, 2 * C_QK_DIM), 1)

    @pl.when(i == 0)
    def _():
        def tbody(c, kn2):
            st = pl.multiple_of(c * tk, tk)
            vt = v_ref[pl.ds(st, tk), :].astype(f32).T.astype(bf16)
            ones_row = lax.broadcasted_iota(jnp.int32, (C_SUMROWS, tk), 0) == 0
            vt_ref[c] = jnp.concatenate([vt, ones_row.astype(bf16)], axis=0)
            kk = k_ref[pl.ds(st, tk), :].astype(f32)
            sq = kk * kk
            n_all = jnp.sum(sq, axis=1, keepdims=True)
            n_0 = jnp.sum(jnp.where(lane_k < C_QK_DIM, sq, 0.0), axis=1, keepdims=True)
            return jnp.maximum(kn2, jnp.max(jnp.maximum(n_0, n_all - n_0)))
        kn2_ref[0] = lax.fori_loop(0, nkv, tbody, jnp.float32(0.0))

    qt = q_ref[...].astype(f32).T
    row = lax.broadcasted_iota(jnp.int32, qt.shape, 0)
    qzt_ref[...] = jnp.concatenate([jnp.where(row < C_QK_DIM, qt, 0.0), jnp.where(row >= C_QK_DIM, qt, 0.0)],
                                   axis=1).astype(bf16)
    m_ref[...] = jnp.full(m_ref.shape, NEG, f32)
    acc_ref[...] = jnp.zeros(acc_ref.shape, f32)

    def chunk_at(t):
        j = jlo + t - 1
        return jnp.where(t == 0, jd, j + (j >= jd).astype(jnp.int32))

    def score_stage(t, s_ref, mx_ref):
        j = chunk_at(t)
        dc = (j * tk + half - i * tq).astype(f32)
        coef = jnp.where(j > jd, -slope, slope)
        base = jnp.where(rowq < 3, coef, -coef * (a_q - dc))
        p1 = base.astype(bf16)
        r1 = base - p1.astype(f32)
        p2 = r1.astype(bf16)
        p3 = (r1 - p2.astype(f32)).astype(bf16)
        piece = rowq % 3
        aug_q = jnp.where(rowq < 6, jnp.where(piece == 0, p1, jnp.where(piece == 1, p2, p3)),
                          jnp.zeros_like(p1))
        kc = k_ref[pl.ds(pl.multiple_of(j * tk, tk), tk), :]
        lhs = jnp.concatenate([kc, aug_k], axis=1)
        rhs = jnp.concatenate([qzt_ref[...], aug_q, zpad], axis=0)
        st = jnp.dot(lhs, rhs, preferred_element_type=f32)
        s_ref[...] = st
        mx_ref[...] = jnp.max(st, axis=0, keepdims=True)

    def softmax_stage(t, s_ref, mx_ref):
        m_old = m_ref[...]
        m_new = jnp.maximum(m_old, mx_ref[...])
        alpha = jnp.exp2(m_old - m_new)
        p = jnp.exp2(s_ref[...] - m_new).astype(bf16)
        acc_ref[...] = alpha * acc_ref[...] + jnp.dot(vt_ref[chunk_at(t)], p, preferred_element_type=f32)
        m_ref[...] = m_new

    colk = lax.broadcasted_iota(jnp.int32, (tk, LANE), 1)
    bk = (lax.broadcasted_iota(jnp.int32, (tk, LANE), 0) - half).astype(f32)
    aug_k = jnp.where(colk < 3, bk, jnp.where(colk < 6, 1.0, 0.0)).astype(bf16)
    rowq = lax.broadcasted_iota(jnp.int32, (C_AUG, 2 * tq), 0)
    a_q = (lax.broadcasted_iota(jnp.int32, (C_AUG, 2 * tq), 1) % tq).astype(f32)
    zpad = jnp.zeros((2 * LANE - 2 * C_QK_DIM - C_AUG, 2 * tq), bf16)

    jd = (i * tq) // tk
    kd = k_ref[pl.ds(pl.multiple_of(jd * tk, tk), tk), :]
    sd = jnp.dot(kd, qzt_ref[...], preferred_element_type=f32)
    rel = (lax.broadcasted_iota(jnp.int32, (tk, tq), 0) - lax.broadcasted_iota(jnp.int32, (tk, tq), 1)
           + (jd * tk - i * tq))
    bias = slope * jnp.abs(rel).astype(f32)
    sd = sd - jnp.concatenate([bias, bias], axis=1)
    s0_ref[...] = sd
    mxd = jnp.max(sd, axis=0, keepdims=True)
    mx0_ref[...] = mxd

    sqq = qt * qt
    n_all = jnp.sum(sqq, axis=0, keepdims=True)
    n_0 = jnp.sum(jnp.where(row < C_QK_DIM, sqq, 0.0), axis=0, keepdims=True)
    s_max = jnp.max(jnp.sqrt(jnp.maximum(n_0, n_all - n_0) * kn2_ref[0]))
    reach = (s_max + C_UNDERFLOW - jnp.min(mxd)) * inv_ref[h]
    wnd = jnp.minimum(reach, float(nkv)).astype(jnp.int32) + 1
    jlo = jnp.maximum(jd - wnd, 0)
    jhi = jnp.minimum(jd + wnd, nkv - 1)
    nvis = jhi - jlo + 1

    def pair(u):
        score_stage(2 * u + 1, s1_ref, mx1_ref)
        softmax_stage(2 * u, s0_ref, mx0_ref)
        score_stage(2 * u + 2, s0_ref, mx0_ref)
        softmax_stage(2 * u + 1, s1_ref, mx1_ref)

    def body(w, carry):
        pair(2 * w)
        pair(2 * w + 1)
        return carry

    npair = (nvis - 1) // 2
    lax.fori_loop(0, npair // 2, body, 0)

    @pl.when(npair % 2 == 1)
    def _():
        pair(npair - 1)

    @pl.when(nvis % 2 == 0)
    def _():
        score_stage(nvis - 1, s1_ref, mx1_ref)
        softmax_stage(nvis - 2, s0_ref, mx0_ref)
        softmax_stage(nvis - 1, s1_ref, mx1_ref)

    @pl.when(nvis % 2 == 1)
    def _():
        softmax_stage(nvis - 1, s0_ref, mx0_ref)

    lam = (jnp.exp(jnp.sum(lq1_ref[...] * lk1_ref[...], axis=-1, keepdims=True))
           - jnp.exp(jnp.sum(lq2_ref[...] * lk2_ref[...], axis=-1, keepdims=True)) + lam_init)
    o = acc_ref[:C_V_DIM, :] / acc_ref[C_V_DIM:C_V_DIM + 1, :]
    o = o[:, :tq] - lam * o[:, tq:]
    y = o * lax.rsqrt(jnp.mean(o * o, axis=0, keepdims=True) + EPS)
    o_ref[...] = (y * g_ref[...] * (1.0 - lam_init)).T.astype(o_ref.dtype)


def _attn_c(qkv, lq1, lk1, lq2, lk2, norm_g, lam_init):
    s, nc = qkv.shape
    tq = _tile(s, C_TQ)
    tk = _tile(s, C_TK)
    nkv = s // tk
    assert tk % tq == 0, (s, tq, tk)
    cq0 = B_COLS // HEAD_DIM
    ck0 = cq0 + C_HEADS
    cv0 = ck0 + C_HEADS
    smem = pl.BlockSpec(memory_space=pltpu.SMEM)
    vec = lambda n: pl.BlockSpec((1, n), lambda h, i: (0, 0))
    slopes_l2 = [LOG2E * sl for sl in _alibi_slopes(C_HEADS)]
    slopes = jnp.asarray(slopes_l2, jnp.float32)
    inv_reach = jnp.asarray([1.0 / (sl * tk) for sl in slopes_l2], jnp.float32)
    return pl.pallas_call(
        functools.partial(_attn_c_kernel, tq=tq, tk=tk, nkv=nkv, lam_init=lam_init),
        out_shape=jax.ShapeDtypeStruct((s, C_OUT), jnp.bfloat16),
        grid=(C_HEADS, s // tq),
        in_specs=[smem, smem,
                  pl.BlockSpec((tq, HEAD_DIM), lambda h, i: (i, cq0 + h)),
                  pl.BlockSpec((s, HEAD_DIM), lambda h, i: (0, ck0 + h)),
                  pl.BlockSpec((s, HEAD_DIM), lambda h, i: (0, cv0 + h)),
                  vec(C_QK_DIM), vec(C_QK_DIM), vec(C_QK_DIM), vec(C_QK_DIM),
                  pl.BlockSpec((C_V_DIM, 1), lambda h, i: (0, 0))],
        out_specs=pl.BlockSpec((tq, C_V_DIM), lambda h, i: (i, h)),
        scratch_shapes=[pltpu.VMEM((nkv, C_V_DIM + C_SUMROWS, tk), jnp.bfloat16),
                        pltpu.VMEM((2 * C_QK_DIM, 2 * tq), jnp.bfloat16),
                        pltpu.VMEM((1, 2 * tq), jnp.float32),
                        pltpu.VMEM((C_V_DIM + C_SUMROWS, 2 * tq), jnp.float32),
                        pltpu.VMEM((tk, 2 * tq), jnp.float32),
                        pltpu.VMEM((tk, 2 * tq), jnp.float32),
                        pltpu.VMEM((1, 2 * tq), jnp.float32),
                        pltpu.VMEM((1, 2 * tq), jnp.float32),
                        pltpu.SMEM((1,), jnp.float32)],
        compiler_params=_params(("parallel", "arbitrary")),
        name="attn_c",
    )(slopes, inv_reach, qkv, qkv, qkv, lq1.reshape(1, -1), lk1.reshape(1, -1), lq2.reshape(1, -1),
      lk2.reshape(1, -1), norm_g.reshape(-1, 1))


def kernel(x, mix_norm_g, w_in, b_sink, diff_lq1, diff_lk1, diff_lq2, diff_lk2, diff_norm_g,
           w_branch_a, w_branch_b, w_branch_c, w_out, mlp_norm_g, w_up, w_down, final_norm_g):
    bsz, seq, d = x.shape
    depth = w_in.shape[0]
    bf16 = jnp.bfloat16
    w_in_b, w_a_b, w_b_b, w_c_b = (w.astype(bf16) for w in (w_in, w_branch_a, w_branch_b, w_branch_c))
    w_out_b, w_up_b, w_down_b = (w.astype(bf16) for w in (w_out, w_up, w_down))
    col = jnp.arange(B_COLS + C_COLS)
    bc_scale = jnp.where(col < B_HEADS * HEAD_DIM, HEAD_DIM ** -0.5 * LOG2E,
                         jnp.where((col >= B_COLS) & (col < B_COLS + C_HEADS * 2 * C_QK_DIM),
                                   C_QK_DIM ** -0.5 * LOG2E, 1.0)).astype(jnp.float32).reshape(1, -1)
    dils = tuple(dil for _, dil in A_PATTERNS)
    pat_cols = 3 * A_OUT
    a_scale = jnp.where(jnp.arange(pat_cols) < A_OUT, HEAD_DIM ** -0.5 * LOG2E,
                        1.0).astype(jnp.float32).reshape(1, -1)
    xs = x.reshape(bsz * seq, d)
    outs = []
    for b in range(bsz):
        h = xs[b * seq:(b + 1) * seq]
        for l in range(depth):
            u, *u_dil = _rmsnorm_mix(h, mix_norm_g[l], tuple(dl for dl in dils if dl > 1))
            u_by_dil = {1: u, **{dl: ud.reshape(seq, d) for dl, ud in zip([dl for dl in dils if dl > 1], u_dil)}}
            pas = [_matmul(u_by_dil[dl], w_in_b, l, g * pat_cols, pat_cols, bf16, col_scale=a_scale,
                           name=f"proj_a{g}").reshape(dl, seq // dl, pat_cols)
                   for g, dl in enumerate(dils)]
            pbc = _matmul(u, w_in_b, l, OFF_B, B_COLS + C_COLS, bf16, col_scale=bc_scale, name="proj_bc")
            gates = _matmul(u, w_in_b, l, OFF_G, N_BRANCH * d, bf16, epilogue="sigmoid", name="proj_gates")
            o_a = _attn_a(pas)
            o_b = _attn_b(pbc, b_sink[l])
            lam_init = 0.8 - 0.6 * math.exp(-0.3 * l)
            o_c = _attn_c(pbc, diff_lq1[l], diff_lk1[l], diff_lq2[l], diff_lk2[l], diff_norm_g[l], lam_init)
            merged = _merge(o_a, o_b, o_c, gates, w_a_b, w_b_b, w_c_b, l)
            h = _matmul(merged, w_out_b, l, 0, d, jnp.float32, residual=h, name="out_proj")
            h = _mlp(h, mlp_norm_g[l], w_up_b, w_down_b, l, final_g=final_norm_g if l == depth - 1 else None)
        outs.append(h.astype(x.dtype))
    return jnp.concatenate(outs, axis=0).reshape(bsz, seq, d)
```

```python
import functools
import math

import jax
import jax.numpy as jnp
from jax import lax
from jax.experimental import pallas as pl
from jax.experimental.pallas import tpu as pltpu

HEAD_DIM = 128
A_PATTERNS = ((128, 1), (512, 4), (2048, 16))
A_HEADS = 8
N_PAT = len(A_PATTERNS)
B_HEADS = 12
B_KV_HEADS = 4
B_GROUP = B_HEADS // B_KV_HEADS
B_HALF = 128
C_HEADS = 12
C_QK_DIM = 64
C_V_DIM = 2 * C_QK_DIM
N_BRANCH = 3
EPS = 1e-6
NEG = -1e30

A_COLS = N_PAT * 3 * A_HEADS * HEAD_DIM
B_COLS = (B_HEADS + 2 * B_KV_HEADS) * HEAD_DIM
C_COLS = C_HEADS * (4 * C_QK_DIM + C_V_DIM)
OFF_B = A_COLS
OFF_C = OFF_B + B_COLS
OFF_G = OFF_C + C_COLS
A_OUT = A_HEADS * HEAD_DIM
B_OUT = B_HEADS * HEAD_DIM
C_OUT = C_HEADS * C_V_DIM

LANE = 128
VMEM_LIMIT = 56 * 1024 * 1024


def _tile(n, pref):
    t = pref
    while t > 1 and n % t:
        t //= 2
    return t


def _params(sem):
    return pltpu.CompilerParams(dimension_semantics=sem, vmem_limit_bytes=VMEM_LIMIT)


def _alibi_slopes(n):
    return [2.0 ** (-8.0 * i / n) for i in range(1, n + 1)]


def _rmsnorm_mix_kernel(x_ref, g_ref, o_ref, *grouped_refs, dils, tr):
    x = x_ref[...]
    y = (x * lax.rsqrt(jnp.mean(x * x, axis=-1, keepdims=True) + EPS) * g_ref[...]).astype(o_ref.dtype)
    o_ref[...] = y
    dst = lax.broadcasted_iota(jnp.int32, (tr, tr), 0)
    src = lax.broadcasted_iota(jnp.int32, (tr, tr), 1)
    for dil, s_ref in zip(dils, grouped_refs):
        n = tr // dil
        perm = (src == (dst % n) * dil + dst // n).astype(y.dtype)
        yp = jnp.dot(perm, y, preferred_element_type=jnp.float32).astype(y.dtype)
        for r in range(dil):
            s_ref[r] = yp[r * n:(r + 1) * n]


def _rmsnorm_mix(x, g, dils):
    s, d = x.shape
    tr = _tile(s, 256)
    out_shape = [jax.ShapeDtypeStruct((s, d), jnp.bfloat16)]
    out_specs = [pl.BlockSpec((tr, d), lambda i: (i, 0))]
    for dil in dils:
        out_shape.append(jax.ShapeDtypeStruct((dil, s // dil, d), jnp.bfloat16))
        out_specs.append(pl.BlockSpec((dil, tr // dil, d), lambda i: (0, i, 0)))
    return pl.pallas_call(
        functools.partial(_rmsnorm_mix_kernel, dils=dils, tr=tr),
        out_shape=out_shape,
        grid=(s // tr,),
        in_specs=[pl.BlockSpec((tr, d), lambda i: (i, 0)),
                  pl.BlockSpec((1, d), lambda i: (0, 0))],
        out_specs=out_specs,
        compiler_params=_params(("parallel",)),
        name="rmsnorm_mix",
    )(x, g.reshape(1, d))


MATMUL_TN = 1024


def _matmul_kernel(x_ref, w_ref, *rest, epilogue):
    *extra, o_ref = rest
    acc = jnp.dot(x_ref[...], w_ref[...], preferred_element_type=jnp.float32)
    if epilogue == "sigmoid":
        acc = jax.nn.sigmoid(acc)
    elif epilogue == "col_scale":
        acc = acc * extra[0][...]
    elif epilogue == "residual":
        acc = extra[0][...] + acc
    o_ref[...] = acc.astype(o_ref.dtype)


def _tile_major(w, tn):
    depth, k, n = w.shape
    return w.reshape(depth, k, n // tn, tn).transpose(0, 2, 1, 3)


def _matmul(x, wt, layer, col_off, ncols, out_dtype, epilogue="none", residual=None, col_scale=None,
            name="matmul"):
    m, k = x.shape
    tn = wt.shape[-1]
    assert col_off % tn == 0 and ncols % tn == 0, (col_off, ncols, tn)
    tm = _tile(m, 1024)
    off = col_off // tn
    in_specs = [pl.BlockSpec((tm, k), lambda i, j: (i, 0)),
                pl.BlockSpec((None, None, k, tn), lambda i, j: (layer, off + j, 0, 0))]
    args = [x, wt]
    if col_scale is not None:
        epilogue = "col_scale"
        in_specs.append(pl.BlockSpec((1, tn), lambda i, j: (0, j)))
        args.append(col_scale)
    elif residual is not None:
        epilogue = "residual"
        in_specs.append(pl.BlockSpec((tm, tn), lambda i, j: (i, j)))
        args.append(residual)
    return pl.pallas_call(
        functools.partial(_matmul_kernel, epilogue=epilogue),
        out_shape=jax.ShapeDtypeStruct((m, ncols), out_dtype),
        grid=(m // tm, ncols // tn),
        in_specs=in_specs,
        out_specs=pl.BlockSpec((tm, tn), lambda i, j: (i, j)),
        compiler_params=_params(("parallel", "parallel")),
        name=name,
    )(*args)


def _merge_kernel(oa_ref, ob_ref, oc_ref, wa_ref, wb_ref, wc_ref, ga_ref, gb_ref, gc_ref, o_ref):
    f32 = jnp.float32
    ya = jnp.dot(oa_ref[...], wa_ref[...], preferred_element_type=f32)
    yb = jnp.dot(ob_ref[...], wb_ref[...], preferred_element_type=f32)
    yc = jnp.dot(oc_ref[...], wc_ref[...], preferred_element_type=f32)
    merged = (ga_ref[...].astype(f32) * ya + gb_ref[...].astype(f32) * yb
              + gc_ref[...].astype(f32) * yc)
    o_ref[...] = merged.astype(o_ref.dtype)


MERGE_TN = 512


def _merge(o_a, o_b, o_c, gates, w_a, w_b, w_c, layer):
    s = o_a.shape[0]
    nj, tn = w_a.shape[1], w_a.shape[-1]
    d = nj * tn
    tm = _tile(s, 1024)
    row = lambda width: pl.BlockSpec((tm, width), lambda i, j: (i, 0))
    wspec = lambda kk: pl.BlockSpec((None, None, kk, tn), lambda i, j: (layer, j, 0, 0))
    gspec = lambda b: pl.BlockSpec((tm, tn), lambda i, j: (i, b * nj + j))
    return pl.pallas_call(
        _merge_kernel,
        out_shape=jax.ShapeDtypeStruct((s, d), jnp.bfloat16),
        grid=(s // tm, nj),
        in_specs=[row(A_OUT), row(B_OUT), row(C_OUT),
                  wspec(A_OUT), wspec(B_OUT), wspec(C_OUT),
                  gspec(0), gspec(1), gspec(2)],
        out_specs=pl.BlockSpec((tm, tn), lambda i, j: (i, j)),
        compiler_params=_params(("parallel", "parallel")),
        name="merge",
    )(o_a, o_b, o_c, w_a, w_b, w_c, gates, gates, gates)


NORM_ROWS = 64
MLP_TF = 512


def _mlp_kernel(h_ref, gin_ref, wu_ref, wd_ref, *rest, final_norm):
    *gout_ref, o_ref, v_ref = rest
    f = pl.program_id(1)

    def norm_rows(src_ref, dst_ref, g_ref):
        def step(c, carry):
            rows = pl.ds(pl.multiple_of(c * NORM_ROWS, NORM_ROWS), NORM_ROWS)
            x = src_ref[rows, :]
            y = x * lax.rsqrt(jnp.mean(x * x, axis=-1, keepdims=True) + EPS) * g_ref[...]
            dst_ref[rows, :] = y.astype(dst_ref.dtype)
            return carry
        lax.fori_loop(0, src_ref.shape[0] // NORM_ROWS, step, 0)

    @pl.when(f == 0)
    def _():
        o_ref[...] = h_ref[...]
        norm_rows(h_ref, v_ref, gin_ref)

    a = jnp.dot(v_ref[...], wu_ref[...], preferred_element_type=jnp.float32)
    a = jnp.square(jnp.maximum(a, 0.0)).astype(jnp.bfloat16)
    o_ref[...] += jnp.dot(a, wd_ref[...], preferred_element_type=jnp.float32)

    if final_norm:
        @pl.when(f == pl.num_programs(1) - 1)
        def _():
            norm_rows(o_ref, o_ref, gout_ref[0])


def _mlp(h, g_in, w_up_t, w_down, layer, final_g=None):
    s, d = h.shape
    tf = w_up_t.shape[-1]
    dff = w_up_t.shape[1] * tf
    tm = _tile(s, 512)
    vec = pl.BlockSpec((1, d), lambda i, f: (0, 0))
    in_specs = [pl.BlockSpec((tm, d), lambda i, f: (i, 0), pipeline_mode=pl.Buffered(1)),
                vec,
                pl.BlockSpec((None, None, d, tf), lambda i, f: (layer, f, 0, 0)),
                pl.BlockSpec((None, tf, d), lambda i, f: (layer, f, 0))]
    args = [h, g_in.reshape(1, d), w_up_t, w_down]
    if final_g is not None:
        in_specs.append(vec)
        args.append(final_g.reshape(1, d))
    return pl.pallas_call(
        functools.partial(_mlp_kernel, final_norm=final_g is not None),
        out_shape=jax.ShapeDtypeStruct((s, d), jnp.float32),
        grid=(s // tm, dff // tf),
        in_specs=in_specs,
        out_specs=pl.BlockSpec((tm, d), lambda i, f: (i, 0)),
        scratch_shapes=[pltpu.VMEM((tm, d), jnp.bfloat16)],
        compiler_params=_params(("parallel", "arbitrary")),
        name="mlp",
    )(*args)


A_TQ = 256
A_SUB = 128
A_BLK = 64
LOG2E = 1.4426950408889634
FAR = 3e32


def _halo_window(p_ref, m_ref, n_ref, cols, r0, nk, halo, tq):
    lo, hi = r0, r0 + nk
    pieces = []
    if lo < halo:
        pieces.append(p_ref[lo:min(hi, halo), cols])
    a, b = max(lo, halo), min(hi, halo + tq)
    if a < b:
        pieces.append(m_ref[a - halo:b - halo, cols])
    a = max(lo, halo + tq)
    if a < hi:
        pieces.append(n_ref[a - halo - tq:hi - halo - tq, cols])
    return pieces[0] if len(pieces) == 1 else jnp.concatenate(pieces, axis=0)


def _band_distance(i, last, sub, nsub_rows, halo, tq):
    nk = nsub_rows + 2 * halo
    a = lax.broadcasted_iota(jnp.int32, (nsub_rows, nk), 0)
    c = lax.broadcasted_iota(jnp.int32, (nsub_rows, nk), 1)
    dist = jnp.abs(c - halo - a)
    row = c + sub * nsub_rows
    valid = (dist <= halo) & ((row >= halo) | (i > 0)) & ((row < tq + halo) | (i < last))
    return jnp.where(valid, dist.astype(jnp.float32), FAR)


def _attn_a_kernel(q_ref, kp_ref, km_ref, kn_ref, vp_ref, vm_ref, vn_ref, o_ref, lse_ref, *, dil, tq):
    i = pl.program_id(1)
    last = pl.num_programs(1) - 1
    nsub = tq // A_SUB if tq >= A_SUB else 1
    rows = tq // nsub
    nk = rows + 2 * A_BLK
    lane = lax.broadcasted_iota(jnp.int32, (rows, LANE), 1)
    slopes = [LOG2E * dil * sl for sl in _alibi_slopes(A_HEADS)]
    probs = [(sub, h) for sub in range(nsub) for h in range(A_HEADS)]
    cols = lambda h: slice(h * HEAD_DIM, (h + 1) * HEAD_DIM)
    dists = [_band_distance(i, last, sub, rows, A_BLK, tq) for sub in range(nsub)]
    scores = []
    for sub, h in probs:
        k = _halo_window(kp_ref, km_ref, kn_ref, cols(h), sub * rows, nk, A_BLK, tq)
        s = lax.dot_general(q_ref[sub * rows:(sub + 1) * rows, cols(h)], k, (((1,), (1,)), ((), ())),
                            preferred_element_type=jnp.float32)
        scores.append(s - slopes[h] * dists[sub])
    es, denoms = [], []
    lse_tiles = [jnp.zeros((rows, LANE), jnp.float32) for _ in range(nsub)]
    for (sub, h), s in zip(probs, scores):
        m = jnp.max(s, axis=-1, keepdims=True)
        e = jnp.exp2(s - m)
        denom = jnp.sum(e, axis=-1, keepdims=True)
        es.append(e.astype(jnp.bfloat16))
        denoms.append(denom)
        lse_tiles[sub] = jnp.where(lane == h, m + jnp.log2(denom), lse_tiles[sub])
    outs = []
    for (sub, h), e, denom in zip(probs, es, denoms):
        v = _halo_window(vp_ref, vm_ref, vn_ref, cols(h), sub * rows, nk, A_BLK, tq)
        outs.append(jnp.dot(e, v, preferred_element_type=jnp.float32) / denom)
    o_rows = [jnp.concatenate(outs[sub * A_HEADS:(sub + 1) * A_HEADS], axis=1) for sub in range(nsub)]
    o_ref[...] = o_rows[0] if nsub == 1 else jnp.concatenate(o_rows, axis=0)
    lse_ref[...] = lse_tiles[0] if nsub == 1 else jnp.concatenate(lse_tiles, axis=0)


def _attn_a_pattern(pa, g):
    dil, ls, _ = pa.shape
    tq = _tile(ls, A_TQ)
    r64 = tq // A_BLK
    nblk64 = ls // A_BLK

    def main(c):
        return pl.BlockSpec((None, tq, A_OUT), lambda r, i: (r, i, c))

    def prev(c):
        return pl.BlockSpec((None, A_BLK, A_OUT), lambda r, i: (r, jnp.maximum(i * r64 - 1, 0), c))

    def nxt(c):
        return pl.BlockSpec((None, A_BLK, A_OUT),
                            lambda r, i: (r, jnp.minimum((i + 1) * r64, nblk64 - 1), c))

    return pl.pallas_call(
        functools.partial(_attn_a_kernel, dil=dil, tq=tq),
        out_shape=(jax.ShapeDtypeStruct((dil, ls, A_OUT), jnp.float32),
                   jax.ShapeDtypeStruct((dil, ls, LANE), jnp.float32)),
        grid=(dil, ls // tq),
        in_specs=[main(0), prev(1), main(1), nxt(1), prev(2), main(2), nxt(2)],
        out_specs=(pl.BlockSpec((None, tq, A_OUT), lambda r, i: (r, i, 0)),
                   pl.BlockSpec((None, tq, LANE), lambda r, i: (r, i, 0))),
        compiler_params=_params(("parallel", "parallel")),
        name=f"attn_a{g}",
    )(pa, pa, pa, pa, pa, pa, pa)


def _combine_a_kernel(o0_ref, o1_ref, o2_ref, l0_ref, l1_ref, l2_ref, o_ref, *scratch, dils, tr):
    o_in, l_in = [o0_ref, o1_ref, o2_ref], [l0_ref, l1_ref, l2_ref]
    nat = [scratch[g] if dil > 1 else None for g, dil in enumerate(dils)]

    def ungroup(g, src):
        dil = dils[g]
        if dil == 1:
            return src(0)
        for r in range(dil):
            nat[g][pl.ds(r, tr // dil, stride=dil), :] = src(r)
        return nat[g][...]

    l0, l1, l2 = (ungroup(g, lambda r, g=g: l_in[g][r]) for g in range(len(dils)))
    mx = jnp.maximum(jnp.maximum(l0, l1), l2)
    e0, e1, e2 = jnp.exp2(l0 - mx), jnp.exp2(l1 - mx), jnp.exp2(l2 - mx)
    inv = 1.0 / (e0 + e1 + e2)
    w = [e0 * inv, e1 * inv, e2 * inv]
    for h in range(A_HEADS):
        sl = slice(h * HEAD_DIM, (h + 1) * HEAD_DIM)
        o = sum(w[g][:, h:h + 1] * ungroup(g, lambda r, g=g: o_in[g][r, :, sl]) for g in range(len(dils)))
        o_ref[:, sl] = o.astype(o_ref.dtype)


def _attn_a(pas):
    dils = tuple(p.shape[0] for p in pas)
    s = pas[0].shape[0] * pas[0].shape[1]
    outs = [_attn_a_pattern(p, g) for g, p in enumerate(pas)]
    tr = _tile(s, 512)
    ospec = lambda dil: pl.BlockSpec((dil, tr // dil, A_OUT), lambda i: (0, i, 0))
    lspec = lambda dil: pl.BlockSpec((dil, tr // dil, LANE), lambda i: (0, i, 0))
    scratch = [pltpu.VMEM((tr, LANE), jnp.float32) for _ in dils]
    return pl.pallas_call(
        functools.partial(_combine_a_kernel, dils=dils, tr=tr),
        out_shape=jax.ShapeDtypeStruct((s, A_OUT), jnp.bfloat16),
        grid=(s // tr,),
        in_specs=[ospec(d) for d in dils] + [lspec(d) for d in dils],
        out_specs=pl.BlockSpec((tr, A_OUT), lambda i: (i, 0)),
        scratch_shapes=scratch,
        compiler_params=_params(("parallel",)),
        name="combine_a",
    )(*[o for o, _ in outs], *[l for _, l in outs])


B_TQ = 256


def _attn_b_kernel(slope_ref, sink_ref, q_ref, kp_ref, km_ref, kn_ref, vp_ref, vm_ref, vn_ref, o_ref, *, tq):
    c = pl.program_id(0)
    i = pl.program_id(1)
    last = pl.num_programs(1) - 1
    nsub = tq // B_HALF
    nk = 3 * B_HALF
    allc = slice(None)
    probs = [(sub, g) for sub in range(nsub) for g in range(B_GROUP)]
    cols = lambda g: slice(g * HEAD_DIM, (g + 1) * HEAD_DIM)
    slopes = [slope_ref[c * B_GROUP + g] for g in range(B_GROUP)]
    sinks = [sink_ref[c * B_GROUP + g] * LOG2E for g in range(B_GROUP)]
    dists = [_band_distance(i, last, sub, B_HALF, B_HALF, tq) for sub in range(nsub)]
    ks = [_halo_window(kp_ref, km_ref, kn_ref, allc, sub * B_HALF, nk, B_HALF, tq) for sub in range(nsub)]
    scores = []
    for sub, g in probs:
        s = lax.dot_general(q_ref[sub * B_HALF:(sub + 1) * B_HALF, cols(g)], ks[sub],
                            (((1,), (1,)), ((), ())), preferred_element_type=jnp.float32)
        scores.append(s - slopes[g] * dists[sub])
    es, denoms = [], []
    for (sub, g), s in zip(probs, scores):
        m = jnp.maximum(jnp.max(s, axis=-1, keepdims=True), sinks[g])
        e = jnp.exp2(s - m)
        es.append(e.astype(jnp.bfloat16))
        denoms.append(jnp.sum(e, axis=-1, keepdims=True) + jnp.exp2(sinks[g] - m))
    vs = [_halo_window(vp_ref, vm_ref, vn_ref, allc, sub * B_HALF, nk, B_HALF, tq) for sub in range(nsub)]
    outs = [(jnp.dot(e, vs[sub], preferred_element_type=jnp.float32) / denom).astype(o_ref.dtype)
            for (sub, g), e, denom in zip(probs, es, denoms)]
    o_rows = [jnp.concatenate(outs[sub * B_GROUP:(sub + 1) * B_GROUP], axis=1) for sub in range(nsub)]
    o_ref[...] = o_rows[0] if nsub == 1 else jnp.concatenate(o_rows, axis=0)


def _attn_b(qkv, sinks):
    s, nc = qkv.shape
    tq = _tile(s, B_TQ)
    r128 = tq // B_HALF
    nblk = s // B_HALF
    gw = B_GROUP * HEAD_DIM
    cq0 = 0
    ck0 = B_HEADS
    cv0 = ck0 + B_KV_HEADS
    smem = pl.BlockSpec(memory_space=pltpu.SMEM)

    def main(c0):
        return pl.BlockSpec((tq, HEAD_DIM), lambda c, i: (i, c0 + c))

    def prev(c0):
        return pl.BlockSpec((B_HALF, HEAD_DIM), lambda c, i: (jnp.maximum(i * r128 - 1, 0), c0 + c))

    def nxt(c0):
        return pl.BlockSpec((B_HALF, HEAD_DIM),
                            lambda c, i: (jnp.minimum((i + 1) * r128, nblk - 1), c0 + c))

    slopes = jnp.asarray([LOG2E * sl for sl in _alibi_slopes(B_HEADS)], jnp.float32)
    return pl.pallas_call(
        functools.partial(_attn_b_kernel, tq=tq),
        out_shape=jax.ShapeDtypeStruct((s, B_OUT), jnp.bfloat16),
        grid=(B_KV_HEADS, s // tq),
        in_specs=[smem, smem,
                  pl.BlockSpec((tq, gw), lambda c, i: (i, cq0 + c)),
                  prev(ck0), main(ck0), nxt(ck0), prev(cv0), main(cv0), nxt(cv0)],
        out_specs=pl.BlockSpec((tq, gw), lambda c, i: (i, c)),
        compiler_params=_params(("parallel", "parallel")),
        name="attn_b",
    )(slopes, sinks.astype(jnp.float32), qkv, qkv, qkv, qkv, qkv, qkv, qkv)


C_TQ = 512
C_TK = 512
C_AUG = 16
C_SUMROWS = 16
C_UNDERFLOW = 170.0


def _attn_c_kernel(slope_ref, inv_ref, q_ref, k_ref, v_ref, lq1_ref, lk1_ref, lq2_ref, lk2_ref, g_ref, o_ref,
                   vt_ref, qzt_ref, m_ref, acc_ref, s0_ref, s1_ref, mx0_ref, mx1_ref, kn2_ref,
                   *, tq, tk, nkv, lam_init):
    f32, bf16 = jnp.float32, jnp.bfloat16
    h = pl.program_id(0)
    i = pl.program_id(1)
    slope = slope_ref[h]
    half = tk // 2
    lane_k = lax.broadcasted_iota(jnp.int32, (tk, 2 * C_QK_DIM), 1)

    @pl.when(i == 0)
    def _():
        def tbody(c, kn2):
            st = pl.multiple_of(c * tk, tk)
            vt = v_ref[pl.ds(st, tk), :].astype(f32).T.astype(bf16)
            ones_row = lax.broadcasted_iota(jnp.int32, (C_SUMROWS, tk), 0) == 0
            vt_ref[c] = jnp.concatenate([vt, ones_row.astype(bf16)], axis=0)
            kk = k_ref[pl.ds(st, tk), :].astype(f32)
            sq = kk * kk
            n_all = jnp.sum(sq, axis=1, keepdims=True)
            n_0 = jnp.sum(jnp.where(lane_k < C_QK_DIM, sq, 0.0), axis=1, keepdims=True)
            return jnp.maximum(kn2, jnp.max(jnp.maximum(n_0, n_all - n_0)))
        kn2_ref[0] = lax.fori_loop(0, nkv, tbody, jnp.float32(0.0))

    qt = q_ref[...].astype(f32).T
    row = lax.broadcasted_iota(jnp.int32, qt.shape, 0)
    qzt_ref[...] = jnp.concatenate([jnp.where(row < C_QK_DIM, qt, 0.0), jnp.where(row >= C_QK_DIM, qt, 0.0)],
                                   axis=1).astype(bf16)
    m_ref[...] = jnp.full(m_ref.shape, NEG, f32)
    acc_ref[...] = jnp.zeros(acc_ref.shape, f32)

    def chunk_at(t):
        j = jlo + t - 1
        return jnp.where(t == 0, jd, j + (j >= jd).astype(jnp.int32))

    def score_stage(t, s_ref, mx_ref):
        j = chunk_at(t)
        dc = (j * tk + half - i * tq).astype(f32)
        coef = jnp.where(j > jd, -slope, slope)
        base = jnp.where(rowq < 3, coef, -coef * (a_q - dc))
        p1 = base.astype(bf16)
        r1 = base - p1.astype(f32)
        p2 = r1.astype(bf16)
        p3 = (r1 - p2.astype(f32)).astype(bf16)
        piece = rowq % 3
        aug_q = jnp.where(rowq < 6, jnp.where(piece == 0, p1, jnp.where(piece == 1, p2, p3)),
                          jnp.zeros_like(p1))
        kc = k_ref[pl.ds(pl.multiple_of(j * tk, tk), tk), :]
        lhs = jnp.concatenate([kc, aug_k], axis=1)
        rhs = jnp.concatenate([qzt_ref[...], aug_q, zpad], axis=0)
        st = jnp.dot(lhs, rhs, preferred_element_type=f32)
        s_ref[...] = st
        mx_ref[...] = jnp.max(st, axis=0, keepdims=True)

    def softmax_stage(t, s_ref, mx_ref):
        m_old = m_ref[...]
        m_new = jnp.maximum(m_old, mx_ref[...])
        alpha = jnp.exp2(m_old - m_new)
        p = jnp.exp2(s_ref[...] - m_new).astype(bf16)
        acc_ref[...] = alpha * acc_ref[...] + jnp.dot(vt_ref[chunk_at(t)], p, preferred_element_type=f32)
        m_ref[...] = m_new

    colk = lax.broadcasted_iota(jnp.int32, (tk, LANE), 1)
    bk = (lax.broadcasted_iota(jnp.int32, (tk, LANE), 0) - half).astype(f32)
    aug_k = jnp.where(colk < 3, bk, jnp.where(colk < 6, 1.0, 0.0)).astype(bf16)
    rowq = lax.broadcasted_iota(jnp.int32, (C_AUG, 2 * tq), 0)
    a_q = (lax.broadcasted_iota(jnp.int32, (C_AUG, 2 * tq), 1) % tq).astype(f32)
    zpad = jnp.zeros((2 * LANE - 2 * C_QK_DIM - C_AUG, 2 * tq), bf16)

    jd = (i * tq) // tk
    kd = k_ref[pl.ds(pl.multiple_of(jd * tk, tk), tk), :]
    sd = jnp.dot(kd, qzt_ref[...], preferred_element_type=f32)
    rel = (lax.broadcasted_iota(jnp.int32, (tk, tq), 0) - lax.broadcasted_iota(jnp.int32, (tk, tq), 1)
           + (jd * tk - i * tq))
    bias = slope * jnp.abs(rel).astype(f32)
    sd = sd - jnp.concatenate([bias, bias], axis=1)
    s0_ref[...] = sd
    mxd = jnp.max(sd, axis=0, keepdims=True)
    mx0_ref[...] = mxd

    sqq = qt * qt
    n_all = jnp.sum(sqq, axis=0, keepdims=True)
    n_0 = jnp.sum(jnp.where(row < C_QK_DIM, sqq, 0.0), axis=0, keepdims=True)
    s_max = jnp.max(jnp.sqrt(jnp.maximum(n_0, n_all - n_0) * kn2_ref[0]))
    reach = (s_max + C_UNDERFLOW - jnp.min(mxd)) * inv_ref[h]
    wnd = jnp.minimum(reach, float(nkv)).astype(jnp.int32) + 1
    jlo = jnp.maximum(jd - wnd, 0)
    jhi = jnp.minimum(jd + wnd, nkv - 1)
    nvis = jhi - jlo + 1

    def pair(u):
        score_stage(2 * u + 1, s1_ref, mx1_ref)
        softmax_stage(2 * u, s0_ref, mx0_ref)
        score_stage(2 * u + 2, s0_ref, mx0_ref)
        softmax_stage(2 * u + 1, s1_ref, mx1_ref)

    def body(w, carry):
        pair(2 * w)
        pair(2 * w + 1)
        return carry

    npair = (nvis - 1) // 2
    lax.fori_loop(0, npair // 2, body, 0)

    @pl.when(npair % 2 == 1)
    def _():
        pair(npair - 1)

    @pl.when(nvis % 2 == 0)
    def _():
        score_stage(nvis - 1, s1_ref, mx1_ref)
        softmax_stage(nvis - 2, s0_ref, mx0_ref)
        softmax_stage(nvis - 1, s1_ref, mx1_ref)

    @pl.when(nvis % 2 == 1)
    def _():
        softmax_stage(nvis - 1, s0_ref, mx0_ref)

    lam = (jnp.exp(jnp.sum(lq1_ref[...] * lk1_ref[...], axis=-1, keepdims=True))
           - jnp.exp(jnp.sum(lq2_ref[...] * lk2_ref[...], axis=-1, keepdims=True)) + lam_init)
    o = acc_ref[:C_V_DIM, :] / acc_ref[C_V_DIM:C_V_DIM + 1, :]
    o = o[:, :tq] - lam * o[:, tq:]
    y = o * lax.rsqrt(jnp.mean(o * o, axis=0, keepdims=True) + EPS)
    o_ref[...] = (y * g_ref[...] * (1.0 - lam_init)).T.astype(o_ref.dtype)


def _attn_c(qkv, lq1, lk1, lq2, lk2, norm_g, lam_init):
    s, nc = qkv.shape
    tq = _tile(s, C_TQ)
    tk = _tile(s, C_TK)
    nkv = s // tk
    assert tk % tq == 0, (s, tq, tk)
    cq0 = B_COLS // HEAD_DIM
    ck0 = cq0 + C_HEADS
    cv0 = ck0 + C_HEADS
    smem = pl.BlockSpec(memory_space=pltpu.SMEM)
    vec = lambda n: pl.BlockSpec((1, n), lambda h, i: (0, 0))
    slopes_l2 = [LOG2E * sl for sl in _alibi_slopes(C_HEADS)]
    slopes = jnp.asarray(slopes_l2, jnp.float32)
    inv_reach = jnp.asarray([1.0 / (sl * tk) for sl in slopes_l2], jnp.float32)
    return pl.pallas_call(
        functools.partial(_attn_c_kernel, tq=tq, tk=tk, nkv=nkv, lam_init=lam_init),
        out_shape=jax.ShapeDtypeStruct((s, C_OUT), jnp.bfloat16),
        grid=(C_HEADS, s // tq),
        in_specs=[smem, smem,
                  pl.BlockSpec((tq, HEAD_DIM), lambda h, i: (i, cq0 + h)),
                  pl.BlockSpec((s, HEAD_DIM), lambda h, i: (0, ck0 + h)),
                  pl.BlockSpec((s, HEAD_DIM), lambda h, i: (0, cv0 + h)),
                  vec(C_QK_DIM), vec(C_QK_DIM), vec(C_QK_DIM), vec(C_QK_DIM),
                  pl.BlockSpec((C_V_DIM, 1), lambda h, i: (0, 0))],
        out_specs=pl.BlockSpec((tq, C_V_DIM), lambda h, i: (i, h)),
        scratch_shapes=[pltpu.VMEM((nkv, C_V_DIM + C_SUMROWS, tk), jnp.bfloat16),
                        pltpu.VMEM((2 * C_QK_DIM, 2 * tq), jnp.bfloat16),
                        pltpu.VMEM((1, 2 * tq), jnp.float32),
                        pltpu.VMEM((C_V_DIM + C_SUMROWS, 2 * tq), jnp.float32),
                        pltpu.VMEM((tk, 2 * tq), jnp.float32),
                        pltpu.VMEM((tk, 2 * tq), jnp.float32),
                        pltpu.VMEM((1, 2 * tq), jnp.float32),
                        pltpu.VMEM((1, 2 * tq), jnp.float32),
                        pltpu.SMEM((1,), jnp.float32)],
        compiler_params=_params(("parallel", "arbitrary")),
        name="attn_c",
    )(slopes, inv_reach, qkv, qkv, qkv, lq1.reshape(1, -1), lk1.reshape(1, -1), lq2.reshape(1, -1),
      lk2.reshape(1, -1), norm_g.reshape(-1, 1))


def kernel(x, mix_norm_g, w_in, b_sink, diff_lq1, diff_lk1, diff_lq2, diff_lk2, diff_norm_g,
           w_branch_a, w_branch_b, w_branch_c, w_out, mlp_norm_g, w_up, w_down, final_norm_g):
    bsz, seq, d = x.shape
    depth = w_in.shape[0]
    bf16 = jnp.bfloat16
    w_down_b = w_down.astype(bf16)
    w_a_b, w_b_b, w_c_b = (_tile_major(w.astype(bf16), _tile(d, MERGE_TN))
                           for w in (w_branch_a, w_branch_b, w_branch_c))
    pat_cols = 3 * A_OUT
    tn_in = _tile(math.gcd(math.gcd(pat_cols, OFF_B), math.gcd(OFF_G, N_BRANCH * d)), MATMUL_TN)
    w_in_b = _tile_major(w_in.astype(bf16), tn_in)
    w_out_b = _tile_major(w_out.astype(bf16), _tile(d, MATMUL_TN))
    w_up_b = _tile_major(w_up.astype(bf16), _tile(w_up.shape[-1], MLP_TF))
    col = jnp.arange(B_COLS + C_COLS)
    bc_scale = jnp.where(col < B_HEADS * HEAD_DIM, HEAD_DIM ** -0.5 * LOG2E,
                         jnp.where((col >= B_COLS) & (col < B_COLS + C_HEADS * 2 * C_QK_DIM),
                                   C_QK_DIM ** -0.5 * LOG2E, 1.0)).astype(jnp.float32).reshape(1, -1)
    dils = tuple(dil for _, dil in A_PATTERNS)
    a_scale = jnp.where(jnp.arange(pat_cols) < A_OUT, HEAD_DIM ** -0.5 * LOG2E,
                        1.0).astype(jnp.float32).reshape(1, -1)
    xs = x.reshape(bsz * seq, d)
    outs = []
    for b in range(bsz):
        h = xs[b * seq:(b + 1) * seq]
        for l in range(depth):
            u, *u_dil = _rmsnorm_mix(h, mix_norm_g[l], tuple(dl for dl in dils if dl > 1))
            u_by_dil = {1: u, **{dl: ud.reshape(seq, d) for dl, ud in zip([dl for dl in dils if dl > 1], u_dil)}}
            pas = [_matmul(u_by_dil[dl], w_in_b, l, g * pat_cols, pat_cols, bf16, col_scale=a_scale,
                           name=f"proj_a{g}").reshape(dl, seq // dl, pat_cols)
                   for g, dl in enumerate(dils)]
            pbc = _matmul(u, w_in_b, l, OFF_B, B_COLS + C_COLS, bf16, col_scale=bc_scale, name="proj_bc")
            gates = _matmul(u, w_in_b, l, OFF_G, N_BRANCH * d, bf16, epilogue="sigmoid", name="proj_gates")
            o_a = _attn_a(pas)
            o_b = _attn_b(pbc, b_sink[l])
            lam_init = 0.8 - 0.6 * math.exp(-0.3 * l)
            o_c = _attn_c(pbc, diff_lq1[l], diff_lk1[l], diff_lq2[l], diff_lk2[l], diff_norm_g[l], lam_init)
            merged = _merge(o_a, o_b, o_c, gates, w_a_b, w_b_b, w_c_b, l)
            h = _matmul(merged, w_out_b, l, 0, d, jnp.float32, residual=h, name="out_proj")
            h = _mlp(h, mlp_norm_g[l], w_up_b, w_down_b, l, final_g=final_norm_g if l == depth - 1 else None)
        outs.append(h.astype(x.dtype))
    return jnp.concatenate(outs, axis=0).reshape(bsz, seq, d)
```

```python
import functools
import math

import jax
import jax.numpy as jnp
from jax import lax
from jax.experimental import pallas as pl
from jax.experimental.pallas import tpu as pltpu

HEAD_DIM = 128
A_PATTERNS = ((128, 1), (512, 4), (2048, 16))
A_HEADS = 8
N_PAT = len(A_PATTERNS)
B_HEADS = 12
B_KV_HEADS = 4
B_GROUP = B_HEADS // B_KV_HEADS
B_HALF = 128
C_HEADS = 12
C_QK_DIM = 64
C_V_DIM = 2 * C_QK_DIM
N_BRANCH = 3
EPS = 1e-6
NEG = -1e30

A_COLS = N_PAT * 3 * A_HEADS * HEAD_DIM
B_COLS = (B_HEADS + 2 * B_KV_HEADS) * HEAD_DIM
C_COLS = C_HEADS * (4 * C_QK_DIM + C_V_DIM)
OFF_B = A_COLS
OFF_C = OFF_B + B_COLS
OFF_G = OFF_C + C_COLS
A_OUT = A_HEADS * HEAD_DIM
B_OUT = B_HEADS * HEAD_DIM
C_OUT = C_HEADS * C_V_DIM

LANE = 128
VMEM_LIMIT = 56 * 1024 * 1024


def _tile(n, pref):
    t = pref
    while t > 1 and n % t:
        t //= 2
    return t


def _params(sem):
    return pltpu.CompilerParams(dimension_semantics=sem, vmem_limit_bytes=VMEM_LIMIT)


def _alibi_slopes(n):
    return [2.0 ** (-8.0 * i / n) for i in range(1, n + 1)]


def _rmsnorm_mix_kernel(x_ref, g_ref, o_ref, *grouped_refs, dils, tr):
    x = x_ref[...]
    y = (x * lax.rsqrt(jnp.mean(x * x, axis=-1, keepdims=True) + EPS) * g_ref[...]).astype(o_ref.dtype)
    o_ref[...] = y
    dst = lax.broadcasted_iota(jnp.int32, (tr, tr), 0)
    src = lax.broadcasted_iota(jnp.int32, (tr, tr), 1)
    for dil, s_ref in zip(dils, grouped_refs):
        n = tr // dil
        perm = (src == (dst % n) * dil + dst // n).astype(y.dtype)
        yp = jnp.dot(perm, y, preferred_element_type=jnp.float32).astype(y.dtype)
        for r in range(dil):
            s_ref[r] = yp[r * n:(r + 1) * n]


def _rmsnorm_mix(x, g, dils):
    s, d = x.shape
    tr = _tile(s, 256)
    out_shape = [jax.ShapeDtypeStruct((s, d), jnp.bfloat16)]
    out_specs = [pl.BlockSpec((tr, d), lambda i: (i, 0))]
    for dil in dils:
        out_shape.append(jax.ShapeDtypeStruct((dil, s // dil, d), jnp.bfloat16))
        out_specs.append(pl.BlockSpec((dil, tr // dil, d), lambda i: (0, i, 0)))
    return pl.pallas_call(
        functools.partial(_rmsnorm_mix_kernel, dils=dils, tr=tr),
        out_shape=out_shape,
        grid=(s // tr,),
        in_specs=[pl.BlockSpec((tr, d), lambda i: (i, 0)),
                  pl.BlockSpec((1, d), lambda i: (0, 0))],
        out_specs=out_specs,
        compiler_params=_params(("parallel",)),
        name="rmsnorm_mix",
    )(x, g.reshape(1, d))


def _matmul_kernel(x_ref, w_ref, *rest, epilogue):
    *extra, o_ref = rest
    acc = jnp.dot(x_ref[...], w_ref[...], preferred_element_type=jnp.float32)
    if epilogue == "sigmoid":
        acc = jax.nn.sigmoid(acc)
    elif epilogue == "col_scale":
        acc = acc * extra[0][...]
    elif epilogue == "residual":
        acc = extra[0][...] + acc
    o_ref[...] = acc.astype(o_ref.dtype)


def _matmul(x, w, layer, col_off, ncols, out_dtype, epilogue="none", residual=None, col_scale=None,
            name="matmul"):
    m, k = x.shape
    tm = _tile(m, 1024)
    tn = _tile(math.gcd(ncols, col_off) if col_off else ncols, 1024)
    off = col_off // tn
    in_specs = [pl.BlockSpec((tm, k), lambda i, j: (i, 0)),
                pl.BlockSpec((None, k, tn), lambda i, j: (layer, 0, off + j))]
    args = [x, w]
    if col_scale is not None:
        epilogue = "col_scale"
        in_specs.append(pl.BlockSpec((1, tn), lambda i, j: (0, j)))
        args.append(col_scale)
    elif residual is not None:
        epilogue = "residual"
        in_specs.append(pl.BlockSpec((tm, tn), lambda i, j: (i, j)))
        args.append(residual)
    return pl.pallas_call(
        functools.partial(_matmul_kernel, epilogue=epilogue),
        out_shape=jax.ShapeDtypeStruct((m, ncols), out_dtype),
        grid=(m // tm, ncols // tn),
        in_specs=in_specs,
        out_specs=pl.BlockSpec((tm, tn), lambda i, j: (i, j)),
        compiler_params=_params(("parallel", "parallel")),
        name=name,
    )(*args)


def _merge_kernel(oa_ref, ob_ref, oc_ref, wa_ref, wb_ref, wc_ref, ga_ref, gb_ref, gc_ref, o_ref):
    f32 = jnp.float32
    ya = jnp.dot(oa_ref[...], wa_ref[...], preferred_element_type=f32)
    yb = jnp.dot(ob_ref[...], wb_ref[...], preferred_element_type=f32)
    yc = jnp.dot(oc_ref[...], wc_ref[...], preferred_element_type=f32)
    merged = (ga_ref[...].astype(f32) * ya + gb_ref[...].astype(f32) * yb
              + gc_ref[...].astype(f32) * yc)
    o_ref[...] = merged.astype(o_ref.dtype)


def _merge(o_a, o_b, o_c, gates, w_a, w_b, w_c, layer):
    s = o_a.shape[0]
    d = w_a.shape[-1]
    tm = _tile(s, 1024)
    tn = _tile(d, 512)
    nj = d // tn
    row = lambda width: pl.BlockSpec((tm, width), lambda i, j: (i, 0))
    wspec = lambda kk: pl.BlockSpec((None, kk, tn), lambda i, j: (layer, 0, j))
    gspec = lambda b: pl.BlockSpec((tm, tn), lambda i, j: (i, b * nj + j))
    return pl.pallas_call(
        _merge_kernel,
        out_shape=jax.ShapeDtypeStruct((s, d), jnp.bfloat16),
        grid=(s // tm, nj),
        in_specs=[row(A_OUT), row(B_OUT), row(C_OUT),
                  wspec(A_OUT), wspec(B_OUT), wspec(C_OUT),
                  gspec(0), gspec(1), gspec(2)],
        out_specs=pl.BlockSpec((tm, tn), lambda i, j: (i, j)),
        compiler_params=_params(("parallel", "parallel")),
        name="merge",
    )(o_a, o_b, o_c, w_a, w_b, w_c, gates, gates, gates)


NORM_ROWS = 64


def _mlp_kernel(h_ref, gin_ref, wu_ref, wd_ref, *rest, final_norm):
    *gout_ref, o_ref, v_ref = rest
    f = pl.program_id(1)

    def norm_rows(src_ref, dst_ref, g_ref):
        def step(c, carry):
            rows = pl.ds(pl.multiple_of(c * NORM_ROWS, NORM_ROWS), NORM_ROWS)
            x = src_ref[rows, :]
            y = x * lax.rsqrt(jnp.mean(x * x, axis=-1, keepdims=True) + EPS) * g_ref[...]
            dst_ref[rows, :] = y.astype(dst_ref.dtype)
            return carry
        lax.fori_loop(0, src_ref.shape[0] // NORM_ROWS, step, 0)

    @pl.when(f == 0)
    def _():
        o_ref[...] = h_ref[...]
        norm_rows(h_ref, v_ref, gin_ref)

    a = jnp.dot(v_ref[...], wu_ref[...], preferred_element_type=jnp.float32)
    a = jnp.square(jnp.maximum(a, 0.0)).astype(jnp.bfloat16)
    o_ref[...] += jnp.dot(a, wd_ref[...], preferred_element_type=jnp.float32)

    if final_norm:
        @pl.when(f == pl.num_programs(1) - 1)
        def _():
            norm_rows(o_ref, o_ref, gout_ref[0])


def _mlp(h, g_in, w_up, w_down, layer, final_g=None):
    s, d = h.shape
    dff = w_up.shape[-1]
    tm = _tile(s, 512)
    tf = _tile(dff, 512)
    vec = pl.BlockSpec((1, d), lambda i, f: (0, 0))
    in_specs = [pl.BlockSpec((tm, d), lambda i, f: (i, 0), pipeline_mode=pl.Buffered(1)),
                vec,
                pl.BlockSpec((None, d, tf), lambda i, f: (layer, 0, f)),
                pl.BlockSpec((None, tf, d), lambda i, f: (layer, f, 0))]
    args = [h, g_in.reshape(1, d), w_up, w_down]
    if final_g is not None:
        in_specs.append(vec)
        args.append(final_g.reshape(1, d))
    return pl.pallas_call(
        functools.partial(_mlp_kernel, final_norm=final_g is not None),
        out_shape=jax.ShapeDtypeStruct((s, d), jnp.float32),
        grid=(s // tm, dff // tf),
        in_specs=in_specs,
        out_specs=pl.BlockSpec((tm, d), lambda i, f: (i, 0)),
        scratch_shapes=[pltpu.VMEM((tm, d), jnp.bfloat16)],
        compiler_params=_params(("parallel", "arbitrary")),
        name="mlp",
    )(*args)


A_TQ = 256
A_SUB = 128
A_BLK = 64
LOG2E = 1.4426950408889634
FAR = 3e32


def _halo_window(p_ref, m_ref, n_ref, cols, r0, nk, halo, tq):
    lo, hi = r0, r0 + nk
    pieces = []
    if lo < halo:
        pieces.append(p_ref[lo:min(hi, halo), cols])
    a, b = max(lo, halo), min(hi, halo + tq)
    if a < b:
        pieces.append(m_ref[a - halo:b - halo, cols])
    a = max(lo, halo + tq)
    if a < hi:
        pieces.append(n_ref[a - halo - tq:hi - halo - tq, cols])
    return pieces[0] if len(pieces) == 1 else jnp.concatenate(pieces, axis=0)


def _band_distance(i, last, sub, nsub_rows, halo, tq):
    nk = nsub_rows + 2 * halo
    a = lax.broadcasted_iota(jnp.int32, (nsub_rows, nk), 0)
    c = lax.broadcasted_iota(jnp.int32, (nsub_rows, nk), 1)
    dist = jnp.abs(c - halo - a)
    row = c + sub * nsub_rows
    valid = (dist <= halo) & ((row >= halo) | (i > 0)) & ((row < tq + halo) | (i < last))
    return jnp.where(valid, dist.astype(jnp.float32), FAR)


def _attn_a_kernel(q_ref, kp_ref, km_ref, kn_ref, vp_ref, vm_ref, vn_ref, o_ref, lse_ref, *, dil, tq):
    i = pl.program_id(1)
    last = pl.num_programs(1) - 1
    nsub = tq // A_SUB if tq >= A_SUB else 1
    rows = tq // nsub
    nk = rows + 2 * A_BLK
    lane = lax.broadcasted_iota(jnp.int32, (rows, LANE), 1)
    slopes = [LOG2E * dil * sl for sl in _alibi_slopes(A_HEADS)]
    probs = [(sub, h) for sub in range(nsub) for h in range(A_HEADS)]
    cols = lambda h: slice(h * HEAD_DIM, (h + 1) * HEAD_DIM)
    dists = [_band_distance(i, last, sub, rows, A_BLK, tq) for sub in range(nsub)]
    scores = []
    for sub, h in probs:
        k = _halo_window(kp_ref, km_ref, kn_ref, cols(h), sub * rows, nk, A_BLK, tq)
        s = lax.dot_general(q_ref[sub * rows:(sub + 1) * rows, cols(h)], k, (((1,), (1,)), ((), ())),
                            preferred_element_type=jnp.float32)
        scores.append(s - slopes[h] * dists[sub])
    es, denoms = [], []
    lse_tiles = [jnp.zeros((rows, LANE), jnp.float32) for _ in range(nsub)]
    for (sub, h), s in zip(probs, scores):
        m = jnp.max(s, axis=-1, keepdims=True)
        e = jnp.exp2(s - m)
        denom = jnp.sum(e, axis=-1, keepdims=True)
        es.append(e.astype(jnp.bfloat16))
        denoms.append(denom)
        lse_tiles[sub] = jnp.where(lane == h, m + jnp.log2(denom), lse_tiles[sub])
    outs = []
    for (sub, h), e, denom in zip(probs, es, denoms):
        v = _halo_window(vp_ref, vm_ref, vn_ref, cols(h), sub * rows, nk, A_BLK, tq)
        outs.append(jnp.dot(e, v, preferred_element_type=jnp.float32) / denom)
    o_rows = [jnp.concatenate(outs[sub * A_HEADS:(sub + 1) * A_HEADS], axis=1) for sub in range(nsub)]
    o_ref[...] = o_rows[0] if nsub == 1 else jnp.concatenate(o_rows, axis=0)
    lse_ref[...] = lse_tiles[0] if nsub == 1 else jnp.concatenate(lse_tiles, axis=0)


def _attn_a_pattern(pa, g):
    dil, ls, _ = pa.shape
    tq = _tile(ls, A_TQ)
    r64 = tq // A_BLK
    nblk64 = ls // A_BLK

    def main(c):
        return pl.BlockSpec((None, tq, A_OUT), lambda r, i: (r, i, c))

    def prev(c):
        return pl.BlockSpec((None, A_BLK, A_OUT), lambda r, i: (r, jnp.maximum(i * r64 - 1, 0), c))

    def nxt(c):
        return pl.BlockSpec((None, A_BLK, A_OUT),
                            lambda r, i: (r, jnp.minimum((i + 1) * r64, nblk64 - 1), c))

    return pl.pallas_call(
        functools.partial(_attn_a_kernel, dil=dil, tq=tq),
        out_shape=(jax.ShapeDtypeStruct((dil, ls, A_OUT), jnp.float32),
                   jax.ShapeDtypeStruct((dil, ls, LANE), jnp.float32)),
        grid=(dil, ls // tq),
        in_specs=[main(0), prev(1), main(1), nxt(1), prev(2), main(2), nxt(2)],
        out_specs=(pl.BlockSpec((None, tq, A_OUT), lambda r, i: (r, i, 0)),
                   pl.BlockSpec((None, tq, LANE), lambda r, i: (r, i, 0))),
        compiler_params=_params(("parallel", "parallel")),
        name=f"attn_a{g}",
    )(pa, pa, pa, pa, pa, pa, pa)


def _combine_a_kernel(o0_ref, o1_ref, o2_ref, l0_ref, l1_ref, l2_ref, o_ref, *scratch, dils, tr):
    o_in, l_in = [o0_ref, o1_ref, o2_ref], [l0_ref, l1_ref, l2_ref]
    nat = [scratch[g] if dil > 1 else None for g, dil in enumerate(dils)]

    def ungroup(g, src):
        dil = dils[g]
        if dil == 1:
            return src(0)
        for r in range(dil):
            nat[g][pl.ds(r, tr // dil, stride=dil), :] = src(r)
        return nat[g][...]

    l0, l1, l2 = (ungroup(g, lambda r, g=g: l_in[g][r]) for g in range(len(dils)))
    mx = jnp.maximum(jnp.maximum(l0, l1), l2)
    e0, e1, e2 = jnp.exp2(l0 - mx), jnp.exp2(l1 - mx), jnp.exp2(l2 - mx)
    inv = 1.0 / (e0 + e1 + e2)
    w = [e0 * inv, e1 * inv, e2 * inv]
    for h in range(A_HEADS):
        sl = slice(h * HEAD_DIM, (h + 1) * HEAD_DIM)
        o = sum(w[g][:, h:h + 1] * ungroup(g, lambda r, g=g: o_in[g][r, :, sl]) for g in range(len(dils)))
        o_ref[:, sl] = o.astype(o_ref.dtype)


def _attn_a(pas):
    dils = tuple(p.shape[0] for p in pas)
    s = pas[0].shape[0] * pas[0].shape[1]
    outs = [_attn_a_pattern(p, g) for g, p in enumerate(pas)]
    tr = _tile(s, 512)
    ospec = lambda dil: pl.BlockSpec((dil, tr // dil, A_OUT), lambda i: (0, i, 0))
    lspec = lambda dil: pl.BlockSpec((dil, tr // dil, LANE), lambda i: (0, i, 0))
    scratch = [pltpu.VMEM((tr, LANE), jnp.float32) for _ in dils]
    return pl.pallas_call(
        functools.partial(_combine_a_kernel, dils=dils, tr=tr),
        out_shape=jax.ShapeDtypeStruct((s, A_OUT), jnp.bfloat16),
        grid=(s // tr,),
        in_specs=[ospec(d) for d in dils] + [lspec(d) for d in dils],
        out_specs=pl.BlockSpec((tr, A_OUT), lambda i: (i, 0)),
        scratch_shapes=scratch,
        compiler_params=_params(("parallel",)),
        name="combine_a",
    )(*[o for o, _ in outs], *[l for _, l in outs])


B_TQ = 256


def _attn_b_kernel(slope_ref, sink_ref, q_ref, kp_ref, km_ref, kn_ref, vp_ref, vm_ref, vn_ref, o_ref, *, tq):
    c = pl.program_id(0)
    i = pl.program_id(1)
    last = pl.num_programs(1) - 1
    nsub = tq // B_HALF
    nk = 3 * B_HALF
    allc = slice(None)
    probs = [(sub, g) for sub in range(nsub) for g in range(B_GROUP)]
    cols = lambda g: slice(g * HEAD_DIM, (g + 1) * HEAD_DIM)
    slopes = [slope_ref[c * B_GROUP + g] for g in range(B_GROUP)]
    sinks = [sink_ref[c * B_GROUP + g] * LOG2E for g in range(B_GROUP)]
    dists = [_band_distance(i, last, sub, B_HALF, B_HALF, tq) for sub in range(nsub)]
    ks = [_halo_window(kp_ref, km_ref, kn_ref, allc, sub * B_HALF, nk, B_HALF, tq) for sub in range(nsub)]
    scores = []
    for sub, g in probs:
        s = lax.dot_general(q_ref[sub * B_HALF:(sub + 1) * B_HALF, cols(g)], ks[sub],
                            (((1,), (1,)), ((), ())), preferred_element_type=jnp.float32)
        scores.append(s - slopes[g] * dists[sub])
    es, denoms = [], []
    for (sub, g), s in zip(probs, scores):
        m = jnp.maximum(jnp.max(s, axis=-1, keepdims=True), sinks[g])
        e = jnp.exp2(s - m)
        es.append(e.astype(jnp.bfloat16))
        denoms.append(jnp.sum(e, axis=-1, keepdims=True) + jnp.exp2(sinks[g] - m))
    vs = [_halo_window(vp_ref, vm_ref, vn_ref, allc, sub * B_HALF, nk, B_HALF, tq) for sub in range(nsub)]
    outs = [(jnp.dot(e, vs[sub], preferred_element_type=jnp.float32) / denom).astype(o_ref.dtype)
            for (sub, g), e, denom in zip(probs, es, denoms)]
    o_rows = [jnp.concatenate(outs[sub * B_GROUP:(sub + 1) * B_GROUP], axis=1) for sub in range(nsub)]
    o_ref[...] = o_rows[0] if nsub == 1 else jnp.concatenate(o_rows, axis=0)


def _attn_b(qkv, sinks):
    s, nc = qkv.shape
    tq = _tile(s, B_TQ)
    r128 = tq // B_HALF
    nblk = s // B_HALF
    gw = B_GROUP * HEAD_DIM
    cq0 = 0
    ck0 = B_HEADS
    cv0 = ck0 + B_KV_HEADS
    smem = pl.BlockSpec(memory_space=pltpu.SMEM)

    def main(c0):
        return pl.BlockSpec((tq, HEAD_DIM), lambda c, i: (i, c0 + c))

    def prev(c0):
        return pl.BlockSpec((B_HALF, HEAD_DIM), lambda c, i: (jnp.maximum(i * r128 - 1, 0), c0 + c))

    def nxt(c0):
        return pl.BlockSpec((B_HALF, HEAD_DIM),
                            lambda c, i: (jnp.minimum((i + 1) * r128, nblk - 1), c0 + c))

    slopes = jnp.asarray([LOG2E * sl for sl in _alibi_slopes(B_HEADS)], jnp.float32)
    return pl.pallas_call(
        functools.partial(_attn_b_kernel, tq=tq),
        out_shape=jax.ShapeDtypeStruct((s, B_OUT), jnp.bfloat16),
        grid=(B_KV_HEADS, s // tq),
        in_specs=[smem, smem,
                  pl.BlockSpec((tq, gw), lambda c, i: (i, cq0 + c)),
                  prev(ck0), main(ck0), nxt(ck0), prev(cv0), main(cv0), nxt(cv0)],
        out_specs=pl.BlockSpec((tq, gw), lambda c, i: (i, c)),
        compiler_params=_params(("parallel", "parallel")),
        name="attn_b",
    )(slopes, sinks.astype(jnp.float32), qkv, qkv, qkv, qkv, qkv, qkv, qkv)


C_TQ = 512
C_TK = 512
C_AUG = 16
C_SUMROWS = 16
C_UNDERFLOW = 170.0


def _attn_c_kernel(slope_ref, inv_ref, q_ref, k_ref, v_ref, lq1_ref, lk1_ref, lq2_ref, lk2_ref, g_ref, *rest,
                   tq, tk, nkv, lam_init, casts):
    f32, bf16 = jnp.float32, jnp.bfloat16
    ncast = len(casts)
    cast_in, o_ref, cast_out = rest[:ncast], rest[ncast], rest[ncast + 1:2 * ncast + 1]
    vt_ref, qzt_ref, m_ref, acc_ref, s0_ref, s1_ref, mx0_ref, mx1_ref, kn2_ref = rest[2 * ncast + 1:]
    h = pl.program_id(0)
    i = pl.program_id(1)
    slope = slope_ref[h]
    half = tk // 2
    lane_k = lax.broadcasted_iota(jnp.int32, (tk, 2 * C_QK_DIM), 1)

    step = h * pl.num_programs(1) + i
    for (start, nblk), w_ref, wb_ref in zip(casts, cast_in, cast_out):
        @pl.when((step >= start) & (step < start + nblk))
        def _():
            wb_ref[...] = w_ref[...].astype(wb_ref.dtype)

    @pl.when(i == 0)
    def _():
        def tbody(c, kn2):
            st = pl.multiple_of(c * tk, tk)
            vt = v_ref[pl.ds(st, tk), :].astype(f32).T.astype(bf16)
            ones_row = lax.broadcasted_iota(jnp.int32, (C_SUMROWS, tk), 0) == 0
            vt_ref[c] = jnp.concatenate([vt, ones_row.astype(bf16)], axis=0)
            kk = k_ref[pl.ds(st, tk), :].astype(f32)
            sq = kk * kk
            n_all = jnp.sum(sq, axis=1, keepdims=True)
            n_0 = jnp.sum(jnp.where(lane_k < C_QK_DIM, sq, 0.0), axis=1, keepdims=True)
            return jnp.maximum(kn2, jnp.max(jnp.maximum(n_0, n_all - n_0)))
        kn2_ref[0] = lax.fori_loop(0, nkv, tbody, jnp.float32(0.0))

    qt = q_ref[...].astype(f32).T
    row = lax.broadcasted_iota(jnp.int32, qt.shape, 0)
    qzt_ref[...] = jnp.concatenate([jnp.where(row < C_QK_DIM, qt, 0.0), jnp.where(row >= C_QK_DIM, qt, 0.0)],
                                   axis=1).astype(bf16)
    m_ref[...] = jnp.full(m_ref.shape, NEG, f32)
    acc_ref[...] = jnp.zeros(acc_ref.shape, f32)

    def chunk_at(t):
        j = jlo + t - 1
        return jnp.where(t == 0, jd, j + (j >= jd).astype(jnp.int32))

    def score_stage(t, s_ref, mx_ref):
        j = chunk_at(t)
        dc = (j * tk + half - i * tq).astype(f32)
        coef = jnp.where(j > jd, -slope, slope)
        base = jnp.where(rowq < 3, coef, -coef * (a_q - dc))
        p1 = base.astype(bf16)
        r1 = base - p1.astype(f32)
        p2 = r1.astype(bf16)
        p3 = (r1 - p2.astype(f32)).astype(bf16)
        piece = rowq % 3
        aug_q = jnp.where(rowq < 6, jnp.where(piece == 0, p1, jnp.where(piece == 1, p2, p3)),
                          jnp.zeros_like(p1))
        kc = k_ref[pl.ds(pl.multiple_of(j * tk, tk), tk), :]
        lhs = jnp.concatenate([kc, aug_k], axis=1)
        rhs = jnp.concatenate([qzt_ref[...], aug_q, zpad], axis=0)
        st = jnp.dot(lhs, rhs, preferred_element_type=f32)
        s_ref[...] = st
        mx_ref[...] = jnp.max(st, axis=0, keepdims=True)

    def softmax_stage(t, s_ref, mx_ref):
        m_old = m_ref[...]
        m_new = jnp.maximum(m_old, mx_ref[...])
        alpha = jnp.exp2(m_old - m_new)
        p = jnp.exp2(s_ref[...] - m_new).astype(bf16)
        acc_ref[...] = alpha * acc_ref[...] + jnp.dot(vt_ref[chunk_at(t)], p, preferred_element_type=f32)
        m_ref[...] = m_new

    colk = lax.broadcasted_iota(jnp.int32, (tk, LANE), 1)
    bk = (lax.broadcasted_iota(jnp.int32, (tk, LANE), 0) - half).astype(f32)
    aug_k = jnp.where(colk < 3, bk, jnp.where(colk < 6, 1.0, 0.0)).astype(bf16)
    rowq = lax.broadcasted_iota(jnp.int32, (C_AUG, 2 * tq), 0)
    a_q = (lax.broadcasted_iota(jnp.int32, (C_AUG, 2 * tq), 1) % tq).astype(f32)
    zpad = jnp.zeros((2 * LANE - 2 * C_QK_DIM - C_AUG, 2 * tq), bf16)

    jd = (i * tq) // tk
    kd = k_ref[pl.ds(pl.multiple_of(jd * tk, tk), tk), :]
    sd = jnp.dot(kd, qzt_ref[...], preferred_element_type=f32)
    rel = (lax.broadcasted_iota(jnp.int32, (tk, tq), 0) - lax.broadcasted_iota(jnp.int32, (tk, tq), 1)
           + (jd * tk - i * tq))
    bias = slope * jnp.abs(rel).astype(f32)
    sd = sd - jnp.concatenate([bias, bias], axis=1)
    s0_ref[...] = sd
    mxd = jnp.max(sd, axis=0, keepdims=True)
    mx0_ref[...] = mxd

    sqq = qt * qt
    n_all = jnp.sum(sqq, axis=0, keepdims=True)
    n_0 = jnp.sum(jnp.where(row < C_QK_DIM, sqq, 0.0), axis=0, keepdims=True)
    s_max = jnp.max(jnp.sqrt(jnp.maximum(n_0, n_all - n_0) * kn2_ref[0]))
    reach = (s_max + C_UNDERFLOW - jnp.min(mxd)) * inv_ref[h]
    wnd = jnp.minimum(reach, float(nkv)).astype(jnp.int32) + 1
    jlo = jnp.maximum(jd - wnd, 0)
    jhi = jnp.minimum(jd + wnd, nkv - 1)
    nvis = jhi - jlo + 1

    def pair(u):
        score_stage(2 * u + 1, s1_ref, mx1_ref)
        softmax_stage(2 * u, s0_ref, mx0_ref)
        score_stage(2 * u + 2, s0_ref, mx0_ref)
        softmax_stage(2 * u + 1, s1_ref, mx1_ref)

    def body(w, carry):
        pair(2 * w)
        pair(2 * w + 1)
        return carry

    npair = (nvis - 1) // 2
    lax.fori_loop(0, npair // 2, body, 0)

    @pl.when(npair % 2 == 1)
    def _():
        pair(npair - 1)

    @pl.when(nvis % 2 == 0)
    def _():
        score_stage(nvis - 1, s1_ref, mx1_ref)
        softmax_stage(nvis - 2, s0_ref, mx0_ref)
        softmax_stage(nvis - 1, s1_ref, mx1_ref)

    @pl.when(nvis % 2 == 1)
    def _():
        softmax_stage(nvis - 1, s0_ref, mx0_ref)

    lam = (jnp.exp(jnp.sum(lq1_ref[...] * lk1_ref[...], axis=-1, keepdims=True))
           - jnp.exp(jnp.sum(lq2_ref[...] * lk2_ref[...], axis=-1, keepdims=True)) + lam_init)
    o = acc_ref[:C_V_DIM, :] / acc_ref[C_V_DIM:C_V_DIM + 1, :]
    o = o[:, :tq] - lam * o[:, tq:]
    y = o * lax.rsqrt(jnp.mean(o * o, axis=0, keepdims=True) + EPS)
    o_ref[...] = (y * g_ref[...] * (1.0 - lam_init)).T.astype(o_ref.dtype)


CAST_BLOCK_BYTES = 2 * 1024 * 1024


def _cast_jobs(weights, nsteps):
    jobs, start = [], 0
    for w, layer in weights:
        _, r, c = w.shape
        rows = max(16, _tile(r, max(16, CAST_BLOCK_BYTES // (4 * c))))
        nblk = r // rows
        assert r % rows == 0 and nblk <= nsteps, (w.shape, rows, nsteps)
        if start + nblk > nsteps:
            start = 0
        jobs.append((w, layer, rows, nblk, start))
        start += nblk
    return jobs


def _attn_c(qkv, lq1, lk1, lq2, lk2, norm_g, lam_init, cast_weights=()):
    s, nc = qkv.shape
    tq = _tile(s, C_TQ)
    tk = _tile(s, C_TK)
    nkv = s // tk
    assert tk % tq == 0, (s, tq, tk)
    nq = s // tq
    cq0 = B_COLS // HEAD_DIM
    ck0 = cq0 + C_HEADS
    cv0 = ck0 + C_HEADS
    smem = pl.BlockSpec(memory_space=pltpu.SMEM)
    vec = lambda n: pl.BlockSpec((1, n), lambda h, i: (0, 0))
    slopes_l2 = [LOG2E * sl for sl in _alibi_slopes(C_HEADS)]
    slopes = jnp.asarray(slopes_l2, jnp.float32)
    inv_reach = jnp.asarray([1.0 / (sl * tk) for sl in slopes_l2], jnp.float32)
    jobs = _cast_jobs(cast_weights, C_HEADS * nq)

    def job_block(start, nblk):
        return lambda h, i: jnp.clip(h * nq + i - start, 0, nblk - 1)

    cast_in_specs = [pl.BlockSpec((None, rows, w.shape[2]),
                                  lambda h, i, layer=layer, blk=job_block(start, nblk): (layer, blk(h, i), 0))
                     for w, layer, rows, nblk, start in jobs]
    cast_out_specs = [pl.BlockSpec((None, rows, w.shape[2]),
                                   lambda h, i, blk=job_block(start, nblk): (0, blk(h, i), 0))
                      for w, layer, rows, nblk, start in jobs]
    cast_out_shapes = [jax.ShapeDtypeStruct((1,) + w.shape[1:], jnp.bfloat16) for w, *_ in jobs]
    outs = pl.pallas_call(
        functools.partial(_attn_c_kernel, tq=tq, tk=tk, nkv=nkv, lam_init=lam_init,
                          casts=tuple((start, nblk) for *_, nblk, start in jobs)),
        out_shape=[jax.ShapeDtypeStruct((s, C_OUT), jnp.bfloat16)] + cast_out_shapes,
        grid=(C_HEADS, nq),
        in_specs=[smem, smem,
                  pl.BlockSpec((tq, HEAD_DIM), lambda h, i: (i, cq0 + h)),
                  pl.BlockSpec((s, HEAD_DIM), lambda h, i: (0, ck0 + h)),
                  pl.BlockSpec((s, HEAD_DIM), lambda h, i: (0, cv0 + h)),
                  vec(C_QK_DIM), vec(C_QK_DIM), vec(C_QK_DIM), vec(C_QK_DIM),
                  pl.BlockSpec((C_V_DIM, 1), lambda h, i: (0, 0))] + cast_in_specs,
        out_specs=[pl.BlockSpec((tq, C_V_DIM), lambda h, i: (i, h))] + cast_out_specs,
        scratch_shapes=[pltpu.VMEM((nkv, C_V_DIM + C_SUMROWS, tk), jnp.bfloat16),
                        pltpu.VMEM((2 * C_QK_DIM, 2 * tq), jnp.bfloat16),
                        pltpu.VMEM((1, 2 * tq), jnp.float32),
                        pltpu.VMEM((C_V_DIM + C_SUMROWS, 2 * tq), jnp.float32),
                        pltpu.VMEM((tk, 2 * tq), jnp.float32),
                        pltpu.VMEM((tk, 2 * tq), jnp.float32),
                        pltpu.VMEM((1, 2 * tq), jnp.float32),
                        pltpu.VMEM((1, 2 * tq), jnp.float32),
                        pltpu.SMEM((1,), jnp.float32)],
        compiler_params=_params(("arbitrary", "arbitrary")),
        name="attn_c",
    )(slopes, inv_reach, qkv, qkv, qkv, lq1.reshape(1, -1), lk1.reshape(1, -1), lq2.reshape(1, -1),
      lk2.reshape(1, -1), norm_g.reshape(-1, 1), *[w for w, *_ in jobs])
    return outs[0], list(outs[1:])


def kernel(x, mix_norm_g, w_in, b_sink, diff_lq1, diff_lk1, diff_lq2, diff_lk2, diff_norm_g,
           w_branch_a, w_branch_b, w_branch_c, w_out, mlp_norm_g, w_up, w_down, final_norm_g):
    bsz, seq, d = x.shape
    depth = w_in.shape[0]
    bf16 = jnp.bfloat16
    w_a_b, w_b_b, w_c_b = (w.astype(bf16) for w in (w_branch_a, w_branch_b, w_branch_c))
    w_in_first = w_in[:1].astype(bf16)
    col = jnp.arange(B_COLS + C_COLS)
    bc_scale = jnp.where(col < B_HEADS * HEAD_DIM, HEAD_DIM ** -0.5 * LOG2E,
                         jnp.where((col >= B_COLS) & (col < B_COLS + C_HEADS * 2 * C_QK_DIM),
                                   C_QK_DIM ** -0.5 * LOG2E, 1.0)).astype(jnp.float32).reshape(1, -1)
    dils = tuple(dil for _, dil in A_PATTERNS)
    pat_cols = 3 * A_OUT
    a_scale = jnp.where(jnp.arange(pat_cols) < A_OUT, HEAD_DIM ** -0.5 * LOG2E,
                        1.0).astype(jnp.float32).reshape(1, -1)
    xs = x.reshape(bsz * seq, d)
    outs = []
    for b in range(bsz):
        h = xs[b * seq:(b + 1) * seq]
        w_in_l = w_in_first
        for l in range(depth):
            u, *u_dil = _rmsnorm_mix(h, mix_norm_g[l], tuple(dl for dl in dils if dl > 1))
            u_by_dil = {1: u, **{dl: ud.reshape(seq, d) for dl, ud in zip([dl for dl in dils if dl > 1], u_dil)}}
            pas = [_matmul(u_by_dil[dl], w_in_l, 0, g * pat_cols, pat_cols, bf16, col_scale=a_scale,
                           name=f"proj_a{g}").reshape(dl, seq // dl, pat_cols)
                   for g, dl in enumerate(dils)]
            pbc = _matmul(u, w_in_l, 0, OFF_B, B_COLS + C_COLS, bf16, col_scale=bc_scale, name="proj_bc")
            gates = _matmul(u, w_in_l, 0, OFF_G, N_BRANCH * d, bf16, epilogue="sigmoid", name="proj_gates")
            o_a = _attn_a(pas)
            o_b = _attn_b(pbc, b_sink[l])
            lam_init = 0.8 - 0.6 * math.exp(-0.3 * l)
            todo = [(w_up, l), (w_down, l), (w_out, l)] + ([(w_in, l + 1)] if l + 1 < depth else [])
            o_c, cast = _attn_c(pbc, diff_lq1[l], diff_lk1[l], diff_lq2[l], diff_lk2[l], diff_norm_g[l],
                                lam_init, cast_weights=todo)
            w_up_l, w_down_l, w_out_l = cast[:3]
            w_in_l = cast[3] if l + 1 < depth else None
            merged = _merge(o_a, o_b, o_c, gates, w_a_b, w_b_b, w_c_b, l)
            h = _matmul(merged, w_out_l, 0, 0, d, jnp.float32, residual=h, name="out_proj")
            h = _mlp(h, mlp_norm_g[l], w_up_l, w_down_l, 0, final_g=final_norm_g if l == depth - 1 else None)
        outs.append(h.astype(x.dtype))
    return jnp.concatenate(outs, axis=0).reshape(bsz, seq, d)
```

```python
import functools
import math

import jax
import jax.numpy as jnp
from jax import lax
from jax.experimental import pallas as pl
from jax.experimental.pallas import tpu as pltpu

HEAD_DIM = 128
A_PATTERNS = ((128, 1), (512, 4), (2048, 16))
A_HEADS = 8
N_PAT = len(A_PATTERNS)
B_HEADS = 12
B_KV_HEADS = 4
B_GROUP = B_HEADS // B_KV_HEADS
B_HALF = 128
C_HEADS = 12
C_QK_DIM = 64
C_V_DIM = 2 * C_QK_DIM
N_BRANCH = 3
EPS = 1e-6
NEG = -1e30

A_COLS = N_PAT * 3 * A_HEADS * HEAD_DIM
B_COLS = (B_HEADS + 2 * B_KV_HEADS) * HEAD_DIM
C_COLS = C_HEADS * (4 * C_QK_DIM + C_V_DIM)
OFF_B = A_COLS
OFF_C = OFF_B + B_COLS
OFF_G = OFF_C + C_COLS
A_OUT = A_HEADS * HEAD_DIM
B_OUT = B_HEADS * HEAD_DIM
C_OUT = C_HEADS * C_V_DIM

LANE = 128
VMEM_LIMIT = 56 * 1024 * 1024


def _tile(n, pref):
    t = pref
    while t > 1 and n % t:
        t //= 2
    return t


def _params(sem):
    return pltpu.CompilerParams(dimension_semantics=sem, vmem_limit_bytes=VMEM_LIMIT)


def _alibi_slopes(n):
    return [2.0 ** (-8.0 * i / n) for i in range(1, n + 1)]


def _rmsnorm_mix_kernel(x_ref, g_ref, o_ref, *grouped_refs, dils, tr):
    x = x_ref[...]
    y = (x * lax.rsqrt(jnp.mean(x * x, axis=-1, keepdims=True) + EPS) * g_ref[...]).astype(o_ref.dtype)
    o_ref[...] = y
    dst = lax.broadcasted_iota(jnp.int32, (tr, tr), 0)
    src = lax.broadcasted_iota(jnp.int32, (tr, tr), 1)
    for dil, s_ref in zip(dils, grouped_refs):
        n = tr // dil
        perm = (src == (dst % n) * dil + dst // n).astype(y.dtype)
        yp = jnp.dot(perm, y, preferred_element_type=jnp.float32).astype(y.dtype)
        for r in range(dil):
            s_ref[r] = yp[r * n:(r + 1) * n]


def _rmsnorm_mix(x, g, dils):
    s, d = x.shape
    tr = _tile(s, 256)
    out_shape = [jax.ShapeDtypeStruct((s, d), jnp.bfloat16)]
    out_specs = [pl.BlockSpec((tr, d), lambda i: (i, 0))]
    for dil in dils:
        out_shape.append(jax.ShapeDtypeStruct((dil, s // dil, d), jnp.bfloat16))
        out_specs.append(pl.BlockSpec((dil, tr // dil, d), lambda i: (0, i, 0)))
    return pl.pallas_call(
        functools.partial(_rmsnorm_mix_kernel, dils=dils, tr=tr),
        out_shape=out_shape,
        grid=(s // tr,),
        in_specs=[pl.BlockSpec((tr, d), lambda i: (i, 0)),
                  pl.BlockSpec((1, d), lambda i: (0, 0))],
        out_specs=out_specs,
        compiler_params=_params(("parallel",)),
        name="rmsnorm_mix",
    )(x, g.reshape(1, d))


def _matmul_kernel(x_ref, w_ref, *rest, epilogue):
    *extra, o_ref = rest
    acc = jnp.dot(x_ref[...], w_ref[...], preferred_element_type=jnp.float32)
    if epilogue == "sigmoid":
        acc = jax.nn.sigmoid(acc)
    elif epilogue == "col_scale":
        acc = acc * extra[0][...]
    elif epilogue == "residual":
        acc = extra[0][...] + acc
    o_ref[...] = acc.astype(o_ref.dtype)


def _matmul(x, w, layer, col_off, ncols, out_dtype, epilogue="none", residual=None, col_scale=None,
            name="matmul"):
    m, k = x.shape
    tm = _tile(m, 1024)
    tn = _tile(math.gcd(ncols, col_off) if col_off else ncols, 1024)
    off = col_off // tn
    in_specs = [pl.BlockSpec((tm, k), lambda i, j: (i, 0)),
                pl.BlockSpec((None, k, tn), lambda i, j: (layer, 0, off + j))]
    args = [x, w]
    if col_scale is not None:
        epilogue = "col_scale"
        in_specs.append(pl.BlockSpec((1, tn), lambda i, j: (0, j)))
        args.append(col_scale)
    elif residual is not None:
        epilogue = "residual"
        in_specs.append(pl.BlockSpec((tm, tn), lambda i, j: (i, j)))
        args.append(residual)
    return pl.pallas_call(
        functools.partial(_matmul_kernel, epilogue=epilogue),
        out_shape=jax.ShapeDtypeStruct((m, ncols), out_dtype),
        grid=(m // tm, ncols // tn),
        in_specs=in_specs,
        out_specs=pl.BlockSpec((tm, tn), lambda i, j: (i, j)),
        compiler_params=_params(("parallel", "parallel")),
        name=name,
    )(*args)


def _merge_kernel(oa_ref, ob_ref, oc_ref, wa_ref, wb_ref, wc_ref, ga_ref, gb_ref, gc_ref, o_ref):
    f32 = jnp.float32
    ya = jnp.dot(oa_ref[...], wa_ref[...], preferred_element_type=f32)
    yb = jnp.dot(ob_ref[...], wb_ref[...], preferred_element_type=f32)
    yc = jnp.dot(oc_ref[...], wc_ref[...], preferred_element_type=f32)
    merged = (ga_ref[...].astype(f32) * ya + gb_ref[...].astype(f32) * yb
              + gc_ref[...].astype(f32) * yc)
    o_ref[...] = merged.astype(o_ref.dtype)


def _merge(o_a, o_b, o_c, gates, w_a, w_b, w_c, layer):
    s = o_a.shape[0]
    d = w_a.shape[-1]
    tm = _tile(s, 1024)
    tn = _tile(d, 1024)
    nj = d // tn
    row = lambda width: pl.BlockSpec((tm, width), lambda i, j: (i, 0))
    wspec = lambda kk: pl.BlockSpec((None, kk, tn), lambda i, j: (layer, 0, j))
    gspec = lambda b: pl.BlockSpec((tm, tn), lambda i, j: (i, b * nj + j))
    return pl.pallas_call(
        _merge_kernel,
        out_shape=jax.ShapeDtypeStruct((s, d), jnp.bfloat16),
        grid=(s // tm, nj),
        in_specs=[row(A_OUT), row(B_OUT), row(C_OUT),
                  wspec(A_OUT), wspec(B_OUT), wspec(C_OUT),
                  gspec(0), gspec(1), gspec(2)],
        out_specs=pl.BlockSpec((tm, tn), lambda i, j: (i, j)),
        compiler_params=_params(("parallel", "parallel")),
        name="merge",
    )(o_a, o_b, o_c, w_a, w_b, w_c, gates, gates, gates)


NORM_ROWS = 64


def _mlp_kernel(h_ref, gin_ref, wu_ref, wd_ref, *rest, final_norm):
    *gout_ref, o_ref, v_ref = rest
    f = pl.program_id(1)

    def norm_rows(src_ref, dst_ref, g_ref):
        def step(c, carry):
            rows = pl.ds(pl.multiple_of(c * NORM_ROWS, NORM_ROWS), NORM_ROWS)
            x = src_ref[rows, :]
            y = x * lax.rsqrt(jnp.mean(x * x, axis=-1, keepdims=True) + EPS) * g_ref[...]
            dst_ref[rows, :] = y.astype(dst_ref.dtype)
            return carry
        lax.fori_loop(0, src_ref.shape[0] // NORM_ROWS, step, 0)

    @pl.when(f == 0)
    def _():
        o_ref[...] = h_ref[...]
        norm_rows(h_ref, v_ref, gin_ref)

    a = jnp.dot(v_ref[...], wu_ref[...], preferred_element_type=jnp.float32)
    a = jnp.square(jnp.maximum(a, 0.0)).astype(jnp.bfloat16)
    o_ref[...] += jnp.dot(a, wd_ref[...], preferred_element_type=jnp.float32)

    if final_norm:
        @pl.when(f == pl.num_programs(1) - 1)
        def _():
            norm_rows(o_ref, o_ref, gout_ref[0])


def _mlp(h, g_in, w_up, w_down, layer, final_g=None):
    s, d = h.shape
    dff = w_up.shape[-1]
    tm = _tile(s, 512)
    tf = _tile(dff, 512)
    vec = pl.BlockSpec((1, d), lambda i, f: (0, 0))
    in_specs = [pl.BlockSpec((tm, d), lambda i, f: (i, 0)),
                vec,
                pl.BlockSpec((None, d, tf), lambda i, f: (layer, 0, f)),
                pl.BlockSpec((None, tf, d), lambda i, f: (layer, f, 0))]
    args = [h, g_in.reshape(1, d), w_up, w_down]
    if final_g is not None:
        in_specs.append(vec)
        args.append(final_g.reshape(1, d))
    return pl.pallas_call(
        functools.partial(_mlp_kernel, final_norm=final_g is not None),
        out_shape=jax.ShapeDtypeStruct((s, d), jnp.float32),
        grid=(s // tm, dff // tf),
        in_specs=in_specs,
        out_specs=pl.BlockSpec((tm, d), lambda i, f: (i, 0)),
        scratch_shapes=[pltpu.VMEM((tm, d), jnp.bfloat16)],
        compiler_params=_params(("parallel", "arbitrary")),
        name="mlp",
    )(*args)


A_TQ = 256
A_SUB = 128
A_BLK = 64
LOG2E = 1.4426950408889634
FAR = 3e32


def _halo_window(p_ref, m_ref, n_ref, cols, r0, nk, halo, tq):
    lo, hi = r0, r0 + nk
    pieces = []
    if lo < halo:
        pieces.append(p_ref[lo:min(hi, halo), cols])
    a, b = max(lo, halo), min(hi, halo + tq)
    if a < b:
        pieces.append(m_ref[a - halo:b - halo, cols])
    a = max(lo, halo + tq)
    if a < hi:
        pieces.append(n_ref[a - halo - tq:hi - halo - tq, cols])
    return pieces[0] if len(pieces) == 1 else jnp.concatenate(pieces, axis=0)


def _band_distance(i, last, sub, nsub_rows, halo, tq):
    nk = nsub_rows + 2 * halo
    a = lax.broadcasted_iota(jnp.int32, (nsub_rows, nk), 0)
    c = lax.broadcasted_iota(jnp.int32, (nsub_rows, nk), 1)
    dist = jnp.abs(c - halo - a)
    row = c + sub * nsub_rows
    valid = (dist <= halo) & ((row >= halo) | (i > 0)) & ((row < tq + halo) | (i < last))
    return jnp.where(valid, dist.astype(jnp.float32), FAR)


def _attn_a_kernel(q_ref, kp_ref, km_ref, kn_ref, vp_ref, vm_ref, vn_ref, o_ref, lse_ref, *, dil, tq):
    i = pl.program_id(1)
    last = pl.num_programs(1) - 1
    nsub = tq // A_SUB if tq >= A_SUB else 1
    rows = tq // nsub
    nk = rows + 2 * A_BLK
    lane = lax.broadcasted_iota(jnp.int32, (rows, LANE), 1)
    slopes = [LOG2E * dil * sl for sl in _alibi_slopes(A_HEADS)]
    probs = [(sub, h) for sub in range(nsub) for h in range(A_HEADS)]
    cols = lambda h: slice(h * HEAD_DIM, (h + 1) * HEAD_DIM)
    dists = [_band_distance(i, last, sub, rows, A_BLK, tq) for sub in range(nsub)]
    scores = []
    for sub, h in probs:
        k = _halo_window(kp_ref, km_ref, kn_ref, cols(h), sub * rows, nk, A_BLK, tq)
        s = lax.dot_general(q_ref[sub * rows:(sub + 1) * rows, cols(h)], k, (((1,), (1,)), ((), ())),
                            preferred_element_type=jnp.float32)
        scores.append(s - slopes[h] * dists[sub])
    es, denoms = [], []
    lse_tiles = [jnp.zeros((rows, LANE), jnp.float32) for _ in range(nsub)]
    for (sub, h), s in zip(probs, scores):
        m = jnp.max(s, axis=-1, keepdims=True)
        e = jnp.exp2(s - m)
        denom = jnp.sum(e, axis=-1, keepdims=True)
        es.append(e.astype(jnp.bfloat16))
        denoms.append(denom)
        lse_tiles[sub] = jnp.where(lane == h, m + jnp.log2(denom), lse_tiles[sub])
    outs = []
    for (sub, h), e, denom in zip(probs, es, denoms):
        v = _halo_window(vp_ref, vm_ref, vn_ref, cols(h), sub * rows, nk, A_BLK, tq)
        outs.append(jnp.dot(e, v, preferred_element_type=jnp.float32) / denom)
    o_rows = [jnp.concatenate(outs[sub * A_HEADS:(sub + 1) * A_HEADS], axis=1) for sub in range(nsub)]
    o_ref[...] = o_rows[0] if nsub == 1 else jnp.concatenate(o_rows, axis=0)
    lse_ref[...] = lse_tiles[0] if nsub == 1 else jnp.concatenate(lse_tiles, axis=0)


def _attn_a_pattern(pa, g):
    dil, ls, _ = pa.shape
    tq = _tile(ls, A_TQ)
    r64 = tq // A_BLK
    nblk64 = ls // A_BLK

    def main(c):
        return pl.BlockSpec((None, tq, A_OUT), lambda r, i: (r, i, c))

    def prev(c):
        return pl.BlockSpec((None, A_BLK, A_OUT), lambda r, i: (r, jnp.maximum(i * r64 - 1, 0), c))

    def nxt(c):
        return pl.BlockSpec((None, A_BLK, A_OUT),
                            lambda r, i: (r, jnp.minimum((i + 1) * r64, nblk64 - 1), c))

    return pl.pallas_call(
        functools.partial(_attn_a_kernel, dil=dil, tq=tq),
        out_shape=(jax.ShapeDtypeStruct((dil, ls, A_OUT), jnp.float32),
                   jax.ShapeDtypeStruct((dil, ls, LANE), jnp.float32)),
        grid=(dil, ls // tq),
        in_specs=[main(0), prev(1), main(1), nxt(1), prev(2), main(2), nxt(2)],
        out_specs=(pl.BlockSpec((None, tq, A_OUT), lambda r, i: (r, i, 0)),
                   pl.BlockSpec((None, tq, LANE), lambda r, i: (r, i, 0))),
        compiler_params=_params(("parallel", "parallel")),
        name=f"attn_a{g}",
    )(pa, pa, pa, pa, pa, pa, pa)


def _combine_a_kernel(o0_ref, o1_ref, o2_ref, l0_ref, l1_ref, l2_ref, o_ref, *scratch, dils, tr):
    o_in, l_in = [o0_ref, o1_ref, o2_ref], [l0_ref, l1_ref, l2_ref]
    nat = [scratch[g] if dil > 1 else None for g, dil in enumerate(dils)]

    def ungroup(g, src):
        dil = dils[g]
        if dil == 1:
            return src(0)
        for r in range(dil):
            nat[g][pl.ds(r, tr // dil, stride=dil), :] = src(r)
        return nat[g][...]

    l0, l1, l2 = (ungroup(g, lambda r, g=g: l_in[g][r]) for g in range(len(dils)))
    mx = jnp.maximum(jnp.maximum(l0, l1), l2)
    e0, e1, e2 = jnp.exp2(l0 - mx), jnp.exp2(l1 - mx), jnp.exp2(l2 - mx)
    inv = 1.0 / (e0 + e1 + e2)
    w = [e0 * inv, e1 * inv, e2 * inv]
    for h in range(A_HEADS):
        sl = slice(h * HEAD_DIM, (h + 1) * HEAD_DIM)
        o = sum(w[g][:, h:h + 1] * ungroup(g, lambda r, g=g: o_in[g][r, :, sl]) for g in range(len(dils)))
        o_ref[:, sl] = o.astype(o_ref.dtype)


def _attn_a(pas):
    dils = tuple(p.shape[0] for p in pas)
    s = pas[0].shape[0] * pas[0].shape[1]
    outs = [_attn_a_pattern(p, g) for g, p in enumerate(pas)]
    tr = _tile(s, 512)
    ospec = lambda dil: pl.BlockSpec((dil, tr // dil, A_OUT), lambda i: (0, i, 0))
    lspec = lambda dil: pl.BlockSpec((dil, tr // dil, LANE), lambda i: (0, i, 0))
    scratch = [pltpu.VMEM((tr, LANE), jnp.float32) for _ in dils]
    return pl.pallas_call(
        functools.partial(_combine_a_kernel, dils=dils, tr=tr),
        out_shape=jax.ShapeDtypeStruct((s, A_OUT), jnp.bfloat16),
        grid=(s // tr,),
        in_specs=[ospec(d) for d in dils] + [lspec(d) for d in dils],
        out_specs=pl.BlockSpec((tr, A_OUT), lambda i: (i, 0)),
        scratch_shapes=scratch,
        compiler_params=_params(("parallel",)),
        name="combine_a",
    )(*[o for o, _ in outs], *[l for _, l in outs])


B_TQ = 256


def _attn_b_kernel(slope_ref, sink_ref, q_ref, kp_ref, km_ref, kn_ref, vp_ref, vm_ref, vn_ref, o_ref, *, tq):
    c = pl.program_id(0)
    i = pl.program_id(1)
    last = pl.num_programs(1) - 1
    nsub = tq // B_HALF
    nk = 3 * B_HALF
    allc = slice(None)
    probs = [(sub, g) for sub in range(nsub) for g in range(B_GROUP)]
    cols = lambda g: slice(g * HEAD_DIM, (g + 1) * HEAD_DIM)
    slopes = [slope_ref[c * B_GROUP + g] for g in range(B_GROUP)]
    sinks = [sink_ref[c * B_GROUP + g] * LOG2E for g in range(B_GROUP)]
    dists = [_band_distance(i, last, sub, B_HALF, B_HALF, tq) for sub in range(nsub)]
    ks = [_halo_window(kp_ref, km_ref, kn_ref, allc, sub * B_HALF, nk, B_HALF, tq) for sub in range(nsub)]
    scores = []
    for sub, g in probs:
        s = lax.dot_general(q_ref[sub * B_HALF:(sub + 1) * B_HALF, cols(g)], ks[sub],
                            (((1,), (1,)), ((), ())), preferred_element_type=jnp.float32)
        scores.append(s - slopes[g] * dists[sub])
    es, denoms = [], []
    for (sub, g), s in zip(probs, scores):
        m = jnp.maximum(jnp.max(s, axis=-1, keepdims=True), sinks[g])
        e = jnp.exp2(s - m)
        es.append(e.astype(jnp.bfloat16))
        denoms.append(jnp.sum(e, axis=-1, keepdims=True) + jnp.exp2(sinks[g] - m))
    vs = [_halo_window(vp_ref, vm_ref, vn_ref, allc, sub * B_HALF, nk, B_HALF, tq) for sub in range(nsub)]
    outs = [(jnp.dot(e, vs[sub], preferred_element_type=jnp.float32) / denom).astype(o_ref.dtype)
            for (sub, g), e, denom in zip(probs, es, denoms)]
    o_rows = [jnp.concatenate(outs[sub * B_GROUP:(sub + 1) * B_GROUP], axis=1) for sub in range(nsub)]
    o_ref[...] = o_rows[0] if nsub == 1 else jnp.concatenate(o_rows, axis=0)


def _attn_b(qkv, sinks):
    s, nc = qkv.shape
    tq = _tile(s, B_TQ)
    r128 = tq // B_HALF
    nblk = s // B_HALF
    gw = B_GROUP * HEAD_DIM
    cq0 = 0
    ck0 = B_HEADS
    cv0 = ck0 + B_KV_HEADS
    smem = pl.BlockSpec(memory_space=pltpu.SMEM)

    def main(c0):
        return pl.BlockSpec((tq, HEAD_DIM), lambda c, i: (i, c0 + c))

    def prev(c0):
        return pl.BlockSpec((B_HALF, HEAD_DIM), lambda c, i: (jnp.maximum(i * r128 - 1, 0), c0 + c))

    def nxt(c0):
        return pl.BlockSpec((B_HALF, HEAD_DIM),
                            lambda c, i: (jnp.minimum((i + 1) * r128, nblk - 1), c0 + c))

    slopes = jnp.asarray([LOG2E * sl for sl in _alibi_slopes(B_HEADS)], jnp.float32)
    return pl.pallas_call(
        functools.partial(_attn_b_kernel, tq=tq),
        out_shape=jax.ShapeDtypeStruct((s, B_OUT), jnp.bfloat16),
        grid=(B_KV_HEADS, s // tq),
        in_specs=[smem, smem,
                  pl.BlockSpec((tq, gw), lambda c, i: (i, cq0 + c)),
                  prev(ck0), main(ck0), nxt(ck0), prev(cv0), main(cv0), nxt(cv0)],
        out_specs=pl.BlockSpec((tq, gw), lambda c, i: (i, c)),
        compiler_params=_params(("parallel", "parallel")),
        name="attn_b",
    )(slopes, sinks.astype(jnp.float32), qkv, qkv, qkv, qkv, qkv, qkv, qkv)


C_TQ = 512
C_TK = 512
C_AUG = 16
C_SUMROWS = 16
C_UNDERFLOW = 170.0


def _attn_c_kernel(slope_ref, inv_ref, q_ref, k_ref, v_ref, lq1_ref, lk1_ref, lq2_ref, lk2_ref, g_ref, *rest,
                   tq, tk, nkv, lam_init, casts):
    f32, bf16 = jnp.float32, jnp.bfloat16
    ncast = len(casts)
    cast_in, o_ref, cast_out = rest[:ncast], rest[ncast], rest[ncast + 1:2 * ncast + 1]
    vt_ref, qzt_ref, m_ref, acc_ref, s0_ref, s1_ref, mx0_ref, mx1_ref, dbias_ref, kn2_ref = rest[2 * ncast + 1:]
    h = pl.program_id(0)
    i = pl.program_id(1)
    slope = slope_ref[h]
    half = tk // 2
    lane_k = lax.broadcasted_iota(jnp.int32, (tk, 2 * C_QK_DIM), 1)

    step = h * pl.num_programs(1) + i
    for (start, nblk), w_ref, wb_ref in zip(casts, cast_in, cast_out):
        @pl.when((step >= start) & (step < start + nblk))
        def _():
            wb_ref[...] = w_ref[...].astype(wb_ref.dtype)

    @pl.when(i == 0)
    def _():
        def tbody(c, kn2):
            st = pl.multiple_of(c * tk, tk)
            vt = v_ref[pl.ds(st, tk), :].astype(f32).T.astype(bf16)
            ones_row = lax.broadcasted_iota(jnp.int32, (C_SUMROWS, tk), 0) == 0
            vt_ref[c] = jnp.concatenate([vt, ones_row.astype(bf16)], axis=0)
            kk = k_ref[pl.ds(st, tk), :].astype(f32)
            sq = kk * kk
            n_all = jnp.sum(sq, axis=1, keepdims=True)
            n_0 = jnp.sum(jnp.where(lane_k < C_QK_DIM, sq, 0.0), axis=1, keepdims=True)
            return jnp.maximum(kn2, jnp.max(jnp.maximum(n_0, n_all - n_0)))
        kn2_ref[0] = lax.fori_loop(0, nkv, tbody, jnp.float32(0.0))
        if tq == tk:
            rel0 = (lax.broadcasted_iota(jnp.int32, (tk, tq), 0) - lax.broadcasted_iota(jnp.int32, (tk, tq), 1))
            dbias_ref[...] = slope * jnp.abs(rel0).astype(f32)

    qt = q_ref[...].astype(f32).T
    row = lax.broadcasted_iota(jnp.int32, qt.shape, 0)
    qzt_ref[...] = jnp.concatenate([jnp.where(row < C_QK_DIM, qt, 0.0), jnp.where(row >= C_QK_DIM, qt, 0.0)],
                                   axis=1).astype(bf16)
    m_ref[...] = jnp.full(m_ref.shape, NEG, f32)
    acc_ref[...] = jnp.zeros(acc_ref.shape, f32)

    def chunk_at(t):
        j = jlo + t - 1
        return jnp.where(t == 0, jd, j + (j >= jd).astype(jnp.int32))

    def score_stage(t, s_ref, mx_ref):
        j = chunk_at(t)
        dc = (j * tk + half - i * tq).astype(f32)
        coef = jnp.where(j > jd, -slope, slope)
        base = jnp.where(rowq < 3, coef, -coef * (a_q - dc))
        p1 = base.astype(bf16)
        r1 = base - p1.astype(f32)
        p2 = r1.astype(bf16)
        p3 = (r1 - p2.astype(f32)).astype(bf16)
        piece = rowq % 3
        aug_q = jnp.where(rowq < 6, jnp.where(piece == 0, p1, jnp.where(piece == 1, p2, p3)),
                          jnp.zeros_like(p1))
        kc = k_ref[pl.ds(pl.multiple_of(j * tk, tk), tk), :]
        lhs = jnp.concatenate([kc, aug_k], axis=1)
        rhs = jnp.concatenate([qzt_ref[...], aug_q, zpad], axis=0)
        st = jnp.dot(lhs, rhs, preferred_element_type=f32)
        s_ref[...] = st
        mx_ref[...] = jnp.max(st, axis=0, keepdims=True)

    def softmax_stage(t, s_ref, mx_ref):
        m_old = m_ref[...]
        m_new = jnp.maximum(m_old, mx_ref[...])
        alpha = jnp.exp2(m_old - m_new)
        p = jnp.exp2(s_ref[...] - m_new).astype(bf16)
        acc_ref[...] = alpha * acc_ref[...] + jnp.dot(vt_ref[chunk_at(t)], p, preferred_element_type=f32)
        m_ref[...] = m_new

    colk = lax.broadcasted_iota(jnp.int32, (tk, LANE), 1)
    bk = (lax.broadcasted_iota(jnp.int32, (tk, LANE), 0) - half).astype(f32)
    aug_k = jnp.where(colk < 3, bk, jnp.where(colk < 6, 1.0, 0.0)).astype(bf16)
    rowq = lax.broadcasted_iota(jnp.int32, (C_AUG, 2 * tq), 0)
    a_q = (lax.broadcasted_iota(jnp.int32, (C_AUG, 2 * tq), 1) % tq).astype(f32)
    zpad = jnp.zeros((2 * LANE - 2 * C_QK_DIM - C_AUG, 2 * tq), bf16)

    jd = (i * tq) // tk
    kd = k_ref[pl.ds(pl.multiple_of(jd * tk, tk), tk), :]
    sd = jnp.dot(kd, qzt_ref[...], preferred_element_type=f32)
    if tq == tk:
        bias = dbias_ref[...]
    else:
        rel = (lax.broadcasted_iota(jnp.int32, (tk, tq), 0) - lax.broadcasted_iota(jnp.int32, (tk, tq), 1)
               + (jd * tk - i * tq))
        bias = slope * jnp.abs(rel).astype(f32)
    sd = sd - jnp.concatenate([bias, bias], axis=1)
    s0_ref[...] = sd
    mxd = jnp.max(sd, axis=0, keepdims=True)
    mx0_ref[...] = mxd

    sqq = qt * qt
    n_all = jnp.sum(sqq, axis=0, keepdims=True)
    n_0 = jnp.sum(jnp.where(row < C_QK_DIM, sqq, 0.0), axis=0, keepdims=True)
    s_max = jnp.max(jnp.sqrt(jnp.maximum(n_0, n_all - n_0) * kn2_ref[0]))
    reach = (s_max + C_UNDERFLOW - jnp.min(mxd)) * inv_ref[h]
    wnd = jnp.minimum(reach, float(nkv)).astype(jnp.int32) + 1
    jlo = jnp.maximum(jd - wnd, 0)
    jhi = jnp.minimum(jd + wnd, nkv - 1)
    nvis = jhi - jlo + 1

    def pair(u):
        score_stage(2 * u + 1, s1_ref, mx1_ref)
        softmax_stage(2 * u, s0_ref, mx0_ref)
        score_stage(2 * u + 2, s0_ref, mx0_ref)
        softmax_stage(2 * u + 1, s1_ref, mx1_ref)

    def body(w, carry):
        pair(2 * w)
        pair(2 * w + 1)
        return carry

    npair = (nvis - 1) // 2
    lax.fori_loop(0, npair // 2, body, 0)

    @pl.when(npair % 2 == 1)
    def _():
        pair(npair - 1)

    @pl.when(nvis % 2 == 0)
    def _():
        score_stage(nvis - 1, s1_ref, mx1_ref)
        softmax_stage(nvis - 2, s0_ref, mx0_ref)
        softmax_stage(nvis - 1, s1_ref, mx1_ref)

    @pl.when(nvis % 2 == 1)
    def _():
        softmax_stage(nvis - 1, s0_ref, mx0_ref)

    lam = (jnp.exp(jnp.sum(lq1_ref[...] * lk1_ref[...], axis=-1, keepdims=True))
           - jnp.exp(jnp.sum(lq2_ref[...] * lk2_ref[...], axis=-1, keepdims=True)) + lam_init)
    o = acc_ref[:C_V_DIM, :] / acc_ref[C_V_DIM:C_V_DIM + 1, :]
    o = o[:, :tq] - lam * o[:, tq:]
    y = o * lax.rsqrt(jnp.mean(o * o, axis=0, keepdims=True) + EPS)
    o_ref[...] = (y * g_ref[...] * (1.0 - lam_init)).T.astype(o_ref.dtype)


CAST_BLOCK_BYTES = 2 * 1024 * 1024


def _cast_jobs(weights, nsteps):
    jobs, start = [], 0
    for w, layer in weights:
        _, r, c = w.shape
        rows = max(16, _tile(r, max(16, CAST_BLOCK_BYTES // (4 * c))))
        nblk = r // rows
        assert r % rows == 0 and nblk <= nsteps, (w.shape, rows, nsteps)
        if start + nblk > nsteps:
            start = 0
        jobs.append((w, layer, rows, nblk, start))
        start += nblk
    return jobs


def _attn_c(qkv, lq1, lk1, lq2, lk2, norm_g, lam_init, cast_weights=()):
    s, nc = qkv.shape
    tq = _tile(s, C_TQ)
    tk = _tile(s, C_TK)
    nkv = s // tk
    assert tk % tq == 0, (s, tq, tk)
    nq = s // tq
    cq0 = B_COLS // HEAD_DIM
    ck0 = cq0 + C_HEADS
    cv0 = ck0 + C_HEADS
    smem = pl.BlockSpec(memory_space=pltpu.SMEM)
    vec = lambda n: pl.BlockSpec((1, n), lambda h, i: (0, 0))
    slopes_l2 = [LOG2E * sl for sl in _alibi_slopes(C_HEADS)]
    slopes = jnp.asarray(slopes_l2, jnp.float32)
    inv_reach = jnp.asarray([1.0 / (sl * tk) for sl in slopes_l2], jnp.float32)
    jobs = _cast_jobs(cast_weights, C_HEADS * nq)

    def job_block(start, nblk):
        return lambda h, i: jnp.clip(h * nq + i - start, 0, nblk - 1)

    cast_in_specs = [pl.BlockSpec((None, rows, w.shape[2]),
                                  lambda h, i, layer=layer, blk=job_block(start, nblk): (layer, blk(h, i), 0))
                     for w, layer, rows, nblk, start in jobs]
    cast_out_specs = [pl.BlockSpec((None, rows, w.shape[2]),
                                   lambda h, i, blk=job_block(start, nblk): (0, blk(h, i), 0))
                      for w, layer, rows, nblk, start in jobs]
    cast_out_shapes = [jax.ShapeDtypeStruct((1,) + w.shape[1:], jnp.bfloat16) for w, *_ in jobs]
    outs = pl.pallas_call(
        functools.partial(_attn_c_kernel, tq=tq, tk=tk, nkv=nkv, lam_init=lam_init,
                          casts=tuple((start, nblk) for *_, nblk, start in jobs)),
        out_shape=[jax.ShapeDtypeStruct((s, C_OUT), jnp.bfloat16)] + cast_out_shapes,
        grid=(C_HEADS, nq),
        in_specs=[smem, smem,
                  pl.BlockSpec((tq, HEAD_DIM), lambda h, i: (i, cq0 + h)),
                  pl.BlockSpec((s, HEAD_DIM), lambda h, i: (0, ck0 + h)),
                  pl.BlockSpec((s, HEAD_DIM), lambda h, i: (0, cv0 + h)),
                  vec(C_QK_DIM), vec(C_QK_DIM), vec(C_QK_DIM), vec(C_QK_DIM),
                  pl.BlockSpec((C_V_DIM, 1), lambda h, i: (0, 0))] + cast_in_specs,
        out_specs=[pl.BlockSpec((tq, C_V_DIM), lambda h, i: (i, h))] + cast_out_specs,
        scratch_shapes=[pltpu.VMEM((nkv, C_V_DIM + C_SUMROWS, tk), jnp.bfloat16),
                        pltpu.VMEM((2 * C_QK_DIM, 2 * tq), jnp.bfloat16),
                        pltpu.VMEM((1, 2 * tq), jnp.float32),
                        pltpu.VMEM((C_V_DIM + C_SUMROWS, 2 * tq), jnp.float32),
                        pltpu.VMEM((tk, 2 * tq), jnp.float32),
                        pltpu.VMEM((tk, 2 * tq), jnp.float32),
                        pltpu.VMEM((1, 2 * tq), jnp.float32),
                        pltpu.VMEM((1, 2 * tq), jnp.float32),
                        pltpu.VMEM((tk, tq) if tq == tk else (8, LANE), jnp.float32),
                        pltpu.SMEM((1,), jnp.float32)],
        compiler_params=_params(("arbitrary", "arbitrary")),
        name="attn_c",
    )(slopes, inv_reach, qkv, qkv, qkv, lq1.reshape(1, -1), lk1.reshape(1, -1), lq2.reshape(1, -1),
      lk2.reshape(1, -1), norm_g.reshape(-1, 1), *[w for w, *_ in jobs])
    return outs[0], list(outs[1:])


def kernel(x, mix_norm_g, w_in, b_sink, diff_lq1, diff_lk1, diff_lq2, diff_lk2, diff_norm_g,
           w_branch_a, w_branch_b, w_branch_c, w_out, mlp_norm_g, w_up, w_down, final_norm_g):
    bsz, seq, d = x.shape
    depth = w_in.shape[0]
    bf16 = jnp.bfloat16
    w_a_b, w_b_b, w_c_b = (w.astype(bf16) for w in (w_branch_a, w_branch_b, w_branch_c))
    w_in_first = w_in[:1].astype(bf16)
    col = jnp.arange(B_COLS + C_COLS)
    bc_scale = jnp.where(col < B_HEADS * HEAD_DIM, HEAD_DIM ** -0.5 * LOG2E,
                         jnp.where((col >= B_COLS) & (col < B_COLS + C_HEADS * 2 * C_QK_DIM),
                                   C_QK_DIM ** -0.5 * LOG2E, 1.0)).astype(jnp.float32).reshape(1, -1)
    dils = tuple(dil for _, dil in A_PATTERNS)
    pat_cols = 3 * A_OUT
    a_scale = jnp.where(jnp.arange(pat_cols) < A_OUT, HEAD_DIM ** -0.5 * LOG2E,
                        1.0).astype(jnp.float32).reshape(1, -1)
    xs = x.reshape(bsz * seq, d)
    outs = []
    for b in range(bsz):
        h = xs[b * seq:(b + 1) * seq]
        w_in_l = w_in_first
        for l in range(depth):
            u, *u_dil = _rmsnorm_mix(h, mix_norm_g[l], tuple(dl for dl in dils if dl > 1))
            u_by_dil = {1: u, **{dl: ud.reshape(seq, d) for dl, ud in zip([dl for dl in dils if dl > 1], u_dil)}}
            pas = [_matmul(u_by_dil[dl], w_in_l, 0, g * pat_cols, pat_cols, bf16, col_scale=a_scale,
                           name=f"proj_a{g}").reshape(dl, seq // dl, pat_cols)
                   for g, dl in enumerate(dils)]
            pbc = _matmul(u, w_in_l, 0, OFF_B, B_COLS + C_COLS, bf16, col_scale=bc_scale, name="proj_bc")
            gates = _matmul(u, w_in_l, 0, OFF_G, N_BRANCH * d, bf16, epilogue="sigmoid", name="proj_gates")
            o_a = _attn_a(pas)
            o_b = _attn_b(pbc, b_sink[l])
            lam_init = 0.8 - 0.6 * math.exp(-0.3 * l)
            todo = [(w_up, l), (w_down, l), (w_out, l)] + ([(w_in, l + 1)] if l + 1 < depth else [])
            o_c, cast = _attn_c(pbc, diff_lq1[l], diff_lk1[l], diff_lq2[l], diff_lk2[l], diff_norm_g[l],
                                lam_init, cast_weights=todo)
            w_up_l, w_down_l, w_out_l = cast[:3]
            w_in_l = cast[3] if l + 1 < depth else None
            merged = _merge(o_a, o_b, o_c, gates, w_a_b, w_b_b, w_c_b, l)
            h = _matmul(merged, w_out_l, 0, 0, d, jnp.float32, residual=h, name="out_proj")
            h = _mlp(h, mlp_norm_g[l], w_up_l, w_down_l, 0, final_g=final_norm_g if l == depth - 1 else None)
        outs.append(h.astype(x.dtype))
    return jnp.concatenate(outs, axis=0).reshape(bsz, seq, d)
```

```python
import functools
import math

import jax
import jax.numpy as jnp
from jax import lax
from jax.experimental import pallas as pl
from jax.experimental.pallas import tpu as pltpu

HEAD_DIM = 128
A_PATTERNS = ((128, 1), (512, 4), (2048, 16))
A_HEADS = 8
N_PAT = len(A_PATTERNS)
B_HEADS = 12
B_KV_HEADS = 4
B_GROUP = B_HEADS // B_KV_HEADS
B_HALF = 128
C_HEADS = 12
C_QK_DIM = 64
C_V_DIM = 2 * C_QK_DIM
N_BRANCH = 3
EPS = 1e-6
NEG = -1e30

A_COLS = N_PAT * 3 * A_HEADS * HEAD_DIM
B_COLS = (B_HEADS + 2 * B_KV_HEADS) * HEAD_DIM
C_COLS = C_HEADS * (4 * C_QK_DIM + C_V_DIM)
OFF_B = A_COLS
OFF_C = OFF_B + B_COLS
OFF_G = OFF_C + C_COLS
A_OUT = A_HEADS * HEAD_DIM
B_OUT = B_HEADS * HEAD_DIM
C_OUT = C_HEADS * C_V_DIM

LANE = 128
VMEM_LIMIT = 56 * 1024 * 1024


def _tile(n, pref):
    t = pref
    while t > 1 and n % t:
        t //= 2
    return t


def _params(sem):
    return pltpu.CompilerParams(dimension_semantics=sem, vmem_limit_bytes=VMEM_LIMIT)


def _alibi_slopes(n):
    return [2.0 ** (-8.0 * i / n) for i in range(1, n + 1)]


def _rmsnorm_mix_kernel(x_ref, g_ref, o_ref, *grouped_refs, dils, tr):
    x = x_ref[...]
    y = (x * lax.rsqrt(jnp.mean(x * x, axis=-1, keepdims=True) + EPS) * g_ref[...]).astype(o_ref.dtype)
    o_ref[...] = y
    dst = lax.broadcasted_iota(jnp.int32, (tr, tr), 0)
    src = lax.broadcasted_iota(jnp.int32, (tr, tr), 1)
    for dil, s_ref in zip(dils, grouped_refs):
        n = tr // dil
        perm = (src == (dst % n) * dil + dst // n).astype(y.dtype)
        yp = jnp.dot(perm, y, preferred_element_type=jnp.float32).astype(y.dtype)
        for r in range(dil):
            s_ref[r] = yp[r * n:(r + 1) * n]


def _rmsnorm_mix(x, g, dils):
    s, d = x.shape
    tr = _tile(s, 256)
    out_shape = [jax.ShapeDtypeStruct((s, d), jnp.bfloat16)]
    out_specs = [pl.BlockSpec((tr, d), lambda i: (i, 0))]
    for dil in dils:
        out_shape.append(jax.ShapeDtypeStruct((dil, s // dil, d), jnp.bfloat16))
        out_specs.append(pl.BlockSpec((dil, tr // dil, d), lambda i: (0, i, 0)))
    return pl.pallas_call(
        functools.partial(_rmsnorm_mix_kernel, dils=dils, tr=tr),
        out_shape=out_shape,
        grid=(s // tr,),
        in_specs=[pl.BlockSpec((tr, d), lambda i: (i, 0)),
                  pl.BlockSpec((1, d), lambda i: (0, 0))],
        out_specs=out_specs,
        compiler_params=_params(("parallel",)),
        name="rmsnorm_mix",
    )(x, g.reshape(1, d))


def _matmul_kernel(x_ref, w_ref, *rest, epilogue):
    *extra, o_ref = rest
    acc = jnp.dot(x_ref[...], w_ref[...], preferred_element_type=jnp.float32)
    if epilogue == "sigmoid":
        acc = jax.nn.sigmoid(acc)
    elif epilogue == "col_scale":
        acc = acc * extra[0][...]
    elif epilogue == "residual":
        acc = extra[0][...] + acc
    o_ref[...] = acc.astype(o_ref.dtype)


def _matmul(x, w, layer, col_off, ncols, out_dtype, epilogue="none", residual=None, col_scale=None,
            name="matmul"):
    m, k = x.shape
    tm = _tile(m, 1024)
    tn = _tile(math.gcd(ncols, col_off) if col_off else ncols, 1024)
    off = col_off // tn
    in_specs = [pl.BlockSpec((tm, k), lambda i, j: (i, 0)),
                pl.BlockSpec((None, k, tn), lambda i, j: (layer, 0, off + j))]
    args = [x, w]
    if col_scale is not None:
        epilogue = "col_scale"
        in_specs.append(pl.BlockSpec((1, tn), lambda i, j: (0, j)))
        args.append(col_scale)
    elif residual is not None:
        epilogue = "residual"
        in_specs.append(pl.BlockSpec((tm, tn), lambda i, j: (i, j)))
        args.append(residual)
    return pl.pallas_call(
        functools.partial(_matmul_kernel, epilogue=epilogue),
        out_shape=jax.ShapeDtypeStruct((m, ncols), out_dtype),
        grid=(m // tm, ncols // tn),
        in_specs=in_specs,
        out_specs=pl.BlockSpec((tm, tn), lambda i, j: (i, j)),
        compiler_params=_params(("parallel", "parallel")),
        name=name,
    )(*args)


def _merge_kernel(oa_ref, ob_ref, oc_ref, wa_ref, wb_ref, wc_ref, ga_ref, gb_ref, gc_ref, o_ref):
    f32 = jnp.float32
    ya = jnp.dot(oa_ref[...], wa_ref[...], preferred_element_type=f32)
    yb = jnp.dot(ob_ref[...], wb_ref[...], preferred_element_type=f32)
    yc = jnp.dot(oc_ref[...], wc_ref[...], preferred_element_type=f32)
    merged = (ga_ref[...].astype(f32) * ya + gb_ref[...].astype(f32) * yb
              + gc_ref[...].astype(f32) * yc)
    o_ref[...] = merged.astype(o_ref.dtype)


def _merge(o_a, o_b, o_c, gates, w_a, w_b, w_c, layer):
    s = o_a.shape[0]
    d = w_a.shape[-1]
    tm = _tile(s, 1024)
    tn = _tile(d, 1024)
    nj = d // tn
    row = lambda width: pl.BlockSpec((tm, width), lambda i, j: (i, 0))
    wspec = lambda kk: pl.BlockSpec((None, kk, tn), lambda i, j: (layer, 0, j))
    gspec = lambda b: pl.BlockSpec((tm, tn), lambda i, j: (i, b * nj + j))
    return pl.pallas_call(
        _merge_kernel,
        out_shape=jax.ShapeDtypeStruct((s, d), jnp.bfloat16),
        grid=(s // tm, nj),
        in_specs=[row(A_OUT), row(B_OUT), row(C_OUT),
                  wspec(A_OUT), wspec(B_OUT), wspec(C_OUT),
                  gspec(0), gspec(1), gspec(2)],
        out_specs=pl.BlockSpec((tm, tn), lambda i, j: (i, j)),
        compiler_params=_params(("parallel", "parallel")),
        name="merge",
    )(o_a, o_b, o_c, w_a, w_b, w_c, gates, gates, gates)


NORM_ROWS = 64


def _mlp_kernel(h_ref, gin_ref, wu_ref, wd_ref, *rest, final_norm):
    *gout_ref, o_ref, v_ref = rest
    f = pl.program_id(1)

    def norm_rows(src_ref, dst_ref, g_ref):
        def step(c, carry):
            rows = pl.ds(pl.multiple_of(c * NORM_ROWS, NORM_ROWS), NORM_ROWS)
            x = src_ref[rows, :]
            y = x * lax.rsqrt(jnp.mean(x * x, axis=-1, keepdims=True) + EPS) * g_ref[...]
            dst_ref[rows, :] = y.astype(dst_ref.dtype)
            return carry
        lax.fori_loop(0, src_ref.shape[0] // NORM_ROWS, step, 0)

    @pl.when(f == 0)
    def _():
        o_ref[...] = h_ref[...]
        norm_rows(h_ref, v_ref, gin_ref)

    a = jnp.dot(v_ref[...], wu_ref[...], preferred_element_type=jnp.float32)
    a = jnp.square(jnp.maximum(a, 0.0)).astype(jnp.bfloat16)
    o_ref[...] += jnp.dot(a, wd_ref[...], preferred_element_type=jnp.float32)

    if final_norm:
        @pl.when(f == pl.num_programs(1) - 1)
        def _():
            norm_rows(o_ref, o_ref, gout_ref[0])


def _mlp(h, g_in, w_up, w_down, layer, final_g=None):
    s, d = h.shape
    dff = w_up.shape[-1]
    tm = _tile(s, 512)
    tf = _tile(dff, 512)
    vec = pl.BlockSpec((1, d), lambda i, f: (0, 0))
    in_specs = [pl.BlockSpec((tm, d), lambda i, f: (i, 0)),
                vec,
                pl.BlockSpec((None, d, tf), lambda i, f: (layer, 0, f)),
                pl.BlockSpec((None, tf, d), lambda i, f: (layer, f, 0))]
    args = [h, g_in.reshape(1, d), w_up, w_down]
    if final_g is not None:
        in_specs.append(vec)
        args.append(final_g.reshape(1, d))
    return pl.pallas_call(
        functools.partial(_mlp_kernel, final_norm=final_g is not None),
        out_shape=jax.ShapeDtypeStruct((s, d), jnp.float32),
        grid=(s // tm, dff // tf),
        in_specs=in_specs,
        out_specs=pl.BlockSpec((tm, d), lambda i, f: (i, 0)),
        scratch_shapes=[pltpu.VMEM((tm, d), jnp.bfloat16)],
        compiler_params=_params(("parallel", "arbitrary")),
        name="mlp",
    )(*args)


A_TQ = 256
A_SUB = 128
A_BLK = 64
LOG2E = 1.4426950408889634
FAR = 3e32


def _halo_window(p_ref, m_ref, n_ref, cols, r0, nk, halo, tq):
    lo, hi = r0, r0 + nk
    pieces = []
    if lo < halo:
        pieces.append(p_ref[lo:min(hi, halo), cols])
    a, b = max(lo, halo), min(hi, halo + tq)
    if a < b:
        pieces.append(m_ref[a - halo:b - halo, cols])
    a = max(lo, halo + tq)
    if a < hi:
        pieces.append(n_ref[a - halo - tq:hi - halo - tq, cols])
    return pieces[0] if len(pieces) == 1 else jnp.concatenate(pieces, axis=0)


def _band_distance(i, last, sub, nsub_rows, halo, tq):
    nk = nsub_rows + 2 * halo
    a = lax.broadcasted_iota(jnp.int32, (nsub_rows, nk), 0)
    c = lax.broadcasted_iota(jnp.int32, (nsub_rows, nk), 1)
    dist = jnp.abs(c - halo - a)
    row = c + sub * nsub_rows
    valid = (dist <= halo) & ((row >= halo) | (i > 0)) & ((row < tq + halo) | (i < last))
    return jnp.where(valid, dist.astype(jnp.float32), FAR)


def _attn_a_kernel(q_ref, kp_ref, km_ref, kn_ref, vp_ref, vm_ref, vn_ref, o_ref, lse_ref, *, dil, tq):
    i = pl.program_id(1)
    last = pl.num_programs(1) - 1
    nsub = tq // A_SUB if tq >= A_SUB else 1
    rows = tq // nsub
    nk = rows + 2 * A_BLK
    lane = lax.broadcasted_iota(jnp.int32, (rows, LANE), 1)
    slopes = [LOG2E * dil * sl for sl in _alibi_slopes(A_HEADS)]
    probs = [(sub, h) for sub in range(nsub) for h in range(A_HEADS)]
    cols = lambda h: slice(h * HEAD_DIM, (h + 1) * HEAD_DIM)
    dists = [_band_distance(i, last, sub, rows, A_BLK, tq) for sub in range(nsub)]
    scores = []
    for sub, h in probs:
        k = _halo_window(kp_ref, km_ref, kn_ref, cols(h), sub * rows, nk, A_BLK, tq)
        s = lax.dot_general(q_ref[sub * rows:(sub + 1) * rows, cols(h)], k, (((1,), (1,)), ((), ())),
                            preferred_element_type=jnp.float32)
        scores.append(s - slopes[h] * dists[sub])
    es, denoms = [], []
    lse_tiles = [jnp.zeros((rows, LANE), jnp.float32) for _ in range(nsub)]
    for (sub, h), s in zip(probs, scores):
        m = jnp.max(s, axis=-1, keepdims=True)
        e = jnp.exp2(s - m)
        denom = jnp.sum(e, axis=-1, keepdims=True)
        es.append(e.astype(jnp.bfloat16))
        denoms.append(denom)
        lse_tiles[sub] = jnp.where(lane == h, m + jnp.log2(denom), lse_tiles[sub])
    outs = []
    for (sub, h), e, denom in zip(probs, es, denoms):
        v = _halo_window(vp_ref, vm_ref, vn_ref, cols(h), sub * rows, nk, A_BLK, tq)
        outs.append(jnp.dot(e, v, preferred_element_type=jnp.float32) / denom)
    o_rows = [jnp.concatenate(outs[sub * A_HEADS:(sub + 1) * A_HEADS], axis=1) for sub in range(nsub)]
    o_ref[...] = o_rows[0] if nsub == 1 else jnp.concatenate(o_rows, axis=0)
    lse_ref[...] = lse_tiles[0] if nsub == 1 else jnp.concatenate(lse_tiles, axis=0)


def _attn_a_pattern(pa, g):
    dil, ls, _ = pa.shape
    tq = _tile(ls, A_TQ)
    r64 = tq // A_BLK
    nblk64 = ls // A_BLK

    def main(c):
        return pl.BlockSpec((None, tq, A_OUT), lambda r, i: (r, i, c))

    def prev(c):
        return pl.BlockSpec((None, A_BLK, A_OUT), lambda r, i: (r, jnp.maximum(i * r64 - 1, 0), c))

    def nxt(c):
        return pl.BlockSpec((None, A_BLK, A_OUT),
                            lambda r, i: (r, jnp.minimum((i + 1) * r64, nblk64 - 1), c))

    return pl.pallas_call(
        functools.partial(_attn_a_kernel, dil=dil, tq=tq),
        out_shape=(jax.ShapeDtypeStruct((dil, ls, A_OUT), jnp.float32),
                   jax.ShapeDtypeStruct((dil, ls, LANE), jnp.float32)),
        grid=(dil, ls // tq),
        in_specs=[main(0), prev(1), main(1), nxt(1), prev(2), main(2), nxt(2)],
        out_specs=(pl.BlockSpec((None, tq, A_OUT), lambda r, i: (r, i, 0)),
                   pl.BlockSpec((None, tq, LANE), lambda r, i: (r, i, 0))),
        compiler_params=_params(("parallel", "parallel")),
        name=f"attn_a{g}",
    )(pa, pa, pa, pa, pa, pa, pa)


def _combine_a_kernel(o0_ref, o1_ref, o2_ref, l0_ref, l1_ref, l2_ref, o_ref, *scratch, dils, tr):
    o_in, l_in = [o0_ref, o1_ref, o2_ref], [l0_ref, l1_ref, l2_ref]
    nat = [scratch[g] if dil > 1 else None for g, dil in enumerate(dils)]

    def ungroup(g, src):
        dil = dils[g]
        if dil == 1:
            return src(0)
        for r in range(dil):
            nat[g][pl.ds(r, tr // dil, stride=dil), :] = src(r)
        return nat[g][...]

    l0, l1, l2 = (ungroup(g, lambda r, g=g: l_in[g][r]) for g in range(len(dils)))
    mx = jnp.maximum(jnp.maximum(l0, l1), l2)
    e0, e1, e2 = jnp.exp2(l0 - mx), jnp.exp2(l1 - mx), jnp.exp2(l2 - mx)
    inv = 1.0 / (e0 + e1 + e2)
    w = [e0 * inv, e1 * inv, e2 * inv]
    for h in range(A_HEADS):
        sl = slice(h * HEAD_DIM, (h + 1) * HEAD_DIM)
        o = sum(w[g][:, h:h + 1] * ungroup(g, lambda r, g=g: o_in[g][r, :, sl]) for g in range(len(dils)))
        o_ref[:, sl] = o.astype(o_ref.dtype)


def _attn_a(pas):
    dils = tuple(p.shape[0] for p in pas)
    s = pas[0].shape[0] * pas[0].shape[1]
    outs = [_attn_a_pattern(p, g) for g, p in enumerate(pas)]
    tr = _tile(s, 512)
    ospec = lambda dil: pl.BlockSpec((dil, tr // dil, A_OUT), lambda i: (0, i, 0))
    lspec = lambda dil: pl.BlockSpec((dil, tr // dil, LANE), lambda i: (0, i, 0))
    scratch = [pltpu.VMEM((tr, LANE), jnp.float32) for _ in dils]
    return pl.pallas_call(
        functools.partial(_combine_a_kernel, dils=dils, tr=tr),
        out_shape=jax.ShapeDtypeStruct((s, A_OUT), jnp.bfloat16),
        grid=(s // tr,),
        in_specs=[ospec(d) for d in dils] + [lspec(d) for d in dils],
        out_specs=pl.BlockSpec((tr, A_OUT), lambda i: (i, 0)),
        scratch_shapes=scratch,
        compiler_params=_params(("parallel",)),
        name="combine_a",
    )(*[o for o, _ in outs], *[l for _, l in outs])


B_TQ = 256


def _attn_b_kernel(slope_ref, sink_ref, q_ref, kp_ref, km_ref, kn_ref, vp_ref, vm_ref, vn_ref, o_ref, *, tq):
    c = pl.program_id(0)
    i = pl.program_id(1)
    last = pl.num_programs(1) - 1
    nsub = tq // B_HALF
    nk = 3 * B_HALF
    allc = slice(None)
    probs = [(sub, g) for sub in range(nsub) for g in range(B_GROUP)]
    cols = lambda g: slice(g * HEAD_DIM, (g + 1) * HEAD_DIM)
    slopes = [slope_ref[c * B_GROUP + g] for g in range(B_GROUP)]
    sinks = [sink_ref[c * B_GROUP + g] * LOG2E for g in range(B_GROUP)]
    dists = [_band_distance(i, last, sub, B_HALF, B_HALF, tq) for sub in range(nsub)]
    ks = [_halo_window(kp_ref, km_ref, kn_ref, allc, sub * B_HALF, nk, B_HALF, tq) for sub in range(nsub)]
    scores = []
    for sub, g in probs:
        s = lax.dot_general(q_ref[sub * B_HALF:(sub + 1) * B_HALF, cols(g)], ks[sub],
                            (((1,), (1,)), ((), ())), preferred_element_type=jnp.float32)
        scores.append(s - slopes[g] * dists[sub])
    es, denoms = [], []
    for (sub, g), s in zip(probs, scores):
        m = jnp.maximum(jnp.max(s, axis=-1, keepdims=True), sinks[g])
        e = jnp.exp2(s - m)
        es.append(e.astype(jnp.bfloat16))
        denoms.append(jnp.sum(e, axis=-1, keepdims=True) + jnp.exp2(sinks[g] - m))
    vs = [_halo_window(vp_ref, vm_ref, vn_ref, allc, sub * B_HALF, nk, B_HALF, tq) for sub in range(nsub)]
    outs = [(jnp.dot(e, vs[sub], preferred_element_type=jnp.float32) / denom).astype(o_ref.dtype)
            for (sub, g), e, denom in zip(probs, es, denoms)]
    o_rows = [jnp.concatenate(outs[sub * B_GROUP:(sub + 1) * B_GROUP], axis=1) for sub in range(nsub)]
    o_ref[...] = o_rows[0] if nsub == 1 else jnp.concatenate(o_rows, axis=0)


def _attn_b(qkv, sinks):
    s, nc = qkv.shape
    tq = _tile(s, B_TQ)
    r128 = tq // B_HALF
    nblk = s // B_HALF
    gw = B_GROUP * HEAD_DIM
    cq0 = 0
    ck0 = B_HEADS
    cv0 = ck0 + B_KV_HEADS
    smem = pl.BlockSpec(memory_space=pltpu.SMEM)

    def main(c0):
        return pl.BlockSpec((tq, HEAD_DIM), lambda c, i: (i, c0 + c))

    def prev(c0):
        return pl.BlockSpec((B_HALF, HEAD_DIM), lambda c, i: (jnp.maximum(i * r128 - 1, 0), c0 + c))

    def nxt(c0):
        return pl.BlockSpec((B_HALF, HEAD_DIM),
                            lambda c, i: (jnp.minimum((i + 1) * r128, nblk - 1), c0 + c))

    slopes = jnp.asarray([LOG2E * sl for sl in _alibi_slopes(B_HEADS)], jnp.float32)
    return pl.pallas_call(
        functools.partial(_attn_b_kernel, tq=tq),
        out_shape=jax.ShapeDtypeStruct((s, B_OUT), jnp.bfloat16),
        grid=(B_KV_HEADS, s // tq),
        in_specs=[smem, smem,
                  pl.BlockSpec((tq, gw), lambda c, i: (i, cq0 + c)),
                  prev(ck0), main(ck0), nxt(ck0), prev(cv0), main(cv0), nxt(cv0)],
        out_specs=pl.BlockSpec((tq, gw), lambda c, i: (i, c)),
        compiler_params=_params(("parallel", "parallel")),
        name="attn_b",
    )(slopes, sinks.astype(jnp.float32), qkv, qkv, qkv, qkv, qkv, qkv, qkv)


C_TQ = 512
C_TK = 512
C_AUG = 16
C_SUMROWS = 16
C_UNDERFLOW = 152.0
C_UNROLL = 4


def _attn_c_kernel(slope_ref, inv_ref, q_ref, k_ref, v_ref, lq1_ref, lk1_ref, lq2_ref, lk2_ref, g_ref, *rest,
                   tq, tk, nkv, lam_init, casts):
    f32, bf16 = jnp.float32, jnp.bfloat16
    ncast = len(casts)
    cast_in, o_ref, cast_out = rest[:ncast], rest[ncast], rest[ncast + 1:2 * ncast + 1]
    vt_ref, qzt_ref, m_ref, acc_ref, s0_ref, s1_ref, mx0_ref, mx1_ref, dbias_ref, kn2_ref = rest[2 * ncast + 1:]
    h = pl.program_id(0)
    i = pl.program_id(1)
    slope = slope_ref[h]
    half = tk // 2
    lane_k = lax.broadcasted_iota(jnp.int32, (tk, 2 * C_QK_DIM), 1)

    step = h * pl.num_programs(1) + i
    for (start, nblk), w_ref, wb_ref in zip(casts, cast_in, cast_out):
        @pl.when((step >= start) & (step < start + nblk))
        def _():
            wb_ref[...] = w_ref[...].astype(wb_ref.dtype)

    @pl.when(i == 0)
    def _():
        def tbody(c, kn2):
            st = pl.multiple_of(c * tk, tk)
            vt = v_ref[pl.ds(st, tk), :].astype(f32).T.astype(bf16)
            ones_row = lax.broadcasted_iota(jnp.int32, (C_SUMROWS, tk), 0) == 0
            vt_ref[c] = jnp.concatenate([vt, ones_row.astype(bf16)], axis=0)
            kk = k_ref[pl.ds(st, tk), :].astype(f32)
            sq = kk * kk
            n_all = jnp.sum(sq, axis=1, keepdims=True)
            n_0 = jnp.sum(jnp.where(lane_k < C_QK_DIM, sq, 0.0), axis=1, keepdims=True)
            return jnp.maximum(kn2, jnp.max(jnp.maximum(n_0, n_all - n_0)))
        kn2_ref[0] = lax.fori_loop(0, nkv, tbody, jnp.float32(0.0))
        if tq == tk:
            rel0 = (lax.broadcasted_iota(jnp.int32, (tk, tq), 0) - lax.broadcasted_iota(jnp.int32, (tk, tq), 1))
            dbias_ref[...] = slope * jnp.abs(rel0).astype(f32)

    qt = q_ref[...].astype(f32).T
    row = lax.broadcasted_iota(jnp.int32, qt.shape, 0)
    qzt_ref[...] = jnp.concatenate([jnp.where(row < C_QK_DIM, qt, 0.0), jnp.where(row >= C_QK_DIM, qt, 0.0)],
                                   axis=1).astype(bf16)
    m_ref[...] = jnp.full(m_ref.shape, NEG, f32)
    acc_ref[...] = jnp.zeros(acc_ref.shape, f32)

    def chunk_at(t):
        j = jlo + t - 1
        return jnp.where(t == 0, jd, j + (j >= jd).astype(jnp.int32))

    def score_stage(t, s_ref, mx_ref):
        j = chunk_at(t)
        dc = (j * tk + half - i * tq).astype(f32)
        coef = jnp.where(j > jd, -slope, slope)
        base = jnp.where(rowq < 3, coef, -coef * (a_q - dc))
        p1 = base.astype(bf16)
        r1 = base - p1.astype(f32)
        p2 = r1.astype(bf16)
        p3 = (r1 - p2.astype(f32)).astype(bf16)
        piece = rowq % 3
        aug_q = jnp.where(rowq < 6, jnp.where(piece == 0, p1, jnp.where(piece == 1, p2, p3)),
                          jnp.zeros_like(p1))
        kc = k_ref[pl.ds(pl.multiple_of(j * tk, tk), tk), :]
        lhs = jnp.concatenate([kc, aug_k], axis=1)
        rhs = jnp.concatenate([qzt_ref[...], aug_q, zpad], axis=0)
        st = jnp.dot(lhs, rhs, preferred_element_type=f32)
        s_ref[...] = st
        mx_ref[...] = jnp.max(st, axis=0, keepdims=True)

    def softmax_stage(t, s_ref, mx_ref):
        m_old = m_ref[...]
        m_new = jnp.maximum(m_old, mx_ref[...])
        alpha = jnp.exp2(m_old - m_new)
        p = jnp.exp2(s_ref[...] - m_new).astype(bf16)
        acc_ref[...] = alpha * acc_ref[...] + jnp.dot(vt_ref[chunk_at(t)], p, preferred_element_type=f32)
        m_ref[...] = m_new

    colk = lax.broadcasted_iota(jnp.int32, (tk, LANE), 1)
    bk = (lax.broadcasted_iota(jnp.int32, (tk, LANE), 0) - half).astype(f32)
    aug_k = jnp.where(colk < 3, bk, jnp.where(colk < 6, 1.0, 0.0)).astype(bf16)
    rowq = lax.broadcasted_iota(jnp.int32, (C_AUG, 2 * tq), 0)
    a_q = (lax.broadcasted_iota(jnp.int32, (C_AUG, 2 * tq), 1) % tq).astype(f32)
    zpad = jnp.zeros((2 * LANE - 2 * C_QK_DIM - C_AUG, 2 * tq), bf16)

    jd = (i * tq) // tk
    kd = k_ref[pl.ds(pl.multiple_of(jd * tk, tk), tk), :]
    sd = jnp.dot(kd, qzt_ref[...], preferred_element_type=f32)
    if tq == tk:
        bias = dbias_ref[...]
    else:
        rel = (lax.broadcasted_iota(jnp.int32, (tk, tq), 0) - lax.broadcasted_iota(jnp.int32, (tk, tq), 1)
               + (jd * tk - i * tq))
        bias = slope * jnp.abs(rel).astype(f32)
    sd = sd - jnp.concatenate([bias, bias], axis=1)
    s0_ref[...] = sd
    mxd = jnp.max(sd, axis=0, keepdims=True)
    mx0_ref[...] = mxd

    sqq = qt * qt
    n_all = jnp.sum(sqq, axis=0, keepdims=True)
    n_0 = jnp.sum(jnp.where(row < C_QK_DIM, sqq, 0.0), axis=0, keepdims=True)
    s_max = jnp.max(jnp.sqrt(jnp.maximum(n_0, n_all - n_0) * kn2_ref[0]))
    reach = (s_max + C_UNDERFLOW - jnp.min(mxd)) * inv_ref[h]
    wnd = jnp.minimum(reach, float(nkv)).astype(jnp.int32) + 1
    jlo = jnp.maximum(jd - wnd, 0)
    jhi = jnp.minimum(jd + wnd, nkv - 1)
    nvis = jhi - jlo + 1

    def pair(u):
        score_stage(2 * u + 1, s1_ref, mx1_ref)
        softmax_stage(2 * u, s0_ref, mx0_ref)
        score_stage(2 * u + 2, s0_ref, mx0_ref)
        softmax_stage(2 * u + 1, s1_ref, mx1_ref)

    def body(w, carry):
        for u in range(C_UNROLL):
            pair(C_UNROLL * w + u)
        return carry

    npair = (nvis - 1) // 2
    nbody = npair // C_UNROLL
    lax.fori_loop(0, nbody, body, 0)
    done = nbody * C_UNROLL
    width = C_UNROLL // 2
    while width:
        take = ((npair - done) // width) % 2 == 1

        @pl.when(take)
        def _(done=done, width=width):
            for u in range(width):
                pair(done + u)

        done = done + take.astype(jnp.int32) * width
        width //= 2

    @pl.when(nvis % 2 == 0)
    def _():
        score_stage(nvis - 1, s1_ref, mx1_ref)
        softmax_stage(nvis - 2, s0_ref, mx0_ref)
        softmax_stage(nvis - 1, s1_ref, mx1_ref)

    @pl.when(nvis % 2 == 1)
    def _():
        softmax_stage(nvis - 1, s0_ref, mx0_ref)

    lam = (jnp.exp(jnp.sum(lq1_ref[...] * lk1_ref[...], axis=-1, keepdims=True))
           - jnp.exp(jnp.sum(lq2_ref[...] * lk2_ref[...], axis=-1, keepdims=True)) + lam_init)
    o = acc_ref[:C_V_DIM, :] / acc_ref[C_V_DIM:C_V_DIM + 1, :]
    o = o[:, :tq] - lam * o[:, tq:]
    y = o * lax.rsqrt(jnp.mean(o * o, axis=0, keepdims=True) + EPS)
    o_ref[...] = (y * g_ref[...] * (1.0 - lam_init)).T.astype(o_ref.dtype)


CAST_BLOCK_BYTES = 2 * 1024 * 1024


def _cast_jobs(weights, nsteps):
    jobs, start = [], 0
    for w, layer in weights:
        _, r, c = w.shape
        rows = max(16, _tile(r, max(16, CAST_BLOCK_BYTES // (4 * c))))
        nblk = r // rows
        assert r % rows == 0 and nblk <= nsteps, (w.shape, rows, nsteps)
        if start + nblk > nsteps:
            start = 0
        jobs.append((w, layer, rows, nblk, start))
        start += nblk
    return jobs


def _attn_c(qkv, lq1, lk1, lq2, lk2, norm_g, lam_init, cast_weights=()):
    s, nc = qkv.shape
    tq = _tile(s, C_TQ)
    tk = _tile(s, C_TK)
    nkv = s // tk
    assert tk % tq == 0, (s, tq, tk)
    nq = s // tq
    cq0 = B_COLS // HEAD_DIM
    ck0 = cq0 + C_HEADS
    cv0 = ck0 + C_HEADS
    smem = pl.BlockSpec(memory_space=pltpu.SMEM)
    vec = lambda n: pl.BlockSpec((1, n), lambda h, i: (0, 0))
    slopes_l2 = [LOG2E * sl for sl in _alibi_slopes(C_HEADS)]
    slopes = jnp.asarray(slopes_l2, jnp.float32)
    inv_reach = jnp.asarray([1.0 / (sl * tk) for sl in slopes_l2], jnp.float32)
    jobs = _cast_jobs(cast_weights, C_HEADS * nq)

    def job_block(start, nblk):
        return lambda h, i: jnp.clip(h * nq + i - start, 0, nblk - 1)

    cast_in_specs = [pl.BlockSpec((None, rows, w.shape[2]),
                                  lambda h, i, layer=layer, blk=job_block(start, nblk): (layer, blk(h, i), 0))
                     for w, layer, rows, nblk, start in jobs]
    cast_out_specs = [pl.BlockSpec((None, rows, w.shape[2]),
                                   lambda h, i, blk=job_block(start, nblk): (0, blk(h, i), 0))
                      for w, layer, rows, nblk, start in jobs]
    cast_out_shapes = [jax.ShapeDtypeStruct((1,) + w.shape[1:], jnp.bfloat16) for w, *_ in jobs]
    outs = pl.pallas_call(
        functools.partial(_attn_c_kernel, tq=tq, tk=tk, nkv=nkv, lam_init=lam_init,
                          casts=tuple((start, nblk) for *_, nblk, start in jobs)),
        out_shape=[jax.ShapeDtypeStruct((s, C_OUT), jnp.bfloat16)] + cast_out_shapes,
        grid=(C_HEADS, nq),
        in_specs=[smem, smem,
                  pl.BlockSpec((tq, HEAD_DIM), lambda h, i: (i, cq0 + h)),
                  pl.BlockSpec((s, HEAD_DIM), lambda h, i: (0, ck0 + h)),
                  pl.BlockSpec((s, HEAD_DIM), lambda h, i: (0, cv0 + h)),
                  vec(C_QK_DIM), vec(C_QK_DIM), vec(C_QK_DIM), vec(C_QK_DIM),
                  pl.BlockSpec((C_V_DIM, 1), lambda h, i: (0, 0))] + cast_in_specs,
        out_specs=[pl.BlockSpec((tq, C_V_DIM), lambda h, i: (i, h))] + cast_out_specs,
        scratch_shapes=[pltpu.VMEM((nkv, C_V_DIM + C_SUMROWS, tk), jnp.bfloat16),
                        pltpu.VMEM((2 * C_QK_DIM, 2 * tq), jnp.bfloat16),
                        pltpu.VMEM((1, 2 * tq), jnp.float32),
                        pltpu.VMEM((C_V_DIM + C_SUMROWS, 2 * tq), jnp.float32),
                        pltpu.VMEM((tk, 2 * tq), jnp.float32),
                        pltpu.VMEM((tk, 2 * tq), jnp.float32),
                        pltpu.VMEM((1, 2 * tq), jnp.float32),
                        pltpu.VMEM((1, 2 * tq), jnp.float32),
                        pltpu.VMEM((tk, tq) if tq == tk else (8, LANE), jnp.float32),
                        pltpu.SMEM((1,), jnp.float32)],
        compiler_params=_params(("arbitrary", "arbitrary")),
        name="attn_c",
    )(slopes, inv_reach, qkv, qkv, qkv, lq1.reshape(1, -1), lk1.reshape(1, -1), lq2.reshape(1, -1),
      lk2.reshape(1, -1), norm_g.reshape(-1, 1), *[w for w, *_ in jobs])
    return outs[0], list(outs[1:])


def kernel(x, mix_norm_g, w_in, b_sink, diff_lq1, diff_lk1, diff_lq2, diff_lk2, diff_norm_g,
           w_branch_a, w_branch_b, w_branch_c, w_out, mlp_norm_g, w_up, w_down, final_norm_g):
    bsz, seq, d = x.shape
    depth = w_in.shape[0]
    bf16 = jnp.bfloat16
    w_a_b, w_b_b, w_c_b = (w.astype(bf16) for w in (w_branch_a, w_branch_b, w_branch_c))
    w_in_first = w_in[:1].astype(bf16)
    col = jnp.arange(B_COLS + C_COLS)
    bc_scale = jnp.where(col < B_HEADS * HEAD_DIM, HEAD_DIM ** -0.5 * LOG2E,
                         jnp.where((col >= B_COLS) & (col < B_COLS + C_HEADS * 2 * C_QK_DIM),
                                   C_QK_DIM ** -0.5 * LOG2E, 1.0)).astype(jnp.float32).reshape(1, -1)
    dils = tuple(dil for _, dil in A_PATTERNS)
    pat_cols = 3 * A_OUT
    a_scale = jnp.where(jnp.arange(pat_cols) < A_OUT, HEAD_DIM ** -0.5 * LOG2E,
                        1.0).astype(jnp.float32).reshape(1, -1)
    xs = x.reshape(bsz * seq, d)
    outs = []
    for b in range(bsz):
        h = xs[b * seq:(b + 1) * seq]
        w_in_l = w_in_first
        for l in range(depth):
            u, *u_dil = _rmsnorm_mix(h, mix_norm_g[l], tuple(dl for dl in dils if dl > 1))
            u_by_dil = {1: u, **{dl: ud.reshape(seq, d) for dl, ud in zip([dl for dl in dils if dl > 1], u_dil)}}
            pas = [_matmul(u_by_dil[dl], w_in_l, 0, g * pat_cols, pat_cols, bf16, col_scale=a_scale,
                           name=f"proj_a{g}").reshape(dl, seq // dl, pat_cols)
                   for g, dl in enumerate(dils)]
            pbc = _matmul(u, w_in_l, 0, OFF_B, B_COLS + C_COLS, bf16, col_scale=bc_scale, name="proj_bc")
            gates = _matmul(u, w_in_l, 0, OFF_G, N_BRANCH * d, bf16, epilogue="sigmoid", name="proj_gates")
            o_a = _attn_a(pas)
            o_b = _attn_b(pbc, b_sink[l])
            lam_init = 0.8 - 0.6 * math.exp(-0.3 * l)
            todo = [(w_up, l), (w_down, l), (w_out, l)] + ([(w_in, l + 1)] if l + 1 < depth else [])
            o_c, cast = _attn_c(pbc, diff_lq1[l], diff_lk1[l], diff_lq2[l], diff_lk2[l], diff_norm_g[l],
                                lam_init, cast_weights=todo)
            w_up_l, w_down_l, w_out_l = cast[:3]
            w_in_l = cast[3] if l + 1 < depth else None
            merged = _merge(o_a, o_b, o_c, gates, w_a_b, w_b_b, w_c_b, l)
            h = _matmul(merged, w_out_l, 0, 0, d, jnp.float32, residual=h, name="out_proj")
            h = _mlp(h, mlp_norm_g[l], w_up_l, w_down_l, 0, final_g=final_norm_g if l == depth - 1 else None)
        outs.append(h.astype(x.dtype))
    return jnp.concatenate(outs, axis=0).reshape(bsz, seq, d)
```

```python
import functools
import math

import jax
import jax.numpy as jnp
from jax import lax
from jax.experimental import pallas as pl
from jax.experimental.pallas import tpu as pltpu

HEAD_DIM = 128
A_PATTERNS = ((128, 1), (512, 4), (2048, 16))
A_HEADS = 8
N_PAT = len(A_PATTERNS)
B_HEADS = 12
B_KV_HEADS = 4
B_GROUP = B_HEADS // B_KV_HEADS
B_HALF = 128
C_HEADS = 12
C_QK_DIM = 64
C_V_DIM = 2 * C_QK_DIM
N_BRANCH = 3
EPS = 1e-6
NEG = -1e30

A_COLS = N_PAT * 3 * A_HEADS * HEAD_DIM
B_COLS = (B_HEADS + 2 * B_KV_HEADS) * HEAD_DIM
C_COLS = C_HEADS * (4 * C_QK_DIM + C_V_DIM)
OFF_B = A_COLS
OFF_C = OFF_B + B_COLS
OFF_G = OFF_C + C_COLS
A_OUT = A_HEADS * HEAD_DIM
B_OUT = B_HEADS * HEAD_DIM
C_OUT = C_HEADS * C_V_DIM

LANE = 128
VMEM_LIMIT = 56 * 1024 * 1024


def _tile(n, pref):
    t = pref
    while t > 1 and n % t:
        t //= 2
    return t


def _params(sem):
    return pltpu.CompilerParams(dimension_semantics=sem, vmem_limit_bytes=VMEM_LIMIT)


def _alibi_slopes(n):
    return [2.0 ** (-8.0 * i / n) for i in range(1, n + 1)]


def _rmsnorm_mix_kernel(x_ref, g_ref, o_ref, *grouped_refs, dils, tr):
    x = x_ref[...]
    y = (x * lax.rsqrt(jnp.mean(x * x, axis=-1, keepdims=True) + EPS) * g_ref[...]).astype(o_ref.dtype)
    o_ref[...] = y
    dst = lax.broadcasted_iota(jnp.int32, (tr, tr), 0)
    src = lax.broadcasted_iota(jnp.int32, (tr, tr), 1)
    for dil, s_ref in zip(dils, grouped_refs):
        n = tr // dil
        perm = (src == (dst % n) * dil + dst // n).astype(y.dtype)
        yp = jnp.dot(perm, y, preferred_element_type=jnp.float32).astype(y.dtype)
        for r in range(dil):
            s_ref[r] = yp[r * n:(r + 1) * n]


def _rmsnorm_mix(x, g, dils):
    s, d = x.shape
    tr = _tile(s, 256)
    out_shape = [jax.ShapeDtypeStruct((s, d), jnp.bfloat16)]
    out_specs = [pl.BlockSpec((tr, d), lambda i: (i, 0))]
    for dil in dils:
        out_shape.append(jax.ShapeDtypeStruct((dil, s // dil, d), jnp.bfloat16))
        out_specs.append(pl.BlockSpec((dil, tr // dil, d), lambda i: (0, i, 0)))
    return pl.pallas_call(
        functools.partial(_rmsnorm_mix_kernel, dils=dils, tr=tr),
        out_shape=out_shape,
        grid=(s // tr,),
        in_specs=[pl.BlockSpec((tr, d), lambda i: (i, 0)),
                  pl.BlockSpec((1, d), lambda i: (0, 0))],
        out_specs=out_specs,
        compiler_params=_params(("parallel",)),
        name="rmsnorm_mix",
    )(x, g.reshape(1, d))


def _matmul_kernel(x_ref, w_ref, *rest, epilogue):
    *extra, o_ref = rest
    acc = jnp.dot(x_ref[...], w_ref[...], preferred_element_type=jnp.float32)
    if epilogue == "sigmoid":
        acc = jax.nn.sigmoid(acc)
    elif epilogue == "col_scale":
        acc = acc * extra[0][...]
    elif epilogue == "residual":
        acc = extra[0][...] + acc
    o_ref[...] = acc.astype(o_ref.dtype)


def _matmul(x, w, layer, col_off, ncols, out_dtype, epilogue="none", residual=None, col_scale=None,
            name="matmul"):
    m, k = x.shape
    tm = _tile(m, 1024)
    tn = _tile(math.gcd(ncols, col_off) if col_off else ncols, 1024)
    off = col_off // tn
    in_specs = [pl.BlockSpec((tm, k), lambda i, j: (i, 0)),
                pl.BlockSpec((None, k, tn), lambda i, j: (layer, 0, off + j))]
    args = [x, w]
    if col_scale is not None:
        epilogue = "col_scale"
        in_specs.append(pl.BlockSpec((1, tn), lambda i, j: (0, j)))
        args.append(col_scale)
    elif residual is not None:
        epilogue = "residual"
        in_specs.append(pl.BlockSpec((tm, tn), lambda i, j: (i, j)))
        args.append(residual)
    return pl.pallas_call(
        functools.partial(_matmul_kernel, epilogue=epilogue),
        out_shape=jax.ShapeDtypeStruct((m, ncols), out_dtype),
        grid=(m // tm, ncols // tn),
        in_specs=in_specs,
        out_specs=pl.BlockSpec((tm, tn), lambda i, j: (i, j)),
        compiler_params=_params(("parallel", "parallel")),
        name=name,
    )(*args)


def _merge_kernel(oa_ref, ob_ref, oc_ref, wa_ref, wb_ref, wc_ref, ga_ref, gb_ref, gc_ref, o_ref):
    f32 = jnp.float32
    ya = jnp.dot(oa_ref[...], wa_ref[...], preferred_element_type=f32)
    yb = jnp.dot(ob_ref[...], wb_ref[...], preferred_element_type=f32)
    yc = jnp.dot(oc_ref[...], wc_ref[...], preferred_element_type=f32)
    merged = (ga_ref[...].astype(f32) * ya + gb_ref[...].astype(f32) * yb
              + gc_ref[...].astype(f32) * yc)
    o_ref[...] = merged.astype(o_ref.dtype)


def _merge(o_a, o_b, o_c, gates, w_a, w_b, w_c, layer):
    s = o_a.shape[0]
    d = w_a.shape[-1]
    tm = _tile(s, 1024)
    tn = _tile(d, 1024)
    nj = d // tn
    row = lambda width: pl.BlockSpec((tm, width), lambda i, j: (i, 0))
    wspec = lambda kk: pl.BlockSpec((None, kk, tn), lambda i, j: (layer, 0, j))
    gspec = lambda b: pl.BlockSpec((tm, tn), lambda i, j: (i, b * nj + j))
    return pl.pallas_call(
        _merge_kernel,
        out_shape=jax.ShapeDtypeStruct((s, d), jnp.bfloat16),
        grid=(s // tm, nj),
        in_specs=[row(A_OUT), row(B_OUT), row(C_OUT),
                  wspec(A_OUT), wspec(B_OUT), wspec(C_OUT),
                  gspec(0), gspec(1), gspec(2)],
        out_specs=pl.BlockSpec((tm, tn), lambda i, j: (i, j)),
        compiler_params=_params(("parallel", "parallel")),
        name="merge",
    )(o_a, o_b, o_c, w_a, w_b, w_c, gates, gates, gates)


NORM_ROWS = 64


def _mlp_kernel(h_ref, gin_ref, wu_ref, wd_ref, *rest, final_norm):
    *gout_ref, o_ref, v_ref = rest
    f = pl.program_id(1)

    def norm_rows(src_ref, dst_ref, g_ref):
        def step(c, carry):
            rows = pl.ds(pl.multiple_of(c * NORM_ROWS, NORM_ROWS), NORM_ROWS)
            x = src_ref[rows, :]
            y = x * lax.rsqrt(jnp.mean(x * x, axis=-1, keepdims=True) + EPS) * g_ref[...]
            dst_ref[rows, :] = y.astype(dst_ref.dtype)
            return carry
        lax.fori_loop(0, src_ref.shape[0] // NORM_ROWS, step, 0)

    @pl.when(f == 0)
    def _():
        o_ref[...] = h_ref[...]
        norm_rows(h_ref, v_ref, gin_ref)

    a = jnp.dot(v_ref[...], wu_ref[...], preferred_element_type=jnp.float32)
    a = jnp.square(jnp.maximum(a, 0.0)).astype(jnp.bfloat16)
    o_ref[...] += jnp.dot(a, wd_ref[...], preferred_element_type=jnp.float32)

    if final_norm:
        @pl.when(f == pl.num_programs(1) - 1)
        def _():
            norm_rows(o_ref, o_ref, gout_ref[0])


def _mlp(h, g_in, w_up, w_down, layer, final_g=None):
    s, d = h.shape
    dff = w_up.shape[-1]
    tm = _tile(s, 512)
    tf = _tile(dff, 512)
    vec = pl.BlockSpec((1, d), lambda i, f: (0, 0))
    in_specs = [pl.BlockSpec((tm, d), lambda i, f: (i, 0)),
                vec,
                pl.BlockSpec((None, d, tf), lambda i, f: (layer, 0, f)),
                pl.BlockSpec((None, tf, d), lambda i, f: (layer, f, 0))]
    args = [h, g_in.reshape(1, d), w_up, w_down]
    if final_g is not None:
        in_specs.append(vec)
        args.append(final_g.reshape(1, d))
    return pl.pallas_call(
        functools.partial(_mlp_kernel, final_norm=final_g is not None),
        out_shape=jax.ShapeDtypeStruct((s, d), jnp.float32),
        grid=(s // tm, dff // tf),
        in_specs=in_specs,
        out_specs=pl.BlockSpec((tm, d), lambda i, f: (i, 0)),
        scratch_shapes=[pltpu.VMEM((tm, d), jnp.bfloat16)],
        compiler_params=_params(("parallel", "arbitrary")),
        name="mlp",
    )(*args)


A_TQ = 512
A_SUB = 128
A_BLK = 64
LOG2E = 1.4426950408889634
FAR = 3e32


def _halo_window(p_ref, m_ref, n_ref, cols, r0, nk, halo, tq):
    lo, hi = r0, r0 + nk
    pieces = []
    if lo < halo:
        pieces.append(p_ref[lo:min(hi, halo), cols])
    a, b = max(lo, halo), min(hi, halo + tq)
    if a < b:
        pieces.append(m_ref[a - halo:b - halo, cols])
    a = max(lo, halo + tq)
    if a < hi:
        pieces.append(n_ref[a - halo - tq:hi - halo - tq, cols])
    return pieces[0] if len(pieces) == 1 else jnp.concatenate(pieces, axis=0)


def _band_distance(i, last, sub, nsub_rows, halo, tq):
    nk = nsub_rows + 2 * halo
    a = lax.broadcasted_iota(jnp.int32, (nsub_rows, nk), 0)
    c = lax.broadcasted_iota(jnp.int32, (nsub_rows, nk), 1)
    dist = jnp.abs(c - halo - a)
    row = c + sub * nsub_rows
    valid = (dist <= halo) & ((row >= halo) | (i > 0)) & ((row < tq + halo) | (i < last))
    return jnp.where(valid, dist.astype(jnp.float32), FAR)


def _attn_a_kernel(q_ref, kp_ref, km_ref, kn_ref, vp_ref, vm_ref, vn_ref, o_ref, lse_ref, *, dil, tq):
    i = pl.program_id(1)
    last = pl.num_programs(1) - 1
    nsub = tq // A_SUB if tq >= A_SUB else 1
    rows = tq // nsub
    nk = rows + 2 * A_BLK
    lane = lax.broadcasted_iota(jnp.int32, (rows, LANE), 1)
    slopes = [LOG2E * dil * sl for sl in _alibi_slopes(A_HEADS)]
    probs = [(sub, h) for sub in range(nsub) for h in range(A_HEADS)]
    cols = lambda h: slice(h * HEAD_DIM, (h + 1) * HEAD_DIM)
    dists = [_band_distance(i, last, sub, rows, A_BLK, tq) for sub in range(nsub)]
    scores = []
    for sub, h in probs:
        k = _halo_window(kp_ref, km_ref, kn_ref, cols(h), sub * rows, nk, A_BLK, tq)
        s = lax.dot_general(q_ref[sub * rows:(sub + 1) * rows, cols(h)], k, (((1,), (1,)), ((), ())),
                            preferred_element_type=jnp.float32)
        scores.append(s - slopes[h] * dists[sub])
    es, denoms = [], []
    lse_tiles = [jnp.zeros((rows, LANE), jnp.float32) for _ in range(nsub)]
    for (sub, h), s in zip(probs, scores):
        m = jnp.max(s, axis=-1, keepdims=True)
        e = jnp.exp2(s - m)
        denom = jnp.sum(e, axis=-1, keepdims=True)
        es.append(e.astype(jnp.bfloat16))
        denoms.append(denom)
        lse_tiles[sub] = jnp.where(lane == h, m + jnp.log2(denom), lse_tiles[sub])
    outs = []
    for (sub, h), e, denom in zip(probs, es, denoms):
        v = _halo_window(vp_ref, vm_ref, vn_ref, cols(h), sub * rows, nk, A_BLK, tq)
        outs.append(jnp.dot(e, v, preferred_element_type=jnp.float32) / denom)
    o_rows = [jnp.concatenate(outs[sub * A_HEADS:(sub + 1) * A_HEADS], axis=1) for sub in range(nsub)]
    o_ref[...] = o_rows[0] if nsub == 1 else jnp.concatenate(o_rows, axis=0)
    lse_ref[...] = lse_tiles[0] if nsub == 1 else jnp.concatenate(lse_tiles, axis=0)


def _attn_a_pattern(pa, g):
    dil, ls, _ = pa.shape
    tq = _tile(ls, A_TQ)
    r64 = tq // A_BLK
    nblk64 = ls // A_BLK

    def main(c):
        return pl.BlockSpec((None, tq, A_OUT), lambda r, i: (r, i, c))

    def prev(c):
        return pl.BlockSpec((None, A_BLK, A_OUT), lambda r, i: (r, jnp.maximum(i * r64 - 1, 0), c))

    def nxt(c):
        return pl.BlockSpec((None, A_BLK, A_OUT),
                            lambda r, i: (r, jnp.minimum((i + 1) * r64, nblk64 - 1), c))

    return pl.pallas_call(
        functools.partial(_attn_a_kernel, dil=dil, tq=tq),
        out_shape=(jax.ShapeDtypeStruct((dil, ls, A_OUT), jnp.float32),
                   jax.ShapeDtypeStruct((dil, ls, LANE), jnp.float32)),
        grid=(dil, ls // tq),
        in_specs=[main(0), prev(1), main(1), nxt(1), prev(2), main(2), nxt(2)],
        out_specs=(pl.BlockSpec((None, tq, A_OUT), lambda r, i: (r, i, 0)),
                   pl.BlockSpec((None, tq, LANE), lambda r, i: (r, i, 0))),
        compiler_params=_params(("parallel", "parallel")),
        name=f"attn_a{g}",
    )(pa, pa, pa, pa, pa, pa, pa)


def _combine_a_kernel(o0_ref, o1_ref, o2_ref, l0_ref, l1_ref, l2_ref, o_ref, *scratch, dils, tr):
    o_in, l_in = [o0_ref, o1_ref, o2_ref], [l0_ref, l1_ref, l2_ref]
    nat = [scratch[g] if dil > 1 else None for g, dil in enumerate(dils)]

    def ungroup(g, src):
        dil = dils[g]
        if dil == 1:
            return src(0)
        for r in range(dil):
            nat[g][pl.ds(r, tr // dil, stride=dil), :] = src(r)
        return nat[g][...]

    l0, l1, l2 = (ungroup(g, lambda r, g=g: l_in[g][r]) for g in range(len(dils)))
    mx = jnp.maximum(jnp.maximum(l0, l1), l2)
    e0, e1, e2 = jnp.exp2(l0 - mx), jnp.exp2(l1 - mx), jnp.exp2(l2 - mx)
    inv = 1.0 / (e0 + e1 + e2)
    w = [e0 * inv, e1 * inv, e2 * inv]
    for h in range(A_HEADS):
        sl = slice(h * HEAD_DIM, (h + 1) * HEAD_DIM)
        o = sum(w[g][:, h:h + 1] * ungroup(g, lambda r, g=g: o_in[g][r, :, sl]) for g in range(len(dils)))
        o_ref[:, sl] = o.astype(o_ref.dtype)


def _attn_a(pas):
    dils = tuple(p.shape[0] for p in pas)
    s = pas[0].shape[0] * pas[0].shape[1]
    outs = [_attn_a_pattern(p, g) for g, p in enumerate(pas)]
    tr = _tile(s, 512)
    ospec = lambda dil: pl.BlockSpec((dil, tr // dil, A_OUT), lambda i: (0, i, 0))
    lspec = lambda dil: pl.BlockSpec((dil, tr // dil, LANE), lambda i: (0, i, 0))
    scratch = [pltpu.VMEM((tr, LANE), jnp.float32) for _ in dils]
    return pl.pallas_call(
        functools.partial(_combine_a_kernel, dils=dils, tr=tr),
        out_shape=jax.ShapeDtypeStruct((s, A_OUT), jnp.bfloat16),
        grid=(s // tr,),
        in_specs=[ospec(d) for d in dils] + [lspec(d) for d in dils],
        out_specs=pl.BlockSpec((tr, A_OUT), lambda i: (i, 0)),
        scratch_shapes=scratch,
        compiler_params=_params(("parallel",)),
        name="combine_a",
    )(*[o for o, _ in outs], *[l for _, l in outs])


B_TQ = 512


def _attn_b_kernel(slope_ref, sink_ref, q_ref, kp_ref, km_ref, kn_ref, vp_ref, vm_ref, vn_ref, o_ref, *, tq):
    c = pl.program_id(0)
    i = pl.program_id(1)
    last = pl.num_programs(1) - 1
    nsub = tq // B_HALF
    nk = 3 * B_HALF
    allc = slice(None)
    probs = [(sub, g) for sub in range(nsub) for g in range(B_GROUP)]
    cols = lambda g: slice(g * HEAD_DIM, (g + 1) * HEAD_DIM)
    slopes = [slope_ref[c * B_GROUP + g] for g in range(B_GROUP)]
    sinks = [sink_ref[c * B_GROUP + g] * LOG2E for g in range(B_GROUP)]
    dists = [_band_distance(i, last, sub, B_HALF, B_HALF, tq) for sub in range(nsub)]
    ks = [_halo_window(kp_ref, km_ref, kn_ref, allc, sub * B_HALF, nk, B_HALF, tq) for sub in range(nsub)]
    scores = []
    for sub, g in probs:
        s = lax.dot_general(q_ref[sub * B_HALF:(sub + 1) * B_HALF, cols(g)], ks[sub],
                            (((1,), (1,)), ((), ())), preferred_element_type=jnp.float32)
        scores.append(s - slopes[g] * dists[sub])
    es, denoms = [], []
    for (sub, g), s in zip(probs, scores):
        m = jnp.maximum(jnp.max(s, axis=-1, keepdims=True), sinks[g])
        e = jnp.exp2(s - m)
        es.append(e.astype(jnp.bfloat16))
        denoms.append(jnp.sum(e, axis=-1, keepdims=True) + jnp.exp2(sinks[g] - m))
    vs = [_halo_window(vp_ref, vm_ref, vn_ref, allc, sub * B_HALF, nk, B_HALF, tq) for sub in range(nsub)]
    outs = [(jnp.dot(e, vs[sub], preferred_element_type=jnp.float32) / denom).astype(o_ref.dtype)
            for (sub, g), e, denom in zip(probs, es, denoms)]
    o_rows = [jnp.concatenate(outs[sub * B_GROUP:(sub + 1) * B_GROUP], axis=1) for sub in range(nsub)]
    o_ref[...] = o_rows[0] if nsub == 1 else jnp.concatenate(o_rows, axis=0)


def _attn_b(qkv, sinks):
    s, nc = qkv.shape
    tq = _tile(s, B_TQ)
    r128 = tq // B_HALF
    nblk = s // B_HALF
    gw = B_GROUP * HEAD_DIM
    cq0 = 0
    ck0 = B_HEADS
    cv0 = ck0 + B_KV_HEADS
    smem = pl.BlockSpec(memory_space=pltpu.SMEM)

    def main(c0):
        return pl.BlockSpec((tq, HEAD_DIM), lambda c, i: (i, c0 + c))

    def prev(c0):
        return pl.BlockSpec((B_HALF, HEAD_DIM), lambda c, i: (jnp.maximum(i * r128 - 1, 0), c0 + c))

    def nxt(c0):
        return pl.BlockSpec((B_HALF, HEAD_DIM),
                            lambda c, i: (jnp.minimum((i + 1) * r128, nblk - 1), c0 + c))

    slopes = jnp.asarray([LOG2E * sl for sl in _alibi_slopes(B_HEADS)], jnp.float32)
    return pl.pallas_call(
        functools.partial(_attn_b_kernel, tq=tq),
        out_shape=jax.ShapeDtypeStruct((s, B_OUT), jnp.bfloat16),
        grid=(B_KV_HEADS, s // tq),
        in_specs=[smem, smem,
                  pl.BlockSpec((tq, gw), lambda c, i: (i, cq0 + c)),
                  prev(ck0), main(ck0), nxt(ck0), prev(cv0), main(cv0), nxt(cv0)],
        out_specs=pl.BlockSpec((tq, gw), lambda c, i: (i, c)),
        compiler_params=_params(("parallel", "parallel")),
        name="attn_b",
    )(slopes, sinks.astype(jnp.float32), qkv, qkv, qkv, qkv, qkv, qkv, qkv)


C_TQ = 512
C_TK = 512
C_AUG = 16
C_SUMROWS = 16
C_UNDERFLOW = 152.0
C_UNROLL = 4


def _attn_c_kernel(slope_ref, inv_ref, q_ref, k_ref, v_ref, lq1_ref, lk1_ref, lq2_ref, lk2_ref, g_ref, *rest,
                   tq, tk, nkv, lam_init, casts):
    f32, bf16 = jnp.float32, jnp.bfloat16
    ncast = len(casts)
    cast_in, o_ref, cast_out = rest[:ncast], rest[ncast], rest[ncast + 1:2 * ncast + 1]
    vt_ref, qzt_ref, m_ref, acc_ref, s0_ref, s1_ref, mx0_ref, mx1_ref, dbias_ref, kn2_ref = rest[2 * ncast + 1:]
    h = pl.program_id(0)
    i = pl.program_id(1)
    slope = slope_ref[h]
    half = tk // 2
    lane_k = lax.broadcasted_iota(jnp.int32, (tk, 2 * C_QK_DIM), 1)

    step = h * pl.num_programs(1) + i
    for (start, nblk), w_ref, wb_ref in zip(casts, cast_in, cast_out):
        @pl.when((step >= start) & (step < start + nblk))
        def _():
            wb_ref[...] = w_ref[...].astype(wb_ref.dtype)

    @pl.when(i == 0)
    def _():
        def tbody(c, kn2):
            st = pl.multiple_of(c * tk, tk)
            vt = v_ref[pl.ds(st, tk), :].astype(f32).T.astype(bf16)
            ones_row = lax.broadcasted_iota(jnp.int32, (C_SUMROWS, tk), 0) == 0
            vt_ref[c] = jnp.concatenate([vt, ones_row.astype(bf16)], axis=0)
            kk = k_ref[pl.ds(st, tk), :].astype(f32)
            sq = kk * kk
            n_all = jnp.sum(sq, axis=1, keepdims=True)
            n_0 = jnp.sum(jnp.where(lane_k < C_QK_DIM, sq, 0.0), axis=1, keepdims=True)
            return jnp.maximum(kn2, jnp.max(jnp.maximum(n_0, n_all - n_0)))
        kn2_ref[0] = lax.fori_loop(0, nkv, tbody, jnp.float32(0.0))
        if tq == tk:
            rel0 = (lax.broadcasted_iota(jnp.int32, (tk, tq), 0) - lax.broadcasted_iota(jnp.int32, (tk, tq), 1))
            dbias_ref[...] = slope * jnp.abs(rel0).astype(f32)

    qt = q_ref[...].astype(f32).T
    row = lax.broadcasted_iota(jnp.int32, qt.shape, 0)
    qzt_ref[...] = jnp.concatenate([jnp.where(row < C_QK_DIM, qt, 0.0), jnp.where(row >= C_QK_DIM, qt, 0.0)],
                                   axis=1).astype(bf16)
    m_ref[...] = jnp.full(m_ref.shape, NEG, f32)
    acc_ref[...] = jnp.zeros(acc_ref.shape, f32)

    def chunk_at(t):
        j = jlo + t - 1
        return jnp.where(t == 0, jd, j + (j >= jd).astype(jnp.int32))

    def score_stage(t, s_ref, mx_ref):
        j = chunk_at(t)
        dc = (j * tk + half - i * tq).astype(f32)
        coef = jnp.where(j > jd, -slope, slope)
        base = jnp.where(rowq < 3, coef, -coef * (a_q - dc))
        p1 = base.astype(bf16)
        r1 = base - p1.astype(f32)
        p2 = r1.astype(bf16)
        p3 = (r1 - p2.astype(f32)).astype(bf16)
        piece = rowq % 3
        aug_q = jnp.where(rowq < 6, jnp.where(piece == 0, p1, jnp.where(piece == 1, p2, p3)),
                          jnp.zeros_like(p1))
        kc = k_ref[pl.ds(pl.multiple_of(j * tk, tk), tk), :]
        lhs = jnp.concatenate([kc, aug_k], axis=1)
        rhs = jnp.concatenate([qzt_ref[...], aug_q, zpad], axis=0)
        st = jnp.dot(lhs, rhs, preferred_element_type=f32)
        s_ref[...] = st
        mx_ref[...] = jnp.max(st, axis=0, keepdims=True)

    def softmax_stage(t, s_ref, mx_ref):
        m_old = m_ref[...]
        m_new = jnp.maximum(m_old, mx_ref[...])
        alpha = jnp.exp2(m_old - m_new)
        p = jnp.exp2(s_ref[...] - m_new).astype(bf16)
        acc_ref[...] = alpha * acc_ref[...] + jnp.dot(vt_ref[chunk_at(t)], p, preferred_element_type=f32)
        m_ref[...] = m_new

    colk = lax.broadcasted_iota(jnp.int32, (tk, LANE), 1)
    bk = (lax.broadcasted_iota(jnp.int32, (tk, LANE), 0) - half).astype(f32)
    aug_k = jnp.where(colk < 3, bk, jnp.where(colk < 6, 1.0, 0.0)).astype(bf16)
    rowq = lax.broadcasted_iota(jnp.int32, (C_AUG, 2 * tq), 0)
    a_q = (lax.broadcasted_iota(jnp.int32, (C_AUG, 2 * tq), 1) % tq).astype(f32)
    zpad = jnp.zeros((2 * LANE - 2 * C_QK_DIM - C_AUG, 2 * tq), bf16)

    jd = (i * tq) // tk
    kd = k_ref[pl.ds(pl.multiple_of(jd * tk, tk), tk), :]
    sd = jnp.dot(kd, qzt_ref[...], preferred_element_type=f32)
    if tq == tk:
        bias = dbias_ref[...]
    else:
        rel = (lax.broadcasted_iota(jnp.int32, (tk, tq), 0) - lax.broadcasted_iota(jnp.int32, (tk, tq), 1)
               + (jd * tk - i * tq))
        bias = slope * jnp.abs(rel).astype(f32)
    sd = sd - jnp.concatenate([bias, bias], axis=1)
    s0_ref[...] = sd
    mxd = jnp.max(sd, axis=0, keepdims=True)
    mx0_ref[...] = mxd

    sqq = qt * qt
    n_all = jnp.sum(sqq, axis=0, keepdims=True)
    n_0 = jnp.sum(jnp.where(row < C_QK_DIM, sqq, 0.0), axis=0, keepdims=True)
    s_max = jnp.max(jnp.sqrt(jnp.maximum(n_0, n_all - n_0) * kn2_ref[0]))
    reach = (s_max + C_UNDERFLOW - jnp.min(mxd)) * inv_ref[h]
    wnd = jnp.minimum(reach, float(nkv)).astype(jnp.int32) + 1
    jlo = jnp.maximum(jd - wnd, 0)
    jhi = jnp.minimum(jd + wnd, nkv - 1)
    nvis = jhi - jlo + 1

    def pair(u):
        score_stage(2 * u + 1, s1_ref, mx1_ref)
        softmax_stage(2 * u, s0_ref, mx0_ref)
        score_stage(2 * u + 2, s0_ref, mx0_ref)
        softmax_stage(2 * u + 1, s1_ref, mx1_ref)

    def body(w, carry):
        for u in range(C_UNROLL):
            pair(C_UNROLL * w + u)
        return carry

    npair = (nvis - 1) // 2
    nbody = npair // C_UNROLL
    lax.fori_loop(0, nbody, body, 0)
    done = nbody * C_UNROLL
    width = C_UNROLL // 2
    while width:
        take = ((npair - done) // width) % 2 == 1

        @pl.when(take)
        def _(done=done, width=width):
            for u in range(width):
                pair(done + u)

        done = done + take.astype(jnp.int32) * width
        width //= 2

    @pl.when(nvis % 2 == 0)
    def _():
        score_stage(nvis - 1, s1_ref, mx1_ref)
        softmax_stage(nvis - 2, s0_ref, mx0_ref)
        softmax_stage(nvis - 1, s1_ref, mx1_ref)

    @pl.when(nvis % 2 == 1)
    def _():
        softmax_stage(nvis - 1, s0_ref, mx0_ref)

    lam = (jnp.exp(jnp.sum(lq1_ref[...] * lk1_ref[...], axis=-1, keepdims=True))
           - jnp.exp(jnp.sum(lq2_ref[...] * lk2_ref[...], axis=-1, keepdims=True)) + lam_init)
    o = acc_ref[:C_V_DIM, :] / acc_ref[C_V_DIM:C_V_DIM + 1, :]
    o = o[:, :tq] - lam * o[:, tq:]
    y = o * lax.rsqrt(jnp.mean(o * o, axis=0, keepdims=True) + EPS)
    o_ref[...] = (y * g_ref[...] * (1.0 - lam_init)).T.astype(o_ref.dtype)


CAST_BLOCK_BYTES = 2 * 1024 * 1024


def _cast_jobs(weights, nsteps):
    jobs, start = [], 0
    for w, layer in weights:
        _, r, c = w.shape
        rows = max(16, _tile(r, max(16, CAST_BLOCK_BYTES // (4 * c))))
        nblk = r // rows
        assert r % rows == 0 and nblk <= nsteps, (w.shape, rows, nsteps)
        if start + nblk > nsteps:
            start = 0
        jobs.append((w, layer, rows, nblk, start))
        start += nblk
    return jobs


def _attn_c(qkv, lq1, lk1, lq2, lk2, norm_g, lam_init, cast_weights=()):
    s, nc = qkv.shape
    tq = _tile(s, C_TQ)
    tk = _tile(s, C_TK)
    nkv = s // tk
    assert tk % tq == 0, (s, tq, tk)
    nq = s // tq
    cq0 = B_COLS // HEAD_DIM
    ck0 = cq0 + C_HEADS
    cv0 = ck0 + C_HEADS
    smem = pl.BlockSpec(memory_space=pltpu.SMEM)
    vec = lambda n: pl.BlockSpec((1, n), lambda h, i: (0, 0))
    slopes_l2 = [LOG2E * sl for sl in _alibi_slopes(C_HEADS)]
    slopes = jnp.asarray(slopes_l2, jnp.float32)
    inv_reach = jnp.asarray([1.0 / (sl * tk) for sl in slopes_l2], jnp.float32)
    jobs = _cast_jobs(cast_weights, C_HEADS * nq)

    def job_block(start, nblk):
        return lambda h, i: jnp.clip(h * nq + i - start, 0, nblk - 1)

    cast_in_specs = [pl.BlockSpec((None, rows, w.shape[2]),
                                  lambda h, i, layer=layer, blk=job_block(start, nblk): (layer, blk(h, i), 0))
                     for w, layer, rows, nblk, start in jobs]
    cast_out_specs = [pl.BlockSpec((None, rows, w.shape[2]),
                                   lambda h, i, blk=job_block(start, nblk): (0, blk(h, i), 0))
                      for w, layer, rows, nblk, start in jobs]
    cast_out_shapes = [jax.ShapeDtypeStruct((1,) + w.shape[1:], jnp.bfloat16) for w, *_ in jobs]
    outs = pl.pallas_call(
        functools.partial(_attn_c_kernel, tq=tq, tk=tk, nkv=nkv, lam_init=lam_init,
                          casts=tuple((start, nblk) for *_, nblk, start in jobs)),
        out_shape=[jax.ShapeDtypeStruct((s, C_OUT), jnp.bfloat16)] + cast_out_shapes,
        grid=(C_HEADS, nq),
        in_specs=[smem, smem,
                  pl.BlockSpec((tq, HEAD_DIM), lambda h, i: (i, cq0 + h)),
                  pl.BlockSpec((s, HEAD_DIM), lambda h, i: (0, ck0 + h)),
                  pl.BlockSpec((s, HEAD_DIM), lambda h, i: (0, cv0 + h)),
                  vec(C_QK_DIM), vec(C_QK_DIM), vec(C_QK_DIM), vec(C_QK_DIM),
                  pl.BlockSpec((C_V_DIM, 1), lambda h, i: (0, 0))] + cast_in_specs,
        out_specs=[pl.BlockSpec((tq, C_V_DIM), lambda h, i: (i, h))] + cast_out_specs,
        scratch_shapes=[pltpu.VMEM((nkv, C_V_DIM + C_SUMROWS, tk), jnp.bfloat16),
                        pltpu.VMEM((2 * C_QK_DIM, 2 * tq), jnp.bfloat16),
                        pltpu.VMEM((1, 2 * tq), jnp.float32),
                        pltpu.VMEM((C_V_DIM + C_SUMROWS, 2 * tq), jnp.float32),
                        pltpu.VMEM((tk, 2 * tq), jnp.float32),
                        pltpu.VMEM((tk, 2 * tq), jnp.float32),
                        pltpu.VMEM((1, 2 * tq), jnp.float32),
                        pltpu.VMEM((1, 2 * tq), jnp.float32),
                        pltpu.VMEM((tk, tq) if tq == tk else (8, LANE), jnp.float32),
                        pltpu.SMEM((1,), jnp.float32)],
        compiler_params=_params(("arbitrary", "arbitrary")),
        name="attn_c",
    )(slopes, inv_reach, qkv, qkv, qkv, lq1.reshape(1, -1), lk1.reshape(1, -1), lq2.reshape(1, -1),
      lk2.reshape(1, -1), norm_g.reshape(-1, 1), *[w for w, *_ in jobs])
    return outs[0], list(outs[1:])


def kernel(x, mix_norm_g, w_in, b_sink, diff_lq1, diff_lk1, diff_lq2, diff_lk2, diff_norm_g,
           w_branch_a, w_branch_b, w_branch_c, w_out, mlp_norm_g, w_up, w_down, final_norm_g):
    bsz, seq, d = x.shape
    depth = w_in.shape[0]
    bf16 = jnp.bfloat16
    w_a_b, w_b_b, w_c_b = (w.astype(bf16) for w in (w_branch_a, w_branch_b, w_branch_c))
    w_in_first = w_in[:1].astype(bf16)
    col = jnp.arange(B_COLS + C_COLS)
    bc_scale = jnp.where(col < B_HEADS * HEAD_DIM, HEAD_DIM ** -0.5 * LOG2E,
                         jnp.where((col >= B_COLS) & (col < B_COLS + C_HEADS * 2 * C_QK_DIM),
                                   C_QK_DIM ** -0.5 * LOG2E, 1.0)).astype(jnp.float32).reshape(1, -1)
    dils = tuple(dil for _, dil in A_PATTERNS)
    pat_cols = 3 * A_OUT
    a_scale = jnp.where(jnp.arange(pat_cols) < A_OUT, HEAD_DIM ** -0.5 * LOG2E,
                        1.0).astype(jnp.float32).reshape(1, -1)
    xs = x.reshape(bsz * seq, d)
    outs = []
    for b in range(bsz):
        h = xs[b * seq:(b + 1) * seq]
        w_in_l = w_in_first
        for l in range(depth):
            u, *u_dil = _rmsnorm_mix(h, mix_norm_g[l], tuple(dl for dl in dils if dl > 1))
            u_by_dil = {1: u, **{dl: ud.reshape(seq, d) for dl, ud in zip([dl for dl in dils if dl > 1], u_dil)}}
            pas = [_matmul(u_by_dil[dl], w_in_l, 0, g * pat_cols, pat_cols, bf16, col_scale=a_scale,
                           name=f"proj_a{g}").reshape(dl, seq // dl, pat_cols)
                   for g, dl in enumerate(dils)]
            pbc = _matmul(u, w_in_l, 0, OFF_B, B_COLS + C_COLS, bf16, col_scale=bc_scale, name="proj_bc")
            gates = _matmul(u, w_in_l, 0, OFF_G, N_BRANCH * d, bf16, epilogue="sigmoid", name="proj_gates")
            o_a = _attn_a(pas)
            o_b = _attn_b(pbc, b_sink[l])
            lam_init = 0.8 - 0.6 * math.exp(-0.3 * l)
            todo = [(w_up, l), (w_down, l), (w_out, l)] + ([(w_in, l + 1)] if l + 1 < depth else [])
            o_c, cast = _attn_c(pbc, diff_lq1[l], diff_lk1[l], diff_lq2[l], diff_lk2[l], diff_norm_g[l],
                                lam_init, cast_weights=todo)
            w_up_l, w_down_l, w_out_l = cast[:3]
            w_in_l = cast[3] if l + 1 < depth else None
            merged = _merge(o_a, o_b, o_c, gates, w_a_b, w_b_b, w_c_b, l)
            h = _matmul(merged, w_out_l, 0, 0, d, jnp.float32, residual=h, name="out_proj")
            h = _mlp(h, mlp_norm_g[l], w_up_l, w_down_l, 0, final_g=final_norm_g if l == depth - 1 else None)
        outs.append(h.astype(x.dtype))
    return jnp.concatenate(outs, axis=0).reshape(bsz, seq, d)
```

```python
import functools
import math

import jax
import jax.numpy as jnp
from jax import lax
from jax.experimental import pallas as pl
from jax.experimental.pallas import tpu as pltpu

HEAD_DIM = 128
A_PATTERNS = ((128, 1), (512, 4), (2048, 16))
A_HEADS = 8
N_PAT = len(A_PATTERNS)
B_HEADS = 12
B_KV_HEADS = 4
B_GROUP = B_HEADS // B_KV_HEADS
B_HALF = 128
C_HEADS = 12
C_QK_DIM = 64
C_V_DIM = 2 * C_QK_DIM
N_BRANCH = 3
EPS = 1e-6
NEG = -1e30

A_COLS = N_PAT * 3 * A_HEADS * HEAD_DIM
B_COLS = (B_HEADS + 2 * B_KV_HEADS) * HEAD_DIM
C_COLS = C_HEADS * (4 * C_QK_DIM + C_V_DIM)
OFF_B = A_COLS
OFF_C = OFF_B + B_COLS
OFF_G = OFF_C + C_COLS
A_OUT = A_HEADS * HEAD_DIM
B_OUT = B_HEADS * HEAD_DIM
C_OUT = C_HEADS * C_V_DIM

LANE = 128
VMEM_LIMIT = 56 * 1024 * 1024


def _tile(n, pref):
    t = pref
    while t > 1 and n % t:
        t //= 2
    return t


def _params(sem):
    return pltpu.CompilerParams(dimension_semantics=sem, vmem_limit_bytes=VMEM_LIMIT)


def _alibi_slopes(n):
    return [2.0 ** (-8.0 * i / n) for i in range(1, n + 1)]


def _rmsnorm_mix_kernel(x_ref, g_ref, o_ref, *grouped_refs, dils, tr):
    x = x_ref[...]
    y = (x * lax.rsqrt(jnp.mean(x * x, axis=-1, keepdims=True) + EPS) * g_ref[...]).astype(o_ref.dtype)
    o_ref[...] = y
    dst = lax.broadcasted_iota(jnp.int32, (tr, tr), 0)
    src = lax.broadcasted_iota(jnp.int32, (tr, tr), 1)
    for dil, s_ref in zip(dils, grouped_refs):
        n = tr // dil
        perm = (src == (dst % n) * dil + dst // n).astype(y.dtype)
        yp = jnp.dot(perm, y, preferred_element_type=jnp.float32).astype(y.dtype)
        for r in range(dil):
            s_ref[r] = yp[r * n:(r + 1) * n]


def _rmsnorm_mix(x, g, dils):
    s, d = x.shape
    tr = _tile(s, 256)
    out_shape = [jax.ShapeDtypeStruct((s, d), jnp.bfloat16)]
    out_specs = [pl.BlockSpec((tr, d), lambda i: (i, 0))]
    for dil in dils:
        out_shape.append(jax.ShapeDtypeStruct((dil, s // dil, d), jnp.bfloat16))
        out_specs.append(pl.BlockSpec((dil, tr // dil, d), lambda i: (0, i, 0)))
    return pl.pallas_call(
        functools.partial(_rmsnorm_mix_kernel, dils=dils, tr=tr),
        out_shape=out_shape,
        grid=(s // tr,),
        in_specs=[pl.BlockSpec((tr, d), lambda i: (i, 0)),
                  pl.BlockSpec((1, d), lambda i: (0, 0))],
        out_specs=out_specs,
        compiler_params=_params(("parallel",)),
        name="rmsnorm_mix",
    )(x, g.reshape(1, d))


def _matmul_kernel(x_ref, w_ref, *rest, epilogue):
    *extra, o_ref = rest
    acc = jnp.dot(x_ref[...], w_ref[...], preferred_element_type=jnp.float32)
    if epilogue == "sigmoid":
        acc = jax.nn.sigmoid(acc)
    elif epilogue == "col_scale":
        acc = acc * extra[0][...]
    elif epilogue == "residual":
        acc = extra[0][...] + acc
    o_ref[...] = acc.astype(o_ref.dtype)


def _matmul(x, w, layer, col_off, ncols, out_dtype, epilogue="none", residual=None, col_scale=None,
            name="matmul"):
    m, k = x.shape
    tm = _tile(m, 1024)
    tn = _tile(math.gcd(ncols, col_off) if col_off else ncols, 1024)
    off = col_off // tn
    in_specs = [pl.BlockSpec((tm, k), lambda i, j: (i, 0)),
                pl.BlockSpec((None, k, tn), lambda i, j: (layer, 0, off + j))]
    args = [x, w]
    if col_scale is not None:
        epilogue = "col_scale"
        in_specs.append(pl.BlockSpec((1, tn), lambda i, j: (0, j)))
        args.append(col_scale)
    elif residual is not None:
        epilogue = "residual"
        in_specs.append(pl.BlockSpec((tm, tn), lambda i, j: (i, j)))
        args.append(residual)
    return pl.pallas_call(
        functools.partial(_matmul_kernel, epilogue=epilogue),
        out_shape=jax.ShapeDtypeStruct((m, ncols), out_dtype),
        grid=(m // tm, ncols // tn),
        in_specs=in_specs,
        out_specs=pl.BlockSpec((tm, tn), lambda i, j: (i, j)),
        compiler_params=_params(("parallel", "parallel")),
        name=name,
    )(*args)


def _merge_kernel(oa_ref, ob_ref, oc_ref, wa_ref, wb_ref, wc_ref, ga_ref, gb_ref, gc_ref, o_ref):
    f32 = jnp.float32
    ya = jnp.dot(oa_ref[...], wa_ref[...], preferred_element_type=f32)
    yb = jnp.dot(ob_ref[...], wb_ref[...], preferred_element_type=f32)
    yc = jnp.dot(oc_ref[...], wc_ref[...], preferred_element_type=f32)
    merged = (ga_ref[...].astype(f32) * ya + gb_ref[...].astype(f32) * yb
              + gc_ref[...].astype(f32) * yc)
    o_ref[...] = merged.astype(o_ref.dtype)


def _merge(o_a, o_b, o_c, gates, w_a, w_b, w_c, layer):
    s = o_a.shape[0]
    d = w_a.shape[-1]
    tm = _tile(s, 1024)
    tn = _tile(d, 1024)
    nj = d // tn
    row = lambda width: pl.BlockSpec((tm, width), lambda i, j: (i, 0))
    wspec = lambda kk: pl.BlockSpec((None, kk, tn), lambda i, j: (layer, 0, j))
    gspec = lambda b: pl.BlockSpec((tm, tn), lambda i, j: (i, b * nj + j))
    return pl.pallas_call(
        _merge_kernel,
        out_shape=jax.ShapeDtypeStruct((s, d), jnp.bfloat16),
        grid=(s // tm, nj),
        in_specs=[row(A_OUT), row(B_OUT), row(C_OUT),
                  wspec(A_OUT), wspec(B_OUT), wspec(C_OUT),
                  gspec(0), gspec(1), gspec(2)],
        out_specs=pl.BlockSpec((tm, tn), lambda i, j: (i, j)),
        compiler_params=_params(("parallel", "parallel")),
        name="merge",
    )(o_a, o_b, o_c, w_a, w_b, w_c, gates, gates, gates)


NORM_ROWS = 64
MLP_TF = 512
MLP_UNROLL = 4


def _mlp_kernel(h_ref, gin_ref, wu_hbm, wd_hbm, *rest, layer, tf, nf, final_norm):
    *gout_ref, o_ref, v_ref, wu_buf, wd_buf, sem = rest
    i = pl.program_id(0)
    not_last_tile = i < pl.num_programs(0) - 1

    def copies(c, slot):
        span = pl.ds(pl.multiple_of(c * tf, tf), tf)
        return (pltpu.make_async_copy(wu_hbm.at[layer, :, span], wu_buf.at[slot], sem.at[0, slot]),
                pltpu.make_async_copy(wd_hbm.at[layer, span, :], wd_buf.at[slot], sem.at[1, slot]))

    def start(c, slot):
        for cp in copies(c, slot):
            cp.start()

    def stage(f, f_static, par):
        for cp in copies(f if f_static is None else f_static, par):
            cp.wait()
        if f_static is not None and f_static + 1 >= nf:
            @pl.when(not_last_tile)
            def _():
                start(0, 1 - par)
        else:
            start(f + 1, 1 - par)
        a = jnp.dot(v_ref[...], wu_buf[par], preferred_element_type=jnp.float32)
        a = jnp.square(jnp.maximum(a, 0.0)).astype(jnp.bfloat16)
        o_ref[...] += jnp.dot(a, wd_buf[par], preferred_element_type=jnp.float32)

    def norm_rows(src_ref, dst_ref, g_ref):
        def step(c, carry):
            rows = pl.ds(pl.multiple_of(c * NORM_ROWS, NORM_ROWS), NORM_ROWS)
            x = src_ref[rows, :]
            y = x * lax.rsqrt(jnp.mean(x * x, axis=-1, keepdims=True) + EPS) * g_ref[...]
            dst_ref[rows, :] = y.astype(dst_ref.dtype)
            return carry
        lax.fori_loop(0, src_ref.shape[0] // NORM_ROWS, step, 0)

    @pl.when(i == 0)
    def _():
        start(0, 0)

    o_ref[...] = h_ref[...]
    norm_rows(h_ref, v_ref, gin_ref)

    n_loop = (nf - 1) // MLP_UNROLL * MLP_UNROLL

    def body(it, carry):
        for u in range(MLP_UNROLL):
            stage(it * MLP_UNROLL + u, None, u % 2)
        return carry

    lax.fori_loop(0, n_loop // MLP_UNROLL, body, 0)
    for f in range(n_loop, nf):
        stage(f, f, f % 2)

    if final_norm:
        norm_rows(o_ref, o_ref, gout_ref[0])


def _mlp(h, g_in, w_up, w_down, layer, final_g=None):
    s, d = h.shape
    dff = w_up.shape[-1]
    tm = _tile(s, 512)
    tf = _tile(dff, MLP_TF)
    nf = dff // tf
    assert nf >= 2 and nf % 2 == 0 and MLP_UNROLL % 2 == 0, (dff, tf)
    vec = pl.BlockSpec((1, d), lambda i: (0, 0))
    hbm = pl.BlockSpec(memory_space=pl.ANY)
    in_specs = [pl.BlockSpec((tm, d), lambda i: (i, 0)), vec, hbm, hbm]
    args = [h, g_in.reshape(1, d), w_up, w_down]
    if final_g is not None:
        in_specs.append(vec)
        args.append(final_g.reshape(1, d))
    return pl.pallas_call(
        functools.partial(_mlp_kernel, layer=layer, tf=tf, nf=nf, final_norm=final_g is not None),
        out_shape=jax.ShapeDtypeStruct((s, d), jnp.float32),
        grid=(s // tm,),
        in_specs=in_specs,
        out_specs=pl.BlockSpec((tm, d), lambda i: (i, 0)),
        scratch_shapes=[pltpu.VMEM((tm, d), jnp.bfloat16),
                        pltpu.VMEM((2, d, tf), jnp.bfloat16),
                        pltpu.VMEM((2, tf, d), jnp.bfloat16),
                        pltpu.SemaphoreType.DMA((2, 2))],
        compiler_params=_params(("arbitrary",)),
        name="mlp",
    )(*args)


A_TQ = 512
A_SUB = 128
A_BLK = 64
LOG2E = 1.4426950408889634
FAR = 3e32


def _halo_window(p_ref, m_ref, n_ref, cols, r0, nk, halo, tq):
    lo, hi = r0, r0 + nk
    pieces = []
    if lo < halo:
        pieces.append(p_ref[lo:min(hi, halo), cols])
    a, b = max(lo, halo), min(hi, halo + tq)
    if a < b:
        pieces.append(m_ref[a - halo:b - halo, cols])
    a = max(lo, halo + tq)
    if a < hi:
        pieces.append(n_ref[a - halo - tq:hi - halo - tq, cols])
    return pieces[0] if len(pieces) == 1 else jnp.concatenate(pieces, axis=0)


def _band_distance(i, last, sub, nsub_rows, halo, tq):
    nk = nsub_rows + 2 * halo
    a = lax.broadcasted_iota(jnp.int32, (nsub_rows, nk), 0)
    c = lax.broadcasted_iota(jnp.int32, (nsub_rows, nk), 1)
    dist = jnp.abs(c - halo - a)
    row = c + sub * nsub_rows
    valid = (dist <= halo) & ((row >= halo) | (i > 0)) & ((row < tq + halo) | (i < last))
    return jnp.where(valid, dist.astype(jnp.float32), FAR)


def _attn_a_kernel(q_ref, kp_ref, km_ref, kn_ref, vp_ref, vm_ref, vn_ref, o_ref, lse_ref, *, dil, tq):
    i = pl.program_id(1)
    last = pl.num_programs(1) - 1
    nsub = tq // A_SUB if tq >= A_SUB else 1
    rows = tq // nsub
    nk = rows + 2 * A_BLK
    lane = lax.broadcasted_iota(jnp.int32, (rows, LANE), 1)
    slopes = [LOG2E * dil * sl for sl in _alibi_slopes(A_HEADS)]
    probs = [(sub, h) for sub in range(nsub) for h in range(A_HEADS)]
    cols = lambda h: slice(h * HEAD_DIM, (h + 1) * HEAD_DIM)
    dists = [_band_distance(i, last, sub, rows, A_BLK, tq) for sub in range(nsub)]
    scores = []
    for sub, h in probs:
        k = _halo_window(kp_ref, km_ref, kn_ref, cols(h), sub * rows, nk, A_BLK, tq)
        s = lax.dot_general(q_ref[sub * rows:(sub + 1) * rows, cols(h)], k, (((1,), (1,)), ((), ())),
                            preferred_element_type=jnp.float32)
        scores.append(s - slopes[h] * dists[sub])
    es, denoms = [], []
    lse_tiles = [jnp.zeros((rows, LANE), jnp.float32) for _ in range(nsub)]
    for (sub, h), s in zip(probs, scores):
        m = jnp.max(s, axis=-1, keepdims=True)
        e = jnp.exp2(s - m)
        denom = jnp.sum(e, axis=-1, keepdims=True)
        es.append(e.astype(jnp.bfloat16))
        denoms.append(denom)
        lse_tiles[sub] = jnp.where(lane == h, m + jnp.log2(denom), lse_tiles[sub])
    outs = []
    for (sub, h), e, denom in zip(probs, es, denoms):
        v = _halo_window(vp_ref, vm_ref, vn_ref, cols(h), sub * rows, nk, A_BLK, tq)
        outs.append(jnp.dot(e, v, preferred_element_type=jnp.float32) / denom)
    o_rows = [jnp.concatenate(outs[sub * A_HEADS:(sub + 1) * A_HEADS], axis=1) for sub in range(nsub)]
    o_ref[...] = o_rows[0] if nsub == 1 else jnp.concatenate(o_rows, axis=0)
    lse_ref[...] = lse_tiles[0] if nsub == 1 else jnp.concatenate(lse_tiles, axis=0)


def _attn_a_pattern(pa, g):
    dil, ls, _ = pa.shape
    tq = _tile(ls, A_TQ)
    r64 = tq // A_BLK
    nblk64 = ls // A_BLK

    def main(c):
        return pl.BlockSpec((None, tq, A_OUT), lambda r, i: (r, i, c))

    def prev(c):
        return pl.BlockSpec((None, A_BLK, A_OUT), lambda r, i: (r, jnp.maximum(i * r64 - 1, 0), c))

    def nxt(c):
        return pl.BlockSpec((None, A_BLK, A_OUT),
                            lambda r, i: (r, jnp.minimum((i + 1) * r64, nblk64 - 1), c))

    return pl.pallas_call(
        functools.partial(_attn_a_kernel, dil=dil, tq=tq),
        out_shape=(jax.ShapeDtypeStruct((dil, ls, A_OUT), jnp.float32),
                   jax.ShapeDtypeStruct((dil, ls, LANE), jnp.float32)),
        grid=(dil, ls // tq),
        in_specs=[main(0), prev(1), main(1), nxt(1), prev(2), main(2), nxt(2)],
        out_specs=(pl.BlockSpec((None, tq, A_OUT), lambda r, i: (r, i, 0)),
                   pl.BlockSpec((None, tq, LANE), lambda r, i: (r, i, 0))),
        compiler_params=_params(("parallel", "parallel")),
        name=f"attn_a{g}",
    )(pa, pa, pa, pa, pa, pa, pa)


def _combine_a_kernel(o0_ref, o1_ref, o2_ref, l0_ref, l1_ref, l2_ref, o_ref, *scratch, dils, tr):
    o_in, l_in = [o0_ref, o1_ref, o2_ref], [l0_ref, l1_ref, l2_ref]
    nat = [scratch[g] if dil > 1 else None for g, dil in enumerate(dils)]

    def ungroup(g, src):
        dil = dils[g]
        if dil == 1:
            return src(0)
        for r in range(dil):
            nat[g][pl.ds(r, tr // dil, stride=dil), :] = src(r)
        return nat[g][...]

    l0, l1, l2 = (ungroup(g, lambda r, g=g: l_in[g][r]) for g in range(len(dils)))
    mx = jnp.maximum(jnp.maximum(l0, l1), l2)
    e0, e1, e2 = jnp.exp2(l0 - mx), jnp.exp2(l1 - mx), jnp.exp2(l2 - mx)
    inv = 1.0 / (e0 + e1 + e2)
    w = [e0 * inv, e1 * inv, e2 * inv]
    for h in range(A_HEADS):
        sl = slice(h * HEAD_DIM, (h + 1) * HEAD_DIM)
        o = sum(w[g][:, h:h + 1] * ungroup(g, lambda r, g=g: o_in[g][r, :, sl]) for g in range(len(dils)))
        o_ref[:, sl] = o.astype(o_ref.dtype)


def _attn_a(pas):
    dils = tuple(p.shape[0] for p in pas)
    s = pas[0].shape[0] * pas[0].shape[1]
    outs = [_attn_a_pattern(p, g) for g, p in enumerate(pas)]
    tr = _tile(s, 512)
    ospec = lambda dil: pl.BlockSpec((dil, tr // dil, A_OUT), lambda i: (0, i, 0))
    lspec = lambda dil: pl.BlockSpec((dil, tr // dil, LANE), lambda i: (0, i, 0))
    scratch = [pltpu.VMEM((tr, LANE), jnp.float32) for _ in dils]
    return pl.pallas_call(
        functools.partial(_combine_a_kernel, dils=dils, tr=tr),
        out_shape=jax.ShapeDtypeStruct((s, A_OUT), jnp.bfloat16),
        grid=(s // tr,),
        in_specs=[ospec(d) for d in dils] + [lspec(d) for d in dils],
        out_specs=pl.BlockSpec((tr, A_OUT), lambda i: (i, 0)),
        scratch_shapes=scratch,
        compiler_params=_params(("parallel",)),
        name="combine_a",
    )(*[o for o, _ in outs], *[l for _, l in outs])


B_TQ = 512


def _attn_b_kernel(slope_ref, sink_ref, q_ref, kp_ref, km_ref, kn_ref, vp_ref, vm_ref, vn_ref, o_ref, *, tq):
    c = pl.program_id(0)
    i = pl.program_id(1)
    last = pl.num_programs(1) - 1
    nsub = tq // B_HALF
    nk = 3 * B_HALF
    allc = slice(None)
    probs = [(sub, g) for sub in range(nsub) for g in range(B_GROUP)]
    cols = lambda g: slice(g * HEAD_DIM, (g + 1) * HEAD_DIM)
    slopes = [slope_ref[c * B_GROUP + g] for g in range(B_GROUP)]
    sinks = [sink_ref[c * B_GROUP + g] * LOG2E for g in range(B_GROUP)]
    dists = [_band_distance(i, last, sub, B_HALF, B_HALF, tq) for sub in range(nsub)]
    ks = [_halo_window(kp_ref, km_ref, kn_ref, allc, sub * B_HALF, nk, B_HALF, tq) for sub in range(nsub)]
    scores = []
    for sub, g in probs:
        s = lax.dot_general(q_ref[sub * B_HALF:(sub + 1) * B_HALF, cols(g)], ks[sub],
                            (((1,), (1,)), ((), ())), preferred_element_type=jnp.float32)
        scores.append(s - slopes[g] * dists[sub])
    es, denoms = [], []
    for (sub, g), s in zip(probs, scores):
        m = jnp.maximum(jnp.max(s, axis=-1, keepdims=True), sinks[g])
        e = jnp.exp2(s - m)
        es.append(e.astype(jnp.bfloat16))
        denoms.append(jnp.sum(e, axis=-1, keepdims=True) + jnp.exp2(sinks[g] - m))
    vs = [_halo_window(vp_ref, vm_ref, vn_ref, allc, sub * B_HALF, nk, B_HALF, tq) for sub in range(nsub)]
    outs = [(jnp.dot(e, vs[sub], preferred_element_type=jnp.float32) / denom).astype(o_ref.dtype)
            for (sub, g), e, denom in zip(probs, es, denoms)]
    o_rows = [jnp.concatenate(outs[sub * B_GROUP:(sub + 1) * B_GROUP], axis=1) for sub in range(nsub)]
    o_ref[...] = o_rows[0] if nsub == 1 else jnp.concatenate(o_rows, axis=0)


def _attn_b(qkv, sinks):
    s, nc = qkv.shape
    tq = _tile(s, B_TQ)
    r128 = tq // B_HALF
    nblk = s // B_HALF
    gw = B_GROUP * HEAD_DIM
    cq0 = 0
    ck0 = B_HEADS
    cv0 = ck0 + B_KV_HEADS
    smem = pl.BlockSpec(memory_space=pltpu.SMEM)

    def main(c0):
        return pl.BlockSpec((tq, HEAD_DIM), lambda c, i: (i, c0 + c))

    def prev(c0):
        return pl.BlockSpec((B_HALF, HEAD_DIM), lambda c, i: (jnp.maximum(i * r128 - 1, 0), c0 + c))

    def nxt(c0):
        return pl.BlockSpec((B_HALF, HEAD_DIM),
                            lambda c, i: (jnp.minimum((i + 1) * r128, nblk - 1), c0 + c))

    slopes = jnp.asarray([LOG2E * sl for sl in _alibi_slopes(B_HEADS)], jnp.float32)
    return pl.pallas_call(
        functools.partial(_attn_b_kernel, tq=tq),
        out_shape=jax.ShapeDtypeStruct((s, B_OUT), jnp.bfloat16),
        grid=(B_KV_HEADS, s // tq),
        in_specs=[smem, smem,
                  pl.BlockSpec((tq, gw), lambda c, i: (i, cq0 + c)),
                  prev(ck0), main(ck0), nxt(ck0), prev(cv0), main(cv0), nxt(cv0)],
        out_specs=pl.BlockSpec((tq, gw), lambda c, i: (i, c)),
        compiler_params=_params(("parallel", "parallel")),
        name="attn_b",
    )(slopes, sinks.astype(jnp.float32), qkv, qkv, qkv, qkv, qkv, qkv, qkv)


C_TQ = 512
C_TK = 512
C_AUG = 16
C_SUMROWS = 16
C_UNDERFLOW = 152.0
C_UNROLL = 4


def _attn_c_kernel(slope_ref, inv_ref, q_ref, k_ref, v_ref, lq1_ref, lk1_ref, lq2_ref, lk2_ref, g_ref, *rest,
                   tq, tk, nkv, lam_init, casts):
    f32, bf16 = jnp.float32, jnp.bfloat16
    ncast = len(casts)
    cast_in, o_ref, cast_out = rest[:ncast], rest[ncast], rest[ncast + 1:2 * ncast + 1]
    vt_ref, qzt_ref, m_ref, acc_ref, s0_ref, s1_ref, mx0_ref, mx1_ref, dbias_ref, kn2_ref = rest[2 * ncast + 1:]
    h = pl.program_id(0)
    i = pl.program_id(1)
    slope = slope_ref[h]
    half = tk // 2
    lane_k = lax.broadcasted_iota(jnp.int32, (tk, 2 * C_QK_DIM), 1)

    step = h * pl.num_programs(1) + i
    for (start, nblk), w_ref, wb_ref in zip(casts, cast_in, cast_out):
        @pl.when((step >= start) & (step < start + nblk))
        def _():
            wb_ref[...] = w_ref[...].astype(wb_ref.dtype)

    @pl.when(i == 0)
    def _():
        def tbody(c, kn2):
            st = pl.multiple_of(c * tk, tk)
            vt = v_ref[pl.ds(st, tk), :].astype(f32).T.astype(bf16)
            ones_row = lax.broadcasted_iota(jnp.int32, (C_SUMROWS, tk), 0) == 0
            vt_ref[c] = jnp.concatenate([vt, ones_row.astype(bf16)], axis=0)
            kk = k_ref[pl.ds(st, tk), :].astype(f32)
            sq = kk * kk
            n_all = jnp.sum(sq, axis=1, keepdims=True)
            n_0 = jnp.sum(jnp.where(lane_k < C_QK_DIM, sq, 0.0), axis=1, keepdims=True)
            return jnp.maximum(kn2, jnp.max(jnp.maximum(n_0, n_all - n_0)))
        kn2_ref[0] = lax.fori_loop(0, nkv, tbody, jnp.float32(0.0))
        if tq == tk:
            rel0 = (lax.broadcasted_iota(jnp.int32, (tk, tq), 0) - lax.broadcasted_iota(jnp.int32, (tk, tq), 1))
            dbias_ref[...] = slope * jnp.abs(rel0).astype(f32)

    qt = q_ref[...].astype(f32).T
    row = lax.broadcasted_iota(jnp.int32, qt.shape, 0)
    qzt_ref[...] = jnp.concatenate([jnp.where(row < C_QK_DIM, qt, 0.0), jnp.where(row >= C_QK_DIM, qt, 0.0)],
                                   axis=1).astype(bf16)
    m_ref[...] = jnp.full(m_ref.shape, NEG, f32)
    acc_ref[...] = jnp.zeros(acc_ref.shape, f32)

    def chunk_at(t):
        j = jlo + t - 1
        return jnp.where(t == 0, jd, j + (j >= jd).astype(jnp.int32))

    def score_stage(t, s_ref, mx_ref):
        j = chunk_at(t)
        dc = (j * tk + half - i * tq).astype(f32)
        coef = jnp.where(j > jd, -slope, slope)
        base = jnp.where(rowq < 3, coef, -coef * (a_q - dc))
        p1 = base.astype(bf16)
        r1 = base - p1.astype(f32)
        p2 = r1.astype(bf16)
        p3 = (r1 - p2.astype(f32)).astype(bf16)
        piece = rowq % 3
        aug_q = jnp.where(rowq < 6, jnp.where(piece == 0, p1, jnp.where(piece == 1, p2, p3)),
                          jnp.zeros_like(p1))
        kc = k_ref[pl.ds(pl.multiple_of(j * tk, tk), tk), :]
        lhs = jnp.concatenate([kc, aug_k], axis=1)
        rhs = jnp.concatenate([qzt_ref[...], aug_q, zpad], axis=0)
        st = jnp.dot(lhs, rhs, preferred_element_type=f32)
        s_ref[...] = st
        mx_ref[...] = jnp.max(st, axis=0, keepdims=True)

    def softmax_stage(t, s_ref, mx_ref):
        m_old = m_ref[...]
        m_new = jnp.maximum(m_old, mx_ref[...])
        alpha = jnp.exp2(m_old - m_new)
        p = jnp.exp2(s_ref[...] - m_new).astype(bf16)
        acc_ref[...] = alpha * acc_ref[...] + jnp.dot(vt_ref[chunk_at(t)], p, preferred_element_type=f32)
        m_ref[...] = m_new

    colk = lax.broadcasted_iota(jnp.int32, (tk, LANE), 1)
    bk = (lax.broadcasted_iota(jnp.int32, (tk, LANE), 0) - half).astype(f32)
    aug_k = jnp.where(colk < 3, bk, jnp.where(colk < 6, 1.0, 0.0)).astype(bf16)
    rowq = lax.broadcasted_iota(jnp.int32, (C_AUG, 2 * tq), 0)
    a_q = (lax.broadcasted_iota(jnp.int32, (C_AUG, 2 * tq), 1) % tq).astype(f32)
    zpad = jnp.zeros((2 * LANE - 2 * C_QK_DIM - C_AUG, 2 * tq), bf16)

    jd = (i * tq) // tk
    kd = k_ref[pl.ds(pl.multiple_of(jd * tk, tk), tk), :]
    sd = jnp.dot(kd, qzt_ref[...], preferred_element_type=f32)
    if tq == tk:
        bias = dbias_ref[...]
    else:
        rel = (lax.broadcasted_iota(jnp.int32, (tk, tq), 0) - lax.broadcasted_iota(jnp.int32, (tk, tq), 1)
               + (jd * tk - i * tq))
        bias = slope * jnp.abs(rel).astype(f32)
    sd = sd - jnp.concatenate([bias, bias], axis=1)
    s0_ref[...] = sd
    mxd = jnp.max(sd, axis=0, keepdims=True)
    mx0_ref[...] = mxd

    sqq = qt * qt
    n_all = jnp.sum(sqq, axis=0, keepdims=True)
    n_0 = jnp.sum(jnp.where(row < C_QK_DIM, sqq, 0.0), axis=0, keepdims=True)
    s_max = jnp.max(jnp.sqrt(jnp.maximum(n_0, n_all - n_0) * kn2_ref[0]))
    reach = (s_max + C_UNDERFLOW - jnp.min(mxd)) * inv_ref[h]
    wnd = jnp.minimum(reach, float(nkv)).astype(jnp.int32) + 1
    jlo = jnp.maximum(jd - wnd, 0)
    jhi = jnp.minimum(jd + wnd, nkv - 1)
    nvis = jhi - jlo + 1

    def pair(u):
        score_stage(2 * u + 1, s1_ref, mx1_ref)
        softmax_stage(2 * u, s0_ref, mx0_ref)
        score_stage(2 * u + 2, s0_ref, mx0_ref)
        softmax_stage(2 * u + 1, s1_ref, mx1_ref)

    def body(w, carry):
        for u in range(C_UNROLL):
            pair(C_UNROLL * w + u)
        return carry

    npair = (nvis - 1) // 2
    nbody = npair // C_UNROLL
    lax.fori_loop(0, nbody, body, 0)
    done = nbody * C_UNROLL
    width = C_UNROLL // 2
    while width:
        take = ((npair - done) // width) % 2 == 1

        @pl.when(take)
        def _(done=done, width=width):
            for u in range(width):
                pair(done + u)

        done = done + take.astype(jnp.int32) * width
        width //= 2

    @pl.when(nvis % 2 == 0)
    def _():
        score_stage(nvis - 1, s1_ref, mx1_ref)
        softmax_stage(nvis - 2, s0_ref, mx0_ref)
        softmax_stage(nvis - 1, s1_ref, mx1_ref)

    @pl.when(nvis % 2 == 1)
    def _():
        softmax_stage(nvis - 1, s0_ref, mx0_ref)

    lam = (jnp.exp(jnp.sum(lq1_ref[...] * lk1_ref[...], axis=-1, keepdims=True))
           - jnp.exp(jnp.sum(lq2_ref[...] * lk2_ref[...], axis=-1, keepdims=True)) + lam_init)
    o = acc_ref[:C_V_DIM, :] / acc_ref[C_V_DIM:C_V_DIM + 1, :]
    o = o[:, :tq] - lam * o[:, tq:]
    y = o * lax.rsqrt(jnp.mean(o * o, axis=0, keepdims=True) + EPS)
    o_ref[...] = (y * g_ref[...] * (1.0 - lam_init)).T.astype(o_ref.dtype)


CAST_BLOCK_BYTES = 2 * 1024 * 1024


def _cast_jobs(weights, nsteps):
    jobs, start = [], 0
    for w, layer in weights:
        _, r, c = w.shape
        rows = max(16, _tile(r, max(16, CAST_BLOCK_BYTES // (4 * c))))
        nblk = r // rows
        assert r % rows == 0 and nblk <= nsteps, (w.shape, rows, nsteps)
        if start + nblk > nsteps:
            start = 0
        jobs.append((w, layer, rows, nblk, start))
        start += nblk
    return jobs


def _attn_c(qkv, lq1, lk1, lq2, lk2, norm_g, lam_init, cast_weights=()):
    s, nc = qkv.shape
    tq = _tile(s, C_TQ)
    tk = _tile(s, C_TK)
    nkv = s // tk
    assert tk % tq == 0, (s, tq, tk)
    nq = s // tq
    cq0 = B_COLS // HEAD_DIM
    ck0 = cq0 + C_HEADS
    cv0 = ck0 + C_HEADS
    smem = pl.BlockSpec(memory_space=pltpu.SMEM)
    vec = lambda n: pl.BlockSpec((1, n), lambda h, i: (0, 0))
    slopes_l2 = [LOG2E * sl for sl in _alibi_slopes(C_HEADS)]
    slopes = jnp.asarray(slopes_l2, jnp.float32)
    inv_reach = jnp.asarray([1.0 / (sl * tk) for sl in slopes_l2], jnp.float32)
    jobs = _cast_jobs(cast_weights, C_HEADS * nq)

    def job_block(start, nblk):
        return lambda h, i: jnp.clip(h * nq + i - start, 0, nblk - 1)

    cast_in_specs = [pl.BlockSpec((None, rows, w.shape[2]),
                                  lambda h, i, layer=layer, blk=job_block(start, nblk): (layer, blk(h, i), 0))
                     for w, layer, rows, nblk, start in jobs]
    cast_out_specs = [pl.BlockSpec((None, rows, w.shape[2]),
                                   lambda h, i, blk=job_block(start, nblk): (0, blk(h, i), 0))
                      for w, layer, rows, nblk, start in jobs]
    cast_out_shapes = [jax.ShapeDtypeStruct((1,) + w.shape[1:], jnp.bfloat16) for w, *_ in jobs]
    outs = pl.pallas_call(
        functools.partial(_attn_c_kernel, tq=tq, tk=tk, nkv=nkv, lam_init=lam_init,
                          casts=tuple((start, nblk) for *_, nblk, start in jobs)),
        out_shape=[jax.ShapeDtypeStruct((s, C_OUT), jnp.bfloat16)] + cast_out_shapes,
        grid=(C_HEADS, nq),
        in_specs=[smem, smem,
                  pl.BlockSpec((tq, HEAD_DIM), lambda h, i: (i, cq0 + h)),
                  pl.BlockSpec((s, HEAD_DIM), lambda h, i: (0, ck0 + h)),
                  pl.BlockSpec((s, HEAD_DIM), lambda h, i: (0, cv0 + h)),
                  vec(C_QK_DIM), vec(C_QK_DIM), vec(C_QK_DIM), vec(C_QK_DIM),
                  pl.BlockSpec((C_V_DIM, 1), lambda h, i: (0, 0))] + cast_in_specs,
        out_specs=[pl.BlockSpec((tq, C_V_DIM), lambda h, i: (i, h))] + cast_out_specs,
        scratch_shapes=[pltpu.VMEM((nkv, C_V_DIM + C_SUMROWS, tk), jnp.bfloat16),
                        pltpu.VMEM((2 * C_QK_DIM, 2 * tq), jnp.bfloat16),
                        pltpu.VMEM((1, 2 * tq), jnp.float32),
                        pltpu.VMEM((C_V_DIM + C_SUMROWS, 2 * tq), jnp.float32),
                        pltpu.VMEM((tk, 2 * tq), jnp.float32),
                        pltpu.VMEM((tk, 2 * tq), jnp.float32),
                        pltpu.VMEM((1, 2 * tq), jnp.float32),
                        pltpu.VMEM((1, 2 * tq), jnp.float32),
                        pltpu.VMEM((tk, tq) if tq == tk else (8, LANE), jnp.float32),
                        pltpu.SMEM((1,), jnp.float32)],
        compiler_params=_params(("arbitrary", "arbitrary")),
        name="attn_c",
    )(slopes, inv_reach, qkv, qkv, qkv, lq1.reshape(1, -1), lk1.reshape(1, -1), lq2.reshape(1, -1),
      lk2.reshape(1, -1), norm_g.reshape(-1, 1), *[w for w, *_ in jobs])
    return outs[0], list(outs[1:])


def kernel(x, mix_norm_g, w_in, b_sink, diff_lq1, diff_lk1, diff_lq2, diff_lk2, diff_norm_g,
           w_branch_a, w_branch_b, w_branch_c, w_out, mlp_norm_g, w_up, w_down, final_norm_g):
    bsz, seq, d = x.shape
    depth = w_in.shape[0]
    bf16 = jnp.bfloat16
    w_a_b, w_b_b, w_c_b = (w.astype(bf16) for w in (w_branch_a, w_branch_b, w_branch_c))
    w_in_first = w_in[:1].astype(bf16)
    col = jnp.arange(B_COLS + C_COLS)
    bc_scale = jnp.where(col < B_HEADS * HEAD_DIM, HEAD_DIM ** -0.5 * LOG2E,
                         jnp.where((col >= B_COLS) & (col < B_COLS + C_HEADS * 2 * C_QK_DIM),
                                   C_QK_DIM ** -0.5 * LOG2E, 1.0)).astype(jnp.float32).reshape(1, -1)
    dils = tuple(dil for _, dil in A_PATTERNS)
    pat_cols = 3 * A_OUT
    a_scale = jnp.where(jnp.arange(pat_cols) < A_OUT, HEAD_DIM ** -0.5 * LOG2E,
                        1.0).astype(jnp.float32).reshape(1, -1)
    xs = x.reshape(bsz * seq, d)
    outs = []
    for b in range(bsz):
        h = xs[b * seq:(b + 1) * seq]
        w_in_l = w_in_first
        for l in range(depth):
            u, *u_dil = _rmsnorm_mix(h, mix_norm_g[l], tuple(dl for dl in dils if dl > 1))
            u_by_dil = {1: u, **{dl: ud.reshape(seq, d) for dl, ud in zip([dl for dl in dils if dl > 1], u_dil)}}
            pas = [_matmul(u_by_dil[dl], w_in_l, 0, g * pat_cols, pat_cols, bf16, col_scale=a_scale,
                           name=f"proj_a{g}").reshape(dl, seq // dl, pat_cols)
                   for g, dl in enumerate(dils)]
            pbc = _matmul(u, w_in_l, 0, OFF_B, B_COLS + C_COLS, bf16, col_scale=bc_scale, name="proj_bc")
            gates = _matmul(u, w_in_l, 0, OFF_G, N_BRANCH * d, bf16, epilogue="sigmoid", name="proj_gates")
            o_a = _attn_a(pas)
            o_b = _attn_b(pbc, b_sink[l])
            lam_init = 0.8 - 0.6 * math.exp(-0.3 * l)
            todo = [(w_up, l), (w_down, l), (w_out, l)] + ([(w_in, l + 1)] if l + 1 < depth else [])
            o_c, cast = _attn_c(pbc, diff_lq1[l], diff_lk1[l], diff_lq2[l], diff_lk2[l], diff_norm_g[l],
                                lam_init, cast_weights=todo)
            w_up_l, w_down_l, w_out_l = cast[:3]
            w_in_l = cast[3] if l + 1 < depth else None
            merged = _merge(o_a, o_b, o_c, gates, w_a_b, w_b_b, w_c_b, l)
            h = _matmul(merged, w_out_l, 0, 0, d, jnp.float32, residual=h, name="out_proj")
            h = _mlp(h, mlp_norm_g[l], w_up_l, w_down_l, 0, final_g=final_norm_g if l == depth - 1 else None)
        outs.append(h.astype(x.dtype))
    return jnp.concatenate(outs, axis=0).reshape(bsz, seq, d)
```
